```python
import jax, jax.numpy as jnp
from jax import lax
import numpy as np

D_MODEL = 4096
BATCH = 8
SEQ = 4096
DEPTH = 1

D_MIX = D_MODEL
RWKV_W = D_MIX // 2
CONV_W = D_MIX - RWKV_W
HEAD_SIZE = 64
N_HEADS = RWKV_W // HEAD_SIZE
LORA_W = 96
LORA_A = 96
CONV_K = 31
NORM_EPS = 1e-6
LN_EPS = 1e-5
GN_EPS = 1e-5 * HEAD_SIZE
SHIFT_COLS = 3 * RWKV_W + LORA_W + LORA_A
IN_COLS = SHIFT_COLS + RWKV_W + 2 * CONV_W + CONV_W

kernel_name = "hybrid_rwkv7_conformer_parallel"


def rms_norm(x, g):
    xf = x.astype(jnp.float32)
    y = xf * lax.rsqrt(jnp.mean(xf * xf, axis=-1, keepdims=True) + NORM_EPS)
    return (y * g.astype(jnp.float32)).astype(x.dtype)


def layer_norm(x, g, b):
    xf = x.astype(jnp.float32)
    mu = jnp.mean(xf, axis=-1, keepdims=True)
    var = jnp.mean(jnp.square(xf - mu), axis=-1, keepdims=True)
    y = (xf - mu) * lax.rsqrt(var + LN_EPS)
    return (y * g.astype(jnp.float32) + b.astype(jnp.float32)).astype(x.dtype)


def rwkv7_recurrence(r, w, k, v, kk, a):
    B, T, H, N = r.shape
    tm = lambda z: jnp.moveaxis(z, 1, 0)
    seq = (tm(r), tm(w), tm(k), tm(v), tm(kk), tm(kk * a))

    def step(S, inp):
        r_t, w_t, k_t, v_t, kk_t, b_t = inp
        sa = jnp.einsum('bhij,bhj->bhi', S, -kk_t)
        S = S * w_t[:, :, None, :] + sa[..., None] * b_t[:, :, None, :] + v_t[..., None] * k_t[:, :, None, :]
        y_t = jnp.einsum('bhij,bhj->bhi', S, r_t)
        return S, y_t

    S0 = jnp.zeros((B, H, N, N), jnp.float32)
    _, ys = lax.scan(step, S0, seq)
    return jnp.moveaxis(ys, 0, 1)


def _fwd_setup_inputs(seed: int = 0) -> dict:
    key = jax.random.key(seed)
    ks = jax.random.split(key, 24)
    f32 = jnp.float32
    nrm = lambda k, s, sc: jax.random.normal(k, s, f32) * sc
    return {
        "x": nrm(ks[0], (BATCH, SEQ, D_MODEL), 1.0),
        "norm_pre_g": 1.0 + nrm(ks[1], (D_MODEL,), 0.02),
        "w_in": nrm(ks[2], (D_MODEL, IN_COLS), D_MODEL ** -0.5),
        "mu_shift": jax.random.uniform(ks[3], (SHIFT_COLS,), f32, 0.0, 1.0),
        "w0": jax.random.uniform(ks[4], (RWKV_W,), f32, -5.5, -0.5),
        "w_lora_up": nrm(ks[5], (LORA_W, RWKV_W), 0.5 * LORA_W ** -0.5),
        "a0": nrm(ks[6], (RWKV_W,), 0.1),
        "a_lora_up": nrm(ks[7], (LORA_A, RWKV_W), 0.5 * LORA_A ** -0.5),
        "k_k": 0.85 + nrm(ks[8], (RWKV_W,), 0.02),
        "k_a": 1.0 + nrm(ks[9], (RWKV_W,), 0.02),
        "r_k": nrm(ks[10], (N_HEADS, HEAD_SIZE), 0.1),
        "lnx_g": 1.0 + nrm(ks[11], (RWKV_W,), 0.02),
        "lnx_b": nrm(ks[12], (RWKV_W,), 0.01),
        "conv_w": nrm(ks[13], (CONV_K, CONV_W), CONV_K ** -0.5),
        "conv_b": nrm(ks[14], (CONV_W,), 0.01),
        "cln_g": 1.0 + nrm(ks[15], (CONV_W,), 0.02),
        "cln_b": nrm(ks[16], (CONV_W,), 0.01),
        "w_pw2": nrm(ks[17], (CONV_W, CONV_W), CONV_W ** -0.5),
        "b_pw2": nrm(ks[18], (CONV_W,), 0.01),
        "w_out": nrm(ks[19], (D_MIX, D_MODEL), D_MIX ** -0.5),
        "norm_post_g": 1.0 + nrm(ks[20], (D_MODEL,), 0.02),
    }


def hybrid_layer(x, norm_pre_g, w_in, mu_shift, w0, w_lora_up, a0, a_lora_up, k_k, k_a, r_k,
                 lnx_g, lnx_b, conv_w, conv_b, cln_g, cln_b, w_pw2, b_pw2, w_out, norm_post_g):
    B, T, _ = x.shape
    f32 = jnp.float32
    h = rms_norm(x, norm_pre_g)
    proj = jnp.einsum('btd,dc->btc', h, w_in)
    c0 = SHIFT_COLS
    c1 = c0 + RWKV_W
    c2 = c1 + CONV_W
    c3 = c2 + CONV_W
    rwkv_in, g_rwkv, glu_v, glu_g, g_conv = jnp.split(proj, [c0, c1, c2, c3], axis=-1)

    prev = jnp.pad(rwkv_in, ((0, 0), (1, 0), (0, 0)))[:, :-1]
    xs = rwkv_in + (prev - rwkv_in) * mu_shift
    r, k, v, w_low, a_low = jnp.split(xs, [RWKV_W, 2 * RWKV_W, 3 * RWKV_W, 3 * RWKV_W + LORA_W], axis=-1)
    w_log = -jax.nn.softplus(-(w0 + jnp.tanh(w_low) @ w_lora_up).astype(f32)) - 0.5
    decay = jnp.exp(-jnp.exp(w_log))
    a = jax.nn.sigmoid((a0 + a_low @ a_lora_up).astype(f32))
    hs = lambda z: z.astype(f32).reshape(B, T, N_HEADS, HEAD_SIZE)
    r_h, k_f, v_h, a_h, w_h = hs(r), hs(k), hs(v), hs(a), hs(decay)
    k_k_h = k_k.astype(f32).reshape(N_HEADS, HEAD_SIZE)
    k_a_h = k_a.astype(f32).reshape(N_HEADS, HEAD_SIZE)
    kk = k_f * k_k_h
    kk = kk / jnp.maximum(jnp.linalg.norm(kk, axis=-1, keepdims=True), 1e-12)
    k_h = k_f * (1.0 + (a_h - 1.0) * k_a_h)
    y = rwkv7_recurrence(r_h, w_h, k_h, v_h, kk, a_h)
    mu = jnp.mean(y, axis=-1, keepdims=True)
    var = jnp.mean(jnp.square(y - mu), axis=-1, keepdims=True)
    y = ((y - mu) * lax.rsqrt(var + GN_EPS)).reshape(B, T, RWKV_W)
    y = y * lnx_g.astype(f32) + lnx_b.astype(f32)
    bonus = jnp.sum(r_h * k_h * r_k.astype(f32), axis=-1, keepdims=True) * v_h
    y = (y + bonus.reshape(B, T, RWKV_W)).astype(x.dtype)
    y_rwkv = y * jax.nn.silu(g_rwkv)

    u = glu_v * jax.nn.sigmoid(glu_g)
    u_pad = jnp.pad(u, ((0, 0), (CONV_K - 1, 0), (0, 0)))
    c = lax.conv_general_dilated(u_pad, conv_w[:, None, :].astype(u.dtype), window_strides=(1,),
                                 padding='VALID', dimension_numbers=('NWC', 'WIO', 'NWC'),
                                 feature_group_count=CONV_W) + conv_b
    c = jax.nn.silu(layer_norm(c, cln_g, cln_b))
    c = jnp.einsum('btc,ce->bte', c, w_pw2) + b_pw2
    y_conv = c * jax.nn.silu(g_conv)

    mix = jnp.concatenate([y_rwkv, y_conv], axis=-1)
    out = jnp.einsum('btc,cd->btd', mix, w_out)
    return x + rms_norm(out, norm_post_g)


def _fwd_reference(x, norm_pre_g, w_in, mu_shift, w0, w_lora_up, a0, a_lora_up, k_k, k_a, r_k,
              lnx_g, lnx_b, conv_w, conv_b, cln_g, cln_b, w_pw2, b_pw2, w_out, norm_post_g):
    for _ in range(DEPTH):
        x = hybrid_layer(x, norm_pre_g, w_in, mu_shift, w0, w_lora_up, a0, a_lora_up, k_k, k_a, r_k,
                         lnx_g, lnx_b, conv_w, conv_b, cln_g, cln_b, w_pw2, b_pw2, w_out, norm_post_g)
    return x


import jax as _jax
import jax.numpy as _jnp

TWIN_FORMAT = 'train_step'
FWD_PARAMS = ['x', 'norm_pre_g', 'w_in', 'mu_shift', 'w0', 'w_lora_up', 'a0', 'a_lora_up', 'k_k', 'k_a', 'r_k', 'lnx_g', 'lnx_b', 'conv_w', 'conv_b', 'cln_g', 'cln_b', 'w_pw2', 'b_pw2', 'w_out', 'norm_post_g']
TWIN_WEIGHTS = ['norm_pre_g', 'w_in', 'mu_shift', 'w0', 'w_lora_up', 'a0', 'a_lora_up', 'k_k', 'k_a', 'r_k', 'lnx_g', 'lnx_b', 'conv_w', 'conv_b', 'cln_g', 'cln_b', 'w_pw2', 'b_pw2', 'w_out', 'norm_post_g']
TWIN_DIFF_INPUT = 'x'
TWIN_INPUTS = ['x', 'norm_pre_g', 'w_in', 'mu_shift', 'w0', 'w_lora_up', 'a0', 'a_lora_up', 'k_k', 'k_a', 'r_k', 'lnx_g', 'lnx_b', 'conv_w', 'conv_b', 'cln_g', 'cln_b', 'w_pw2', 'b_pw2', 'w_out', 'norm_post_g', 'loss_target', 'm_norm_pre_g', 'm_w_in', 'm_mu_shift', 'm_w0', 'm_w_lora_up', 'm_a0', 'm_a_lora_up', 'm_k_k', 'm_k_a', 'm_r_k', 'm_lnx_g', 'm_lnx_b', 'm_conv_w', 'm_conv_b', 'm_cln_g', 'm_cln_b', 'm_w_pw2', 'm_b_pw2', 'm_w_out', 'm_norm_post_g', 'v_norm_pre_g', 'v_w_in', 'v_mu_shift', 'v_w0', 'v_w_lora_up', 'v_a0', 'v_a_lora_up', 'v_k_k', 'v_k_a', 'v_r_k', 'v_lnx_g', 'v_lnx_b', 'v_conv_w', 'v_conv_b', 'v_cln_g', 'v_cln_b', 'v_w_pw2', 'v_b_pw2', 'v_w_out', 'v_norm_post_g']
TWIN_OUTPUTS = ['loss', 'grad_x', 'grad_norm_pre_g', 'grad_w_in', 'grad_mu_shift', 'grad_w0', 'grad_w_lora_up', 'grad_a0', 'grad_a_lora_up', 'grad_k_k', 'grad_k_a', 'grad_r_k', 'grad_lnx_g', 'grad_lnx_b', 'grad_conv_w', 'grad_conv_b', 'grad_cln_g', 'grad_cln_b', 'grad_w_pw2', 'grad_b_pw2', 'grad_w_out', 'grad_norm_post_g', 'delta_norm_pre_g', 'delta_w_in', 'delta_mu_shift', 'delta_w0', 'delta_w_lora_up', 'delta_a0', 'delta_a_lora_up', 'delta_k_k', 'delta_k_a', 'delta_r_k', 'delta_lnx_g', 'delta_lnx_b', 'delta_conv_w', 'delta_conv_b', 'delta_cln_g', 'delta_cln_b', 'delta_w_pw2', 'delta_b_pw2', 'delta_w_out', 'delta_norm_post_g', 'new_m_norm_pre_g', 'new_m_w_in', 'new_m_mu_shift', 'new_m_w0', 'new_m_w_lora_up', 'new_m_a0', 'new_m_a_lora_up', 'new_m_k_k', 'new_m_k_a', 'new_m_r_k', 'new_m_lnx_g', 'new_m_lnx_b', 'new_m_conv_w', 'new_m_conv_b', 'new_m_cln_g', 'new_m_cln_b', 'new_m_w_pw2', 'new_m_b_pw2', 'new_m_w_out', 'new_m_norm_post_g', 'new_v_norm_pre_g', 'new_v_w_in', 'new_v_mu_shift', 'new_v_w0', 'new_v_w_lora_up', 'new_v_a0', 'new_v_a_lora_up', 'new_v_k_k', 'new_v_k_a', 'new_v_r_k', 'new_v_lnx_g', 'new_v_lnx_b', 'new_v_conv_w', 'new_v_conv_b', 'new_v_cln_g', 'new_v_cln_b', 'new_v_w_pw2', 'new_v_b_pw2', 'new_v_w_out', 'new_v_norm_post_g']
TWIN_LEAF_KINDS = {'loss': 'loss', 'grad_x': 'grad_x', 'grad_norm_pre_g': 'grad_w', 'grad_w_in': 'grad_w', 'grad_mu_shift': 'grad_w', 'grad_w0': 'grad_w', 'grad_w_lora_up': 'grad_w', 'grad_a0': 'grad_w', 'grad_a_lora_up': 'grad_w', 'grad_k_k': 'grad_w', 'grad_k_a': 'grad_w', 'grad_r_k': 'grad_w', 'grad_lnx_g': 'grad_w', 'grad_lnx_b': 'grad_w', 'grad_conv_w': 'grad_w', 'grad_conv_b': 'grad_w', 'grad_cln_g': 'grad_w', 'grad_cln_b': 'grad_w', 'grad_w_pw2': 'grad_w', 'grad_b_pw2': 'grad_w', 'grad_w_out': 'grad_w', 'grad_norm_post_g': 'grad_w', 'delta_norm_pre_g': 'delta_w', 'delta_w_in': 'delta_w', 'delta_mu_shift': 'delta_w', 'delta_w0': 'delta_w', 'delta_w_lora_up': 'delta_w', 'delta_a0': 'delta_w', 'delta_a_lora_up': 'delta_w', 'delta_k_k': 'delta_w', 'delta_k_a': 'delta_w', 'delta_r_k': 'delta_w', 'delta_lnx_g': 'delta_w', 'delta_lnx_b': 'delta_w', 'delta_conv_w': 'delta_w', 'delta_conv_b': 'delta_w', 'delta_cln_g': 'delta_w', 'delta_cln_b': 'delta_w', 'delta_w_pw2': 'delta_w', 'delta_b_pw2': 'delta_w', 'delta_w_out': 'delta_w', 'delta_norm_post_g': 'delta_w', 'new_m_norm_pre_g': 'new_m', 'new_m_w_in': 'new_m', 'new_m_mu_shift': 'new_m', 'new_m_w0': 'new_m', 'new_m_w_lora_up': 'new_m', 'new_m_a0': 'new_m', 'new_m_a_lora_up': 'new_m', 'new_m_k_k': 'new_m', 'new_m_k_a': 'new_m', 'new_m_r_k': 'new_m', 'new_m_lnx_g': 'new_m', 'new_m_lnx_b': 'new_m', 'new_m_conv_w': 'new_m', 'new_m_conv_b': 'new_m', 'new_m_cln_g': 'new_m', 'new_m_cln_b': 'new_m', 'new_m_w_pw2': 'new_m', 'new_m_b_pw2': 'new_m', 'new_m_w_out': 'new_m', 'new_m_norm_post_g': 'new_m', 'new_v_norm_pre_g': 'new_v', 'new_v_w_in': 'new_v', 'new_v_mu_shift': 'new_v', 'new_v_w0': 'new_v', 'new_v_w_lora_up': 'new_v', 'new_v_a0': 'new_v', 'new_v_a_lora_up': 'new_v', 'new_v_k_k': 'new_v', 'new_v_k_a': 'new_v', 'new_v_r_k': 'new_v', 'new_v_lnx_g': 'new_v', 'new_v_lnx_b': 'new_v', 'new_v_conv_w': 'new_v', 'new_v_conv_b': 'new_v', 'new_v_cln_g': 'new_v', 'new_v_cln_b': 'new_v', 'new_v_w_pw2': 'new_v', 'new_v_b_pw2': 'new_v', 'new_v_w_out': 'new_v', 'new_v_norm_post_g': 'new_v'}


def _forward(args):
    return _fwd_reference(*[args[k] for k in FWD_PARAMS])


def _output_shape():
    out = _jax.eval_shape(lambda: _forward(_fwd_setup_inputs(0)))
    return out.shape, out.dtype

N_MICROBATCH = 1
ADAM_LR = 0.001
ADAM_B1 = 0.9
ADAM_B2 = 0.999
ADAM_EPS = 1e-08
ADAM_WD = 0.01
ADAM_STEP = 10
PER_EXAMPLE_BATCH_AXIS = {'x': 0, 'loss_target': 0}
SHARED_INPUTS = []
_WEIGHT_DTYPES = {'norm_pre_g': _jnp.float32, 'w_in': _jnp.float32, 'mu_shift': _jnp.float32, 'w0': _jnp.float32, 'w_lora_up': _jnp.float32, 'a0': _jnp.float32, 'a_lora_up': _jnp.float32, 'k_k': _jnp.float32, 'k_a': _jnp.float32, 'r_k': _jnp.float32, 'lnx_g': _jnp.float32, 'lnx_b': _jnp.float32, 'conv_w': _jnp.float32, 'conv_b': _jnp.float32, 'cln_g': _jnp.float32, 'cln_b': _jnp.float32, 'w_pw2': _jnp.float32, 'b_pw2': _jnp.float32, 'w_out': _jnp.float32, 'norm_post_g': _jnp.float32}
MOMENT_SCALE = {'norm_pre_g': 1.285139e-01, 'w_in': 6.595452e-02, 'mu_shift': 1.296027e-01, 'w0': 3.696491e-02, 'w_lora_up': 4.135118e-03, 'a0': 3.245707e-02, 'a_lora_up': 3.094285e-02, 'k_k': 9.548143e-02, 'k_a': 8.454193e-02, 'r_k': 1.724771e-01, 'lnx_g': 7.564692e-02, 'lnx_b': 1.036197e-01, 'conv_w': 4.645714e-02, 'conv_b': 1.003683e-01, 'cln_g': 6.000654e-02, 'cln_b': 6.010850e-02, 'w_pw2': 4.760842e-02, 'b_pw2': 1.129250e-01, 'w_out': 6.335486e-02, 'norm_post_g': 7.986358e+00}


def _to_microbatches(a, axis):
    t = _jnp.moveaxis(a, axis, 0)
    t = t.reshape((N_MICROBATCH, t.shape[0] // N_MICROBATCH) + t.shape[1:])
    return _jnp.moveaxis(t, 1, axis + 1)


def setup_inputs(seed: int = 0) -> dict:
    inp = _fwd_setup_inputs(seed)
    key = _jax.random.fold_in(_jax.random.key(seed), 7919)
    shape, _ = _output_shape()
    out = dict(inp)
    out["loss_target"] = _jax.random.normal(_jax.random.fold_in(key, 0), shape, _jnp.float32)
    for i, name in enumerate(TWIN_WEIGHTS):
        w = inp[name].astype(_jnp.float32)
        if MOMENT_SCALE is None:
            s = _jnp.sqrt(_jnp.mean(_jnp.square(w)) + 1e-30)
        else:
            s = MOMENT_SCALE[name]
        km, kv = _jax.random.split(_jax.random.fold_in(key, i + 1))
        out[name] = w
        out["m_" + name] = s * _jax.random.normal(km, w.shape, _jnp.float32)
        out["v_" + name] = (s * s) * _jax.random.uniform(kv, w.shape, _jnp.float32, 0.5, 1.5)
    if N_MICROBATCH > 1:
        for name, axis in PER_EXAMPLE_BATCH_AXIS.items():
            out[name] = _to_microbatches(out[name], axis)
    return {'x': out['x'], 'norm_pre_g': out['norm_pre_g'], 'w_in': out['w_in'], 'mu_shift': out['mu_shift'], 'w0': out['w0'], 'w_lora_up': out['w_lora_up'], 'a0': out['a0'], 'a_lora_up': out['a_lora_up'], 'k_k': out['k_k'], 'k_a': out['k_a'], 'r_k': out['r_k'], 'lnx_g': out['lnx_g'], 'lnx_b': out['lnx_b'], 'conv_w': out['conv_w'], 'conv_b': out['conv_b'], 'cln_g': out['cln_g'], 'cln_b': out['cln_b'], 'w_pw2': out['w_pw2'], 'b_pw2': out['b_pw2'], 'w_out': out['w_out'], 'norm_post_g': out['norm_post_g'], 'loss_target': out['loss_target'], 'm_norm_pre_g': out['m_norm_pre_g'], 'm_w_in': out['m_w_in'], 'm_mu_shift': out['m_mu_shift'], 'm_w0': out['m_w0'], 'm_w_lora_up': out['m_w_lora_up'], 'm_a0': out['m_a0'], 'm_a_lora_up': out['m_a_lora_up'], 'm_k_k': out['m_k_k'], 'm_k_a': out['m_k_a'], 'm_r_k': out['m_r_k'], 'm_lnx_g': out['m_lnx_g'], 'm_lnx_b': out['m_lnx_b'], 'm_conv_w': out['m_conv_w'], 'm_conv_b': out['m_conv_b'], 'm_cln_g': out['m_cln_g'], 'm_cln_b': out['m_cln_b'], 'm_w_pw2': out['m_w_pw2'], 'm_b_pw2': out['m_b_pw2'], 'm_w_out': out['m_w_out'], 'm_norm_post_g': out['m_norm_post_g'], 'v_norm_pre_g': out['v_norm_pre_g'], 'v_w_in': out['v_w_in'], 'v_mu_shift': out['v_mu_shift'], 'v_w0': out['v_w0'], 'v_w_lora_up': out['v_w_lora_up'], 'v_a0': out['v_a0'], 'v_a_lora_up': out['v_a_lora_up'], 'v_k_k': out['v_k_k'], 'v_k_a': out['v_k_a'], 'v_r_k': out['v_r_k'], 'v_lnx_g': out['v_lnx_g'], 'v_lnx_b': out['v_lnx_b'], 'v_conv_w': out['v_conv_w'], 'v_conv_b': out['v_conv_b'], 'v_cln_g': out['v_cln_g'], 'v_cln_b': out['v_cln_b'], 'v_w_pw2': out['v_w_pw2'], 'v_b_pw2': out['v_b_pw2'], 'v_w_out': out['v_w_out'], 'v_norm_post_g': out['v_norm_post_g']}


def _loss(weights, diff, rest, loss_target):
    with _jax.named_scope("forward"):
        args = {**rest, TWIN_DIFF_INPUT: diff, **{k: w.astype(_WEIGHT_DTYPES[k]) for k, w in weights.items()}}
        y = _forward(args)
    with _jax.named_scope("loss_head"):
        err = _jnp.square(y.astype(_jnp.float32) - loss_target)
        return 0.5 * _jnp.sum(_jnp.mean(err, axis=-1)) if err.ndim else 0.5 * err


def _adamw(w, g, m, v):
    m = ADAM_B1 * m + (1.0 - ADAM_B1) * g
    v = ADAM_B2 * v + (1.0 - ADAM_B2) * _jnp.square(g)
    m_hat = m / (1.0 - ADAM_B1 ** ADAM_STEP)
    v_hat = v / (1.0 - ADAM_B2 ** ADAM_STEP)
    delta = -ADAM_LR * (m_hat / (_jnp.sqrt(v_hat) + ADAM_EPS) + ADAM_WD * w)
    return delta, m, v


def reference(x, norm_pre_g, w_in, mu_shift, w0, w_lora_up, a0, a_lora_up, k_k, k_a, r_k, lnx_g, lnx_b, conv_w, conv_b, cln_g, cln_b, w_pw2, b_pw2, w_out, norm_post_g, loss_target, m_norm_pre_g, m_w_in, m_mu_shift, m_w0, m_w_lora_up, m_a0, m_a_lora_up, m_k_k, m_k_a, m_r_k, m_lnx_g, m_lnx_b, m_conv_w, m_conv_b, m_cln_g, m_cln_b, m_w_pw2, m_b_pw2, m_w_out, m_norm_post_g, v_norm_pre_g, v_w_in, v_mu_shift, v_w0, v_w_lora_up, v_a0, v_a_lora_up, v_k_k, v_k_a, v_r_k, v_lnx_g, v_lnx_b, v_conv_w, v_conv_b, v_cln_g, v_cln_b, v_w_pw2, v_b_pw2, v_w_out, v_norm_post_g):
    given = dict(x=x, norm_pre_g=norm_pre_g, w_in=w_in, mu_shift=mu_shift, w0=w0, w_lora_up=w_lora_up, a0=a0, a_lora_up=a_lora_up, k_k=k_k, k_a=k_a, r_k=r_k, lnx_g=lnx_g, lnx_b=lnx_b, conv_w=conv_w, conv_b=conv_b, cln_g=cln_g, cln_b=cln_b, w_pw2=w_pw2, b_pw2=b_pw2, w_out=w_out, norm_post_g=norm_post_g, loss_target=loss_target, m_norm_pre_g=m_norm_pre_g, m_w_in=m_w_in, m_mu_shift=m_mu_shift, m_w0=m_w0, m_w_lora_up=m_w_lora_up, m_a0=m_a0, m_a_lora_up=m_a_lora_up, m_k_k=m_k_k, m_k_a=m_k_a, m_r_k=m_r_k, m_lnx_g=m_lnx_g, m_lnx_b=m_lnx_b, m_conv_w=m_conv_w, m_conv_b=m_conv_b, m_cln_g=m_cln_g, m_cln_b=m_cln_b, m_w_pw2=m_w_pw2, m_b_pw2=m_b_pw2, m_w_out=m_w_out, m_norm_post_g=m_norm_post_g, v_norm_pre_g=v_norm_pre_g, v_w_in=v_w_in, v_mu_shift=v_mu_shift, v_w0=v_w0, v_w_lora_up=v_w_lora_up, v_a0=v_a0, v_a_lora_up=v_a_lora_up, v_k_k=v_k_k, v_k_a=v_k_a, v_r_k=v_r_k, v_lnx_g=v_lnx_g, v_lnx_b=v_lnx_b, v_conv_w=v_conv_w, v_conv_b=v_conv_b, v_cln_g=v_cln_g, v_cln_b=v_cln_b, v_w_pw2=v_w_pw2, v_b_pw2=v_b_pw2, v_w_out=v_w_out, v_norm_post_g=v_norm_post_g)
    weights = {n: given[n] for n in TWIN_WEIGHTS}
    shared = {n: given[n] for n in SHARED_INPUTS}
    per_example = {n: given[n] for n in ['x']}
    grad_fn = _jax.value_and_grad(_loss, argnums=(0, 1))

    def one_microbatch(ex, loss_target):
        ex = dict(ex)
        diff = ex.pop(TWIN_DIFF_INPUT)
        return grad_fn(weights, diff, {**shared, **ex}, loss_target)

    if N_MICROBATCH == 1:
        loss, (grad_w, grad_x) = one_microbatch(per_example, given["loss_target"])
    else:
        def body(carry, xs):
            loss_sum, grad_sum = carry
            l_k, (gw_k, gx_k) = one_microbatch(xs[0], xs[1])
            with _jax.named_scope("update"):
                return (loss_sum + l_k, _jax.tree.map(_jnp.add, grad_sum, gw_k)), gx_k

        init = (_jnp.zeros((), _jnp.float32), _jax.tree.map(_jnp.zeros_like, weights))
        (loss, grad_w), grad_x = _jax.lax.scan(body, init, (per_example, given["loss_target"]))
    with _jax.named_scope("update"):
        delta_w, new_m, new_v = {}, {}, {}
        for n in TWIN_WEIGHTS:
            delta_w[n], new_m[n], new_v[n] = _adamw(weights[n], grad_w[n], given["m_" + n], given["v_" + n])
    return (loss, grad_x, *[grad_w[n] for n in TWIN_WEIGHTS], *[delta_w[n] for n in TWIN_WEIGHTS],
            *[new_m[n] for n in TWIN_WEIGHTS], *[new_v[n] for n in TWIN_WEIGHTS])
```

```python
import functools

import jax
import jax.numpy as jnp
from jax import lax
from jax.experimental import pallas as pl
from jax.experimental.pallas import tpu as pltpu

F32 = jnp.float32
MXU_DTYPE = jnp.bfloat16
WIRE_DTYPE = jnp.bfloat16
HI = lax.Precision.HIGHEST
SEG_PRECISION = lax.Precision.HIGH

NORM_EPS = 1e-6
LN_EPS = 1e-5
GN_EPS_PER_CHANNEL = 1e-5
KK_EPS = 1e-12
ADAM_LR = 0.001
ADAM_B1 = 0.9
ADAM_B2 = 0.999
ADAM_EPS = 1e-08
ADAM_WD = 0.01
ADAM_STEP = 10

LANE = 128
SUBLANE = 8
LORA_PAD = 128
CONV_HALO = 32
N_DEV = 8
VMEM_LIMIT = 56 * 1024 * 1024
WKV_CHUNK = 32
RWKV_ROWS = 64
MESH = pl.DeviceIdType.MESH


def _tile(n, target, mult):
    if n <= target:
        return n
    best = None
    for d in range(mult, target + 1, mult):
        if n % d == 0:
            best = d
    assert best is not None, (n, target, mult)
    return best


def _cparams(sem=None):
    return pltpu.CompilerParams(dimension_semantics=sem, vmem_limit_bytes=VMEM_LIMIT)


def _sigmoid(x):
    return 1.0 / (1.0 + jnp.exp(-x))


def _full(shape):
    nd = len(shape)
    return pl.BlockSpec(shape, lambda *_: (0,) * nd)


def _matmul(a, b, mode, out_dtype, name):
    if mode == "nn":
        (m, k), (k2, n) = a.shape, b.shape
    elif mode == "nt":
        (m, k), (n, k2) = a.shape, b.shape
    else:
        (k, m), (k2, n) = a.shape, b.shape
    assert k == k2, (a.shape, b.shape, mode)
    tm, tn, tk = _tile(m, 1024, LANE), _tile(n, 768, LANE), _tile(k, 512, LANE)
    nk = k // tk
    if mode == "nn":
        a_spec = pl.BlockSpec((tm, tk), lambda i, j, kk: (i, kk))
        b_spec = pl.BlockSpec((tk, tn), lambda i, j, kk: (kk, j))
        dims = (((1,), (0,)), ((), ()))
    elif mode == "nt":
        a_spec = pl.BlockSpec((tm, tk), lambda i, j, kk: (i, kk))
        b_spec = pl.BlockSpec((tn, tk), lambda i, j, kk: (j, kk))
        dims = (((1,), (1,)), ((), ()))
    else:
        a_spec = pl.BlockSpec((tk, tm), lambda i, j, kk: (kk, i))
        b_spec = pl.BlockSpec((tk, tn), lambda i, j, kk: (kk, j))
        dims = (((0,), (0,)), ((), ()))

    def body(a_ref, b_ref, o_ref, acc_ref):
        kk = pl.program_id(2)

        @pl.when(kk == 0)
        def _():
            acc_ref[...] = jnp.zeros_like(acc_ref)

        acc_ref[...] += lax.dot_general(a_ref[...], b_ref[...], dims, preferred_element_type=F32)

        @pl.when(kk == nk - 1)
        def _():
            o_ref[...] = acc_ref[...].astype(o_ref.dtype)

    return pl.pallas_call(
        body, name=name,
        grid=(m // tm, n // tn, nk),
        in_specs=[a_spec, b_spec],
        out_specs=pl.BlockSpec((tm, tn), lambda i, j, kk: (i, j)),
        out_shape=jax.ShapeDtypeStruct((m, n), out_dtype),
        scratch_shapes=[pltpu.VMEM((tm, tn), F32)],
        compiler_params=_cparams(("parallel", "parallel", "arbitrary")),
    )(a, b)


def _prenorm(x, g):
    t, d = x.shape
    tt = _tile(t, 256, SUBLANE)

    def body(x_ref, g_ref, h_ref):
        xv = x_ref[...]
        rinv = lax.rsqrt(jnp.mean(xv * xv, axis=-1, keepdims=True) + NORM_EPS)
        h_ref[...] = (xv * rinv * g_ref[...]).astype(h_ref.dtype)

    return pl.pallas_call(
        body, name="prenorm", grid=(t // tt,),
        in_specs=[pl.BlockSpec((tt, d), lambda i: (i, 0)), _full((1, d))],
        out_specs=pl.BlockSpec((tt, d), lambda i: (i, 0)),
        out_shape=jax.ShapeDtypeStruct((t, d), MXU_DTYPE),
        compiler_params=_cparams(("parallel",)),
    )(x, g)


def _post_loss(out, x, target, g):
    t, d = out.shape
    tt = _tile(t, 128, SUBLANE)

    def body(o_ref, x_ref, t_ref, g_ref, dout_ref, dy_ref, loss_ref, dg_ref):
        i = pl.program_id(0)
        ov = o_ref[...]
        rinv = lax.rsqrt(jnp.mean(ov * ov, axis=-1, keepdims=True) + NORM_EPS)
        nv = ov * rinv
        gv = g_ref[...]
        err = x_ref[...] + nv * gv - t_ref[...]
        part = 0.5 * jnp.sum(jnp.mean(err * err, axis=-1, keepdims=True), axis=0, keepdims=True)
        dy = err * (1.0 / d)
        dy_ref[...] = dy
        dn = dy * gv
        dout = rinv * (dn - nv * jnp.mean(dn * nv, axis=-1, keepdims=True))
        dout_ref[...] = dout.astype(dout_ref.dtype)
        dg = jnp.sum(dy * nv, axis=0, keepdims=True)

        @pl.when(i == 0)
        def _():
            loss_ref[...] = jnp.zeros_like(loss_ref)
            dg_ref[...] = jnp.zeros_like(dg_ref)

        loss_ref[...] += jnp.broadcast_to(part, loss_ref.shape)
        dg_ref[...] += dg

    row = pl.BlockSpec((tt, d), lambda i: (i, 0))
    return pl.pallas_call(
        body, name="post_loss", grid=(t // tt,),
        in_specs=[row, row, row, _full((1, d))],
        out_specs=[row, row, _full((1, LANE)), _full((1, d))],
        out_shape=[jax.ShapeDtypeStruct((t, d), MXU_DTYPE), jax.ShapeDtypeStruct((t, d), F32),
                   jax.ShapeDtypeStruct((1, LANE), F32), jax.ShapeDtypeStruct((1, d), F32)],
        compiler_params=_cparams(("arbitrary",)),
    )(out, x, target, g)


def _prenorm_bwd(dh, x, dy, g):
    t, d = x.shape
    tt = _tile(t, 128, SUBLANE)

    def body(dh_ref, x_ref, dy_ref, g_ref, gx_ref, dg_ref):
        i = pl.program_id(0)
        xv = x_ref[...]
        rinv = lax.rsqrt(jnp.mean(xv * xv, axis=-1, keepdims=True) + NORM_EPS)
        nx = xv * rinv
        dhv = dh_ref[...]
        dnx = dhv * g_ref[...]
        dx = rinv * (dnx - nx * jnp.mean(dnx * nx, axis=-1, keepdims=True))
        gx_ref[...] = dy_ref[...] + dx

        @pl.when(i == 0)
        def _():
            dg_ref[...] = jnp.zeros_like(dg_ref)

        dg_ref[...] += jnp.sum(dhv * nx, axis=0, keepdims=True)

    row = pl.BlockSpec((tt, d), lambda i: (i, 0))
    return pl.pallas_call(
        body, name="prenorm_bwd", grid=(t // tt,),
        in_specs=[row, row, row, _full((1, d))],
        out_specs=[row, _full((1, d))],
        out_shape=[jax.ShapeDtypeStruct((t, d), F32), jax.ShapeDtypeStruct((1, d), F32)],
        compiler_params=_cparams(("arbitrary",)),
    )(dh, x, dy, g)


def _segsum(v, e_ref, et_ref):
    s = jnp.dot(v, e_ref[...], preferred_element_type=F32, precision=SEG_PRECISION)
    return jnp.dot(s, et_ref[...], preferred_element_type=F32, precision=SEG_PRECISION)


def _shifted(cur_ref, prev_ref, lo, hi, first):
    cur = cur_ref[:, lo:hi]
    last = jnp.where(first, 0.0, prev_ref[SUBLANE - 1:SUBLANE, lo:hi])
    prev = pltpu.roll(cur, 1, 0)
    rows = lax.broadcasted_iota(jnp.int32, cur.shape, 0)
    return cur, jnp.where(rows == 0, last, prev)


def _rwkv_mix(main_ref, mainp_ref, lo_ref, lop_ref, mu_ref, mulo_ref, w0_ref, a0_ref, kk_ref, ka_ref,
              wupw_ref, wupa_ref, e_ref, et_ref, hw, first):
    def xs(cur_ref, prev_ref, m_ref, lo, hi):
        cur, prev = _shifted(cur_ref, prev_ref, lo, hi, first)
        return cur + (prev - cur) * m_ref[:, lo:hi], prev - cur

    out = {}
    out["r"], out["r_d"] = xs(main_ref, mainp_ref, mu_ref, 0, hw)
    out["k"], out["k_d"] = xs(main_ref, mainp_ref, mu_ref, hw, 2 * hw)
    out["v"], out["v_d"] = xs(main_ref, mainp_ref, mu_ref, 2 * hw, 3 * hw)
    out["wl"], out["wl_d"] = xs(lo_ref, lop_ref, mulo_ref, 0, LORA_PAD)
    out["al"], out["al_d"] = xs(lo_ref, lop_ref, mulo_ref, LORA_PAD, 2 * LORA_PAD)
    th = jnp.tanh(out["wl"])
    zw = w0_ref[...] + jnp.dot(th, wupw_ref[...], preferred_element_type=F32, precision=HI)
    u = -zw
    softplus = jnp.maximum(u, 0.0) + jnp.log(1.0 + jnp.exp(-jnp.abs(u)))
    wlog = -softplus - 0.5
    ew = jnp.exp(wlog)
    za = a0_ref[...] + jnp.dot(out["al"], wupa_ref[...], preferred_element_type=F32, precision=HI)
    a = _sigmoid(za)
    kkr = out["k"] * kk_ref[...]
    nr = jnp.sqrt(_segsum(kkr * kkr, e_ref, et_ref))
    nrm = jnp.maximum(nr, KK_EPS)
    out.update(th=th, zw=zw, ew=ew, decay=jnp.exp(-ew), a=a, kkr=kkr, nr=nr, nrm=nrm, kk=kkr / nrm)
    out["kh"] = out["k"] * (1.0 + (a - 1.0) * ka_ref[...])
    return out


def _mix_specs(tt, hw, lo_blk):
    mw, lw2 = 3 * hw, 2 * LORA_PAD
    before = lambda i: jnp.maximum(i * (tt // SUBLANE) - 1, 0)
    vec = _full((1, hw))
    return [pl.BlockSpec((tt, mw), lambda i: (i, 0)), pl.BlockSpec((SUBLANE, mw), lambda i: (before(i), 0)),
            pl.BlockSpec((tt, lw2), lambda i: (i, lo_blk)), pl.BlockSpec((SUBLANE, lw2), lambda i: (before(i), lo_blk)),
            _full((1, mw)), _full((1, lw2)), vec, vec, vec, vec, _full((LORA_PAD, hw)), _full((LORA_PAD, hw)),
            _full((hw, LANE)), _full((LANE, hw))]


def _rwkv_pre(proj, mu, mu_lo, w0, a0, k_k, k_a, wup_w, wup_a, e, et, hw, lo_blk):
    t = proj.shape[0]
    tt = _tile(t, RWKV_ROWS, SUBLANE)

    def body(*refs):
        r_o, w_o, kh_o, v_o, kn_o, b_o = refs[-6:]
        f = _rwkv_mix(*refs[:-6], hw, pl.program_id(0) == 0)
        r_o[...] = f["r"]
        w_o[...] = f["decay"]
        kh_o[...] = f["kh"]
        v_o[...] = f["v"]
        kn_o[...] = -f["kk"]
        b_o[...] = f["kk"] * f["a"]

    row = pl.BlockSpec((tt, hw), lambda i: (i, 0))
    return pl.pallas_call(
        body, name="rwkv_pre", grid=(t // tt,),
        in_specs=_mix_specs(tt, hw, lo_blk),
        out_specs=[row] * 6,
        out_shape=[jax.ShapeDtypeStruct((t, hw), F32)] * 6,
        compiler_params=_cparams(("parallel",)),
    )(proj, proj, proj, proj, mu, mu_lo, w0, a0, k_k, k_a, wup_w, wup_a, e, et)


def _rwkv_post_math(y, r, kh, v, g, lnx_g, lnx_b, r_k, e_ref, et_ref, n):
    mean = _segsum(y, e_ref, et_ref) * (1.0 / n)
    yc = y - mean
    var = _segsum(yc * yc, e_ref, et_ref) * (1.0 / n)
    rstd = lax.rsqrt(var + GN_EPS_PER_CHANNEL * n)
    yn = yc * rstd
    s = _segsum(r * kh * r_k, e_ref, et_ref)
    y3 = yn * lnx_g + lnx_b + s * v
    sg = _sigmoid(g)
    return yn, rstd, s, y3, sg


def _rwkv_post(y, r, kh, v, proj, lnx_g, lnx_b, r_k, e, et, hw, n, gate_blk):
    t = y.shape[0]
    tt = _tile(t, RWKV_ROWS, SUBLANE)

    def body(y_ref, r_ref, kh_ref, v_ref, g_ref, lg_ref, lb_ref, rk_ref, e_ref, et_ref, o_ref):
        g = g_ref[...]
        _, _, _, y3, sg = _rwkv_post_math(y_ref[...], r_ref[...], kh_ref[...], v_ref[...], g,
                                          lg_ref[...], lb_ref[...], rk_ref[...], e_ref, et_ref, n)
        o_ref[...] = (y3 * (g * sg)).astype(o_ref.dtype)

    row = pl.BlockSpec((tt, hw), lambda i: (i, 0))
    vec = _full((1, hw))
    return pl.pallas_call(
        body, name="rwkv_post", grid=(t // tt,),
        in_specs=[row, row, row, row, pl.BlockSpec((tt, hw), lambda i: (i, gate_blk)),
                  vec, vec, vec, _full((hw, LANE)), _full((LANE, hw))],
        out_specs=row,
        out_shape=jax.ShapeDtypeStruct((t, hw), MXU_DTYPE),
        compiler_params=_cparams(("parallel",)),
    )(y, r, kh, v, proj, lnx_g, lnx_b, r_k, e, et)


def _rwkv_post_bwd(dmix, y, r, kh, v, proj, lnx_g, lnx_b, r_k, e, et, hw, n, gate_blk):
    t = y.shape[0]
    tt = _tile(t, RWKV_ROWS, SUBLANE)

    def body(dm_ref, y_ref, r_ref, kh_ref, v_ref, g_ref, lg_ref, lb_ref, rk_ref, e_ref, et_ref,
             dg_o, dy_o, dr_o, dkh_o, dv_o, dlg_o, dlb_o, drk_o):
        i = pl.program_id(0)
        g, r, kh, v, rk, lg = g_ref[...], r_ref[...], kh_ref[...], v_ref[...], rk_ref[...], lg_ref[...]
        yn, rstd, s, y3, sg = _rwkv_post_math(y_ref[...], r, kh, v, g, lg, lb_ref[...], rk, e_ref, et_ref, n)
        dyr = dm_ref[...]
        dy3 = dyr * (g * sg)
        dg_o[...] = (dyr * y3 * (sg * (1.0 + g * (1.0 - sg)))).astype(dg_o.dtype)
        ds = _segsum(dy3 * v, e_ref, et_ref)
        dv_o[...] = dy3 * s
        dr_o[...] = ds * kh * rk
        dkh_o[...] = ds * r * rk
        dyn = dy3 * lg
        m1 = _segsum(dyn, e_ref, et_ref) * (1.0 / n)
        m2 = _segsum(dyn * yn, e_ref, et_ref) * (1.0 / n)
        dy_o[...] = rstd * (dyn - m1 - yn * m2)

        @pl.when(i == 0)
        def _():
            dlg_o[...] = jnp.zeros_like(dlg_o)
            dlb_o[...] = jnp.zeros_like(dlb_o)
            drk_o[...] = jnp.zeros_like(drk_o)

        dlg_o[...] += jnp.sum(dy3 * yn, axis=0, keepdims=True)
        dlb_o[...] += jnp.sum(dy3, axis=0, keepdims=True)
        drk_o[...] += jnp.sum(ds * r * kh, axis=0, keepdims=True)

    row = pl.BlockSpec((tt, hw), lambda i: (i, 0))
    vec = _full((1, hw))
    rowf = jax.ShapeDtypeStruct((t, hw), F32)
    vecf = jax.ShapeDtypeStruct((1, hw), F32)
    return pl.pallas_call(
        body, name="rwkv_post_bwd", grid=(t // tt,),
        in_specs=[row, row, row, row, row, pl.BlockSpec((tt, hw), lambda i: (i, gate_blk)),
                  vec, vec, vec, _full((hw, LANE)), _full((LANE, hw))],
        out_specs=[row, row, row, row, row, vec, vec, vec],
        out_shape=[jax.ShapeDtypeStruct((t, hw), MXU_DTYPE), rowf, rowf, rowf, rowf, vecf, vecf, vecf],
        compiler_params=_cparams(("arbitrary",)),
    )(dmix, y, r, kh, v, proj, lnx_g, lnx_b, r_k, e, et)


def _rwkv_pre_bwd(proj, mu, mu_lo, w0, a0, k_k, k_a, wup_w, wup_a, e, et, hw, lo_blk,
                  dr_rec, dw_rec, dkh_rec, dv_rec, dkn_rec, db_rec, dr_bon, dkh_bon, dv_bon):
    t = proj.shape[0]
    mw, lw2 = 3 * hw, 2 * LORA_PAD
    tt = _tile(t, RWKV_ROWS, SUBLANE)
    n_in = 14

    def body(*refs):
        mix_refs = refs[:n_in]
        drr_ref, dwr_ref, dkhr_ref, dvr_ref, dknr_ref, dbr_ref, drb_ref, dkhb_ref, dvb_ref = refs[n_in:n_in + 9]
        dxs_o, dxl_o, dmu_o, dmul_o, dw0_o, da0_o, dkk_o, dka_o, dwupw_o, dwupa_o = refs[n_in + 9:]
        kk_ref, ka_ref, wupw_ref, wupa_ref, e_ref, et_ref = mix_refs[8:14]
        i = pl.program_id(0)
        f = _rwkv_mix(*mix_refs, hw, i == 0)
        k, a, kk, nrm = f["k"], f["a"], f["kk"], f["nrm"]
        k_a, k_k = ka_ref[...], kk_ref[...]
        dr = drr_ref[...] + drb_ref[...]
        dkh = dkhr_ref[...] + dkhb_ref[...]
        dv = dvr_ref[...] + dvb_ref[...]
        db = dbr_ref[...]
        da = db * kk + dkh * k * k_a
        dkk = db * a - dknr_ref[...]
        dk = dkh * (1.0 + (a - 1.0) * k_a)
        dka = jnp.sum(dkh * k * (a - 1.0), axis=0, keepdims=True)
        proj_kk = _segsum(dkk * kk, e_ref, et_ref)
        dkkr = jnp.where(f["nr"] > KK_EPS, (dkk - kk * proj_kk) / nrm, dkk * (1.0 / KK_EPS))
        dk = dk + dkkr * k_k
        dkk_w = jnp.sum(dkkr * k, axis=0, keepdims=True)
        dza = da * a * (1.0 - a)
        dzw = dwr_ref[...] * f["decay"] * (-f["ew"]) * _sigmoid(-f["zw"])
        nt_dims = (((1,), (1,)), ((), ()))
        tn_dims = (((0,), (0,)), ((), ()))
        dal = lax.dot_general(dza, wupa_ref[...], nt_dims, preferred_element_type=F32, precision=HI)
        dth = lax.dot_general(dzw, wupw_ref[...], nt_dims, preferred_element_type=F32, precision=HI)
        dwl = dth * (1.0 - f["th"] * f["th"])
        dxs_o[:, 0:hw] = dr
        dxs_o[:, hw:2 * hw] = dk
        dxs_o[:, 2 * hw:3 * hw] = dv
        dxl_o[:, 0:LORA_PAD] = dwl
        dxl_o[:, LORA_PAD:lw2] = dal

        @pl.when(i == 0)
        def _():
            for ref in (dmu_o, dmul_o, dw0_o, da0_o, dkk_o, dka_o, dwupw_o, dwupa_o):
                ref[...] = jnp.zeros_like(ref)

        def colsum(v):
            return jnp.sum(v, axis=0, keepdims=True)

        dmu_o[:, 0:hw] += colsum(dr * f["r_d"])
        dmu_o[:, hw:2 * hw] += colsum(dk * f["k_d"])
        dmu_o[:, 2 * hw:3 * hw] += colsum(dv * f["v_d"])
        dmul_o[:, 0:LORA_PAD] += colsum(dwl * f["wl_d"])
        dmul_o[:, LORA_PAD:lw2] += colsum(dal * f["al_d"])
        dw0_o[...] += colsum(dzw)
        da0_o[...] += colsum(dza)
        dkk_o[...] += dkk_w
        dka_o[...] += dka
        dwupw_o[...] += lax.dot_general(f["th"], dzw, tn_dims, preferred_element_type=F32, precision=HI)
        dwupa_o[...] += lax.dot_general(f["al"], dza, tn_dims, preferred_element_type=F32, precision=HI)

    row = pl.BlockSpec((tt, hw), lambda i: (i, 0))
    vec = _full((1, hw))
    vecf = jax.ShapeDtypeStruct((1, hw), F32)
    return pl.pallas_call(
        body, name="rwkv_pre_bwd", grid=(t // tt,),
        in_specs=_mix_specs(tt, hw, lo_blk) + [row] * 9,
        out_specs=[pl.BlockSpec((tt, mw), lambda i: (i, 0)), pl.BlockSpec((tt, lw2), lambda i: (i, 0)),
                   _full((1, mw)), _full((1, lw2)), vec, vec, vec, vec,
                   _full((LORA_PAD, hw)), _full((LORA_PAD, hw))],
        out_shape=[jax.ShapeDtypeStruct((t, mw), F32), jax.ShapeDtypeStruct((t, lw2), F32),
                   jax.ShapeDtypeStruct((1, mw), F32), jax.ShapeDtypeStruct((1, lw2), F32),
                   vecf, vecf, vecf, vecf,
                   jax.ShapeDtypeStruct((LORA_PAD, hw), F32), jax.ShapeDtypeStruct((LORA_PAD, hw), F32)],
        compiler_params=_cparams(("arbitrary",)),
    )(proj, proj, proj, proj, mu, mu_lo, w0, a0, k_k, k_a, wup_w, wup_a, e, et,
      dr_rec, dw_rec, dkh_rec, dv_rec, dkn_rec, db_rec, dr_bon, dkh_bon, dv_bon)


def _shift_bwd(dxs, mu, name):
    t, sw = dxs.shape
    tt = _tile(t, 256, SUBLANE)
    nblk = t // SUBLANE

    def body(d_ref, nxt_ref, mu_ref, o_ref):
        last = pl.program_id(0) == pl.num_programs(0) - 1
        cur = d_ref[...]
        first_next = jnp.where(last, 0.0, nxt_ref[0:1, :])
        nxt = pltpu.roll(cur, tt - 1, 0)
        rows = lax.broadcasted_iota(jnp.int32, cur.shape, 0)
        nxt = jnp.where(rows == tt - 1, first_next, nxt)
        m = mu_ref[...]
        o_ref[...] = (cur * (1.0 - m) + nxt * m).astype(o_ref.dtype)

    return pl.pallas_call(
        body, name=name, grid=(t // tt,),
        in_specs=[pl.BlockSpec((tt, sw), lambda i: (i, 0)),
                  pl.BlockSpec((SUBLANE, sw), lambda i: (jnp.minimum((i + 1) * (tt // SUBLANE), nblk - 1), 0)),
                  _full((1, sw))],
        out_specs=pl.BlockSpec((tt, sw), lambda i: (i, 0)),
        out_shape=jax.ShapeDtypeStruct((t, sw), MXU_DTYPE),
        compiler_params=_cparams(("parallel",)),
    )(dxs, dxs, mu)


def _tree_sum(parts):
    while len(parts) > 1:
        parts = [parts[p] + parts[p + 1] for p in range(0, len(parts) - 1, 2)] + ([parts[-1]] if len(parts) % 2 else [])
    return parts[0]


def _wkv_fwd(w_t, b_t, k_t, r_t, kn_t, v_c):
    t, n, _ = w_t.shape
    q = v_c.shape[1]
    tc = _tile(t, WKV_CHUNK, 1)
    nacc = 4

    def body(w_ref, b_ref, k_ref, r_ref, kn_ref, v_ref, y_ref, sa_ref, ck_ref, s_ref):
        @pl.when(pl.program_id(0) == 0)
        def _():
            s_ref[...] = jnp.zeros_like(s_ref)

        ck_ref[0] = s_ref[...]

        def step(ts, carry):
            vt = v_ref[ts]
            acc = [None] * nacc
            for j in range(n):
                term = s_ref[j] * kn_ref[ts, pl.ds(j, 1), :]
                acc[j % nacc] = term if acc[j % nacc] is None else acc[j % nacc] + term
            sa = _tree_sum(acc)
            sa_ref[ts] = sa
            acc = [None] * nacc
            for j in range(n):
                sj = (s_ref[j] * w_ref[ts, pl.ds(j, 1), :] + sa * b_ref[ts, pl.ds(j, 1), :]
                      + vt * k_ref[ts, pl.ds(j, 1), :])
                s_ref[j] = sj
                term = sj * r_ref[ts, pl.ds(j, 1), :]
                acc[j % nacc] = term if acc[j % nacc] is None else acc[j % nacc] + term
            y_ref[ts] = _tree_sum(acc)
            return carry

        lax.fori_loop(0, tc, step, 0)

    tiled = pl.BlockSpec((tc, n, LANE), lambda c: (c, 0, 0))
    comp = pl.BlockSpec((tc, q, LANE), lambda c: (c, 0, 0))
    return pl.pallas_call(
        body, name="wkv_fwd", grid=(t // tc,),
        in_specs=[tiled] * 5 + [comp],
        out_specs=[comp, comp, pl.BlockSpec((1, n, q, LANE), lambda c: (c, 0, 0, 0))],
        out_shape=[jax.ShapeDtypeStruct((t, q, LANE), F32), jax.ShapeDtypeStruct((t, q, LANE), F32),
                   jax.ShapeDtypeStruct((t // tc, n, q, LANE), F32)],
        scratch_shapes=[pltpu.VMEM((n, q, LANE), F32)],
        compiler_params=_cparams(("arbitrary",)),
    )(w_t, b_t, k_t, r_t, kn_t, v_c)


def _wkv_bwd(r_t, b_t, k_t, w_t, kn_t, dy_c, r_c, w_c, b_c, k_c, kn_c, dy_t, sa_t, v_t, ck_i, nh):
    t, n, _ = r_t.shape
    q = dy_c.shape[1]
    tc = _tile(t, WKV_CHUNK, 1)
    nc = t // tc
    rep = LANE // nh
    nacc = 4

    def body(rt_ref, bt_ref, kt_ref, wt_ref, knt_ref, dyc_ref, rc_ref, wc_ref, bc_ref, kc_ref, knc_ref,
             dyt_ref, sat_ref, vt_ref, ck_ref,
             dv_o, dr_o, dw_o, db_o, dk_o, dkn_o, hist, g_ref, gp_ref, dsat_ref):
        @pl.when(pl.program_id(0) == 0)
        def _():
            g_ref[...] = jnp.zeros_like(g_ref)
            gp_ref[...] = jnp.zeros_like(gp_ref)

        hist[0] = ck_ref[0]

        def fstep(ts, carry):
            wv, bv, kv = wc_ref[ts], bc_ref[ts], kc_ref[ts]
            for i in range(n):
                hist[ts + 1, i] = (hist[ts, i] * wv + sat_ref[ts, pl.ds(i, 1), :] * bv
                                   + vt_ref[ts, pl.ds(i, 1), :] * kv)
            return carry

        lax.fori_loop(0, tc, fstep, 0)

        lane_group = lax.broadcasted_iota(jnp.int32, (q, LANE), 1) // nh

        def bstep(s, carry):
            ts = tc - 1 - s
            dy = dyc_ref[ts]
            acc_sa, acc_v = [None] * nacc, [None] * nacc
            for j in range(n):
                gj = g_ref[j] + dy * rt_ref[ts, pl.ds(j, 1), :]
                g_ref[j] = gj
                t1 = gj * bt_ref[ts, pl.ds(j, 1), :]
                t2 = gj * kt_ref[ts, pl.ds(j, 1), :]
                a = j % nacc
                acc_sa[a] = t1 if acc_sa[a] is None else acc_sa[a] + t1
                acc_v[a] = t2 if acc_v[a] is None else acc_v[a] + t2
            dsa = _tree_sum(acc_sa)
            dv_o[ts] = _tree_sum(acc_v)
            for j in range(n):
                g_ref[j] = g_ref[j] * wt_ref[ts, pl.ds(j, 1), :] + dsa * knt_ref[ts, pl.ds(j, 1), :]
            for grp in range(rep):
                m = jnp.where(lane_group == grp, dsa, 0.0)
                shift = nh
                while shift < LANE:
                    m = m + pltpu.roll(m, shift, 1)
                    shift *= 2
                dsat_ref[grp] = m
            rv, wv, knv = rc_ref[ts], wc_ref[ts], knc_ref[ts]
            names = ("dr", "dw", "db", "dk", "dkn")
            accs = {nm: [None] * nacc for nm in names}
            for i in range(n):
                dsai = dsat_ref[i % rep, pl.ds(i // rep, 1), :]
                dyi = dyt_ref[ts, pl.ds(i, 1), :]
                s_prev = hist[ts, i]
                gi = gp_ref[i] + dyi * rv
                terms = {"dr": hist[ts + 1, i] * dyi, "dw": gi * s_prev, "db": gi * sat_ref[ts, pl.ds(i, 1), :],
                         "dk": gi * vt_ref[ts, pl.ds(i, 1), :], "dkn": dsai * s_prev}
                a = i % nacc
                for nm in names:
                    accs[nm][a] = terms[nm] if accs[nm][a] is None else accs[nm][a] + terms[nm]
                gp_ref[i] = gi * wv + dsai * knv
            dr_o[ts] = _tree_sum(accs["dr"])
            dw_o[ts] = _tree_sum(accs["dw"])
            db_o[ts] = _tree_sum(accs["db"])
            dk_o[ts] = _tree_sum(accs["dk"])
            dkn_o[ts] = _tree_sum(accs["dkn"])
            return carry

        lax.fori_loop(0, tc, bstep, 0)

    tiled = pl.BlockSpec((tc, n, LANE), lambda c: (nc - 1 - c, 0, 0))
    comp = pl.BlockSpec((tc, q, LANE), lambda c: (nc - 1 - c, 0, 0))
    outc = jax.ShapeDtypeStruct((t, q, LANE), F32)
    return pl.pallas_call(
        body, name="wkv_bwd", grid=(nc,),
        in_specs=[tiled] * 5 + [comp] * 6 + [tiled] * 3
        + [pl.BlockSpec((1, n, q, LANE), lambda c: (nc - 1 - c, 0, 0, 0))],
        out_specs=[comp] * 6,
        out_shape=[outc] * 6,
        scratch_shapes=[pltpu.VMEM((tc + 1, n, q, LANE), F32), pltpu.VMEM((n, q, LANE), F32),
                        pltpu.VMEM((n, q, LANE), F32), pltpu.VMEM((rep, q, LANE), F32)],
        compiler_params=_cparams(("arbitrary",)),
    )(r_t, b_t, k_t, w_t, kn_t, dy_c, r_c, w_c, b_c, k_c, kn_c, dy_t, sa_t, v_t, ck_i)


def _conv_stage(gv_ref, gg_ref, gvh_ref, ggh_ref, cw_ref, cb_ref, lg_ref, lb_ref, ext_ref, first, tt, taps):
    u = gv_ref[...] * _sigmoid(gg_ref[...])
    uh = jnp.where(first, 0.0, gvh_ref[...] * _sigmoid(ggh_ref[...]))
    ext_ref[0:CONV_HALO, :] = uh
    ext_ref[CONV_HALO:CONV_HALO + tt, :] = u
    off = CONV_HALO - (taps - 1)
    c = cb_ref[...] + ext_ref[off:off + tt, :] * cw_ref[0:1, :]
    for j in range(1, taps):
        c = c + ext_ref[off + j:off + j + tt, :] * cw_ref[j:j + 1, :]
    mean = jnp.mean(c, axis=-1, keepdims=True)
    cc = c - mean
    rstd = lax.rsqrt(jnp.mean(cc * cc, axis=-1, keepdims=True) + LN_EPS)
    chat = cc * rstd
    cn = chat * lg_ref[...] + lb_ref[...]
    return chat, rstd, cn


def _conv_specs(t, tt, cw, taps, gv_blk, gg_blk):
    hb = tt // CONV_HALO
    return [pl.BlockSpec((tt, cw), lambda i: (i, gv_blk)), pl.BlockSpec((tt, cw), lambda i: (i, gg_blk)),
            pl.BlockSpec((CONV_HALO, cw), lambda i: (jnp.maximum(i * hb - 1, 0), gv_blk)),
            pl.BlockSpec((CONV_HALO, cw), lambda i: (jnp.maximum(i * hb - 1, 0), gg_blk)),
            _full((taps, cw)), _full((1, cw)), _full((1, cw)), _full((1, cw))]


def _conv_fwd(proj, conv_w, conv_b, cln_g, cln_b, cw, gv_blk, gg_blk):
    t = proj.shape[0]
    taps = conv_w.shape[0]
    tt = _tile(t, 128, CONV_HALO)

    def body(gv_ref, gg_ref, gvh_ref, ggh_ref, cw_ref, cb_ref, lg_ref, lb_ref, o_ref, ext_ref):
        _, _, cn = _conv_stage(gv_ref, gg_ref, gvh_ref, ggh_ref, cw_ref, cb_ref, lg_ref, lb_ref, ext_ref,
                               pl.program_id(0) == 0, tt, taps)
        o_ref[...] = (cn * _sigmoid(cn)).astype(o_ref.dtype)

    return pl.pallas_call(
        body, name="conv_fwd", grid=(t // tt,),
        in_specs=_conv_specs(t, tt, cw, taps, gv_blk, gg_blk),
        out_specs=pl.BlockSpec((tt, cw), lambda i: (i, 0)),
        out_shape=jax.ShapeDtypeStruct((t, cw), MXU_DTYPE),
        scratch_shapes=[pltpu.VMEM((CONV_HALO + tt, cw), F32)],
        compiler_params=_cparams(("parallel",)),
    )(proj, proj, proj, proj, conv_w, conv_b, cln_g, cln_b)


def _conv_gate(c2, proj, b_pw2, cw, gc_blk):
    t = c2.shape[0]
    tt = _tile(t, 256, SUBLANE)

    def body(c_ref, g_ref, b_ref, o_ref):
        g = g_ref[...]
        o_ref[...] = ((c_ref[...] + b_ref[...]) * (g * _sigmoid(g))).astype(o_ref.dtype)

    return pl.pallas_call(
        body, name="conv_gate", grid=(t // tt,),
        in_specs=[pl.BlockSpec((tt, cw), lambda i: (i, 0)), pl.BlockSpec((tt, cw), lambda i: (i, gc_blk)),
                  _full((1, cw))],
        out_specs=pl.BlockSpec((tt, cw), lambda i: (i, 0)),
        out_shape=jax.ShapeDtypeStruct((t, cw), MXU_DTYPE),
        compiler_params=_cparams(("parallel",)),
    )(c2, proj, b_pw2)


def _conv_gate_bwd(dmix, c2, proj, b_pw2, cw, dm_blk, gc_blk):
    t = c2.shape[0]
    tt = _tile(t, 256, SUBLANE)

    def body(dm_ref, c_ref, g_ref, b_ref, dc2_o, dg_o, db_o):
        g = g_ref[...]
        sg = _sigmoid(g)
        dyc = dm_ref[...]
        dc2 = dyc * (g * sg)
        dc2_o[...] = dc2.astype(dc2_o.dtype)
        dg_o[...] = (dyc * (c_ref[...] + b_ref[...]) * (sg * (1.0 + g * (1.0 - sg)))).astype(dg_o.dtype)

        @pl.when(pl.program_id(0) == 0)
        def _():
            db_o[...] = jnp.zeros_like(db_o)

        db_o[...] += jnp.sum(dc2, axis=0, keepdims=True)

    row = pl.BlockSpec((tt, cw), lambda i: (i, 0))
    return pl.pallas_call(
        body, name="conv_gate_bwd", grid=(t // tt,),
        in_specs=[pl.BlockSpec((tt, cw), lambda i: (i, dm_blk)), row,
                  pl.BlockSpec((tt, cw), lambda i: (i, gc_blk)), _full((1, cw))],
        out_specs=[row, row, _full((1, cw))],
        out_shape=[jax.ShapeDtypeStruct((t, cw), MXU_DTYPE), jax.ShapeDtypeStruct((t, cw), MXU_DTYPE),
                   jax.ShapeDtypeStruct((1, cw), F32)],
        compiler_params=_cparams(("arbitrary",)),
    )(dmix, c2, proj, b_pw2)


def _conv_bwd_norm(proj, dcs, conv_w, conv_b, cln_g, cln_b, cw, gv_blk, gg_blk):
    t = proj.shape[0]
    taps = conv_w.shape[0]
    tt = _tile(t, 128, CONV_HALO)

    def body(gv_ref, gg_ref, gvh_ref, ggh_ref, cw_ref, cb_ref, lg_ref, lb_ref, dcs_ref,
             dc_o, dcw_o, dcb_o, dlg_o, dlb_o, ext_ref):
        chat, rstd, cn = _conv_stage(gv_ref, gg_ref, gvh_ref, ggh_ref, cw_ref, cb_ref, lg_ref, lb_ref, ext_ref,
                                     pl.program_id(0) == 0, tt, taps)
        s = _sigmoid(cn)
        dcn = dcs_ref[...] * (s * (1.0 + cn * (1.0 - s)))
        dchat = dcn * lg_ref[...]
        dc = rstd * (dchat - jnp.mean(dchat, axis=-1, keepdims=True)
                     - chat * jnp.mean(dchat * chat, axis=-1, keepdims=True))
        dc_o[...] = dc

        @pl.when(pl.program_id(0) == 0)
        def _():
            for ref in (dcw_o, dcb_o, dlg_o, dlb_o):
                ref[...] = jnp.zeros_like(ref)

        dlg_o[...] += jnp.sum(dcn * chat, axis=0, keepdims=True)
        dlb_o[...] += jnp.sum(dcn, axis=0, keepdims=True)
        dcb_o[...] += jnp.sum(dc, axis=0, keepdims=True)
        off = CONV_HALO - (taps - 1)
        for j in range(taps):
            dcw_o[j:j + 1, :] += jnp.sum(ext_ref[off + j:off + j + tt, :] * dc, axis=0, keepdims=True)

    vec = _full((1, cw))
    vecf = jax.ShapeDtypeStruct((1, cw), F32)
    return pl.pallas_call(
        body, name="conv_bwd_norm", grid=(t // tt,),
        in_specs=_conv_specs(t, tt, cw, taps, gv_blk, gg_blk) + [pl.BlockSpec((tt, cw), lambda i: (i, 0))],
        out_specs=[pl.BlockSpec((tt, cw), lambda i: (i, 0)), _full((taps, cw)), vec, vec, vec],
        out_shape=[jax.ShapeDtypeStruct((t, cw), F32), jax.ShapeDtypeStruct((taps, cw), F32), vecf, vecf, vecf],
        scratch_shapes=[pltpu.VMEM((CONV_HALO + tt, cw), F32)],
        compiler_params=_cparams(("arbitrary",)),
    )(proj, proj, proj, proj, conv_w, conv_b, cln_g, cln_b, dcs)


def _conv_bwd_glu(dc, proj, conv_w, cw, gv_blk, gg_blk):
    t = dc.shape[0]
    taps = conv_w.shape[0]
    tt = _tile(t, 128, CONV_HALO)
    hb = tt // CONV_HALO
    nhalo = t // CONV_HALO

    def body(dc_ref, dch_ref, gv_ref, gg_ref, cw_ref, dgv_o, dgg_o, ext_ref):
        last = pl.program_id(0) == pl.num_programs(0) - 1
        ext_ref[0:tt, :] = dc_ref[...]
        ext_ref[tt:tt + CONV_HALO, :] = jnp.where(last, 0.0, dch_ref[...])
        du = ext_ref[taps - 1:taps - 1 + tt, :] * cw_ref[0:1, :]
        for j in range(1, taps):
            du = du + ext_ref[taps - 1 - j:taps - 1 - j + tt, :] * cw_ref[j:j + 1, :]
        sg = _sigmoid(gg_ref[...])
        dgv_o[...] = (du * sg).astype(dgv_o.dtype)
        dgg_o[...] = (du * gv_ref[...] * sg * (1.0 - sg)).astype(dgg_o.dtype)

    row = pl.BlockSpec((tt, cw), lambda i: (i, 0))
    return pl.pallas_call(
        body, name="conv_bwd_glu", grid=(t // tt,),
        in_specs=[row, pl.BlockSpec((CONV_HALO, cw), lambda i: (jnp.minimum((i + 1) * hb, nhalo - 1), 0)),
                  pl.BlockSpec((tt, cw), lambda i: (i, gv_blk)), pl.BlockSpec((tt, cw), lambda i: (i, gg_blk)),
                  _full((taps, cw))],
        out_specs=[row, row],
        out_shape=[jax.ShapeDtypeStruct((t, cw), MXU_DTYPE)] * 2,
        scratch_shapes=[pltpu.VMEM((tt + CONV_HALO, cw), F32)],
        compiler_params=_cparams(("parallel",)),
    )(dc, dc, proj, proj, conv_w)


HBM_SPEC = pl.BlockSpec(memory_space=pltpu.HBM)


def _all_gather(shards, name):
    na = len(shards)

    def body(*refs):
        ins, outs = refs[:na], refs[na:2 * na]
        send_sems, recv_sems, local_sems = refs[2 * na:]
        x, y, c = lax.axis_index("x"), lax.axis_index("y"), lax.axis_index("c")
        me, sibling = (x, y, c), (x, y, 1 - c)
        chips = [(1 - x, y), (x, 1 - y), (1 - x, 1 - y)]

        def slot(px, py, pc):
            return 4 * px + 2 * py + pc

        def copy(a, k, block, to, src=None):
            dst = outs[a].at[slot(*block)]
            return pltpu.make_async_remote_copy(
                src_ref=dst if src is None else src, dst_ref=dst,
                send_sem=send_sems.at[a, k], recv_sem=recv_sems.at[a, k],
                device_id=to, device_id_type=MESH)

        mine = [pltpu.make_async_copy(ins[a], outs[a].at[slot(*me)], local_sems.at[a]) for a in range(na)]
        for cp in mine:
            cp.start()
        first = []
        for a in range(na):
            first.append(copy(a, 0, me, sibling, src=ins[a]))
            first += [copy(a, 1 + j, me, (*chip, c), src=ins[a]) for j, chip in enumerate(chips)]
        for cp in first:
            cp.start()
        passed = []
        for j, chip in enumerate(chips):
            for a in range(na):
                copy(a, 1 + j, (*chip, c), me).wait_recv()
                fwd = copy(a, 4 + j, (*chip, c), sibling)
                fwd.start()
                passed.append(fwd)
        for a in range(na):
            copy(a, 0, sibling, me).wait_recv()
            for j, chip in enumerate(chips):
                copy(a, 4 + j, (*chip, 1 - c), me).wait_recv()
        for cp in first + passed:
            cp.wait_send()
        for cp in mine:
            cp.wait()

    return pl.pallas_call(
        body, name=name,
        in_specs=[HBM_SPEC] * na, out_specs=[HBM_SPEC] * na,
        out_shape=[jax.ShapeDtypeStruct((N_DEV,) + s.shape, s.dtype) for s in shards],
        scratch_shapes=[pltpu.SemaphoreType.DMA((na, 7)), pltpu.SemaphoreType.DMA((na, 7)),
                        pltpu.SemaphoreType.DMA((na,))],
        compiler_params=pltpu.CompilerParams(has_side_effects=True),
    )(*shards)


def _exchange(parts, name):
    na = len(parts)

    def body(*refs):
        ins, outs = refs[:na], refs[na:2 * na]
        send_sems, recv_sems = refs[2 * na:]
        x, y, c = lax.axis_index("x"), lax.axis_index("y"), lax.axis_index("c")
        copies = []
        for k in range(1, N_DEV):
            px = 1 - x if k & 4 else x
            py = 1 - y if k & 2 else y
            pc = 1 - c if k & 1 else c
            for a in range(na):
                copies.append(pltpu.make_async_remote_copy(
                    src_ref=ins[a].at[4 * px + 2 * py + pc], dst_ref=outs[a].at[k - 1],
                    send_sem=send_sems.at[a, k - 1], recv_sem=recv_sems.at[a, k - 1],
                    device_id=(px, py, pc), device_id_type=MESH))
        for cp in copies:
            cp.start()
        for cp in copies:
            cp.wait()

    return pl.pallas_call(
        body, name=name,
        in_specs=[HBM_SPEC] * na, out_specs=[HBM_SPEC] * na,
        out_shape=[jax.ShapeDtypeStruct((N_DEV - 1,) + p.shape[1:], p.dtype) for p in parts],
        scratch_shapes=[pltpu.SemaphoreType.DMA((na, N_DEV - 1)), pltpu.SemaphoreType.DMA((na, N_DEV - 1))],
        compiler_params=pltpu.CompilerParams(has_side_effects=True),
    )(*parts)


def _cast(v, dtype, name):
    r, c = v.shape
    tr = _tile(r, 256, SUBLANE)

    def body(i_ref, o_ref):
        o_ref[...] = i_ref[...].astype(o_ref.dtype)

    return pl.pallas_call(
        body, name=name, grid=(r // tr,),
        in_specs=[pl.BlockSpec((tr, c), lambda i: (i, 0))],
        out_specs=pl.BlockSpec((tr, c), lambda i: (i, 0)),
        out_shape=jax.ShapeDtypeStruct((r, c), dtype),
        compiler_params=_cparams(("parallel",)),
    )(v)


def _adamw(w, m, v, recv, own, name):
    r, c = w.shape
    ns = recv.shape[0]
    tr = _tile(r, 128, SUBLANE)
    c1 = 1.0 - ADAM_B1 ** ADAM_STEP
    c2 = 1.0 - ADAM_B2 ** ADAM_STEP

    def body(*refs):
        if own is None:
            w_ref, m_ref, v_ref, rc_ref = refs[:4]
            g = rc_ref[0].astype(F32)
            start = 1
        else:
            w_ref, m_ref, v_ref, rc_ref, own_ref = refs[:5]
            g = own_ref[...]
            start = 0
        g_o, d_o, m_o, v_o = refs[-4:]
        for s in range(start, ns):
            g = g + rc_ref[s].astype(F32)
        mn = ADAM_B1 * m_ref[...] + (1.0 - ADAM_B1) * g
        vn = ADAM_B2 * v_ref[...] + (1.0 - ADAM_B2) * (g * g)
        m_hat = mn / c1
        v_hat = vn / c2
        g_o[...] = g
        d_o[...] = -ADAM_LR * (m_hat / (jnp.sqrt(v_hat) + ADAM_EPS) + ADAM_WD * w_ref[...])
        m_o[...] = mn
        v_o[...] = vn

    row = pl.BlockSpec((tr, c), lambda i: (i, 0))
    ins = [w, m, v, recv] + ([] if own is None else [own])
    in_specs = [row, row, row, pl.BlockSpec((ns, tr, c), lambda i: (0, i, 0))] + ([] if own is None else [row])
    return pl.pallas_call(
        body, name=name, grid=(r // tr,),
        in_specs=in_specs, out_specs=[row] * 4,
        out_shape=[jax.ShapeDtypeStruct((r, c), F32)] * 4,
        compiler_params=_cparams(("parallel",)),
    )(*ins)


def _to_t(v, nh, n, axis=-1):
    v = jnp.moveaxis(v, axis, -1)
    v = v.reshape(v.shape[:-1] + (nh, n)).swapaxes(-1, -2).reshape(v.shape)
    return jnp.moveaxis(v, -1, axis)


def _from_t(v, nh, n, axis=-1):
    v = jnp.moveaxis(v, axis, -1)
    v = v.reshape(v.shape[:-1] + (n, nh)).swapaxes(-1, -2).reshape(v.shape)
    return jnp.moveaxis(v, -1, axis)


def _pad_to(v, size, axis):
    pad = [(0, 0)] * v.ndim
    pad[axis] = (0, size - v.shape[axis])
    return jnp.pad(v, pad)


def kernel(x, norm_pre_g, w_in, mu_shift, w0, w_lora_up, a0, a_lora_up, k_k, k_a, r_k, lnx_g, lnx_b, conv_w, conv_b, cln_g, cln_b, w_pw2, b_pw2, w_out, norm_post_g, loss_target, m_norm_pre_g, m_w_in, m_mu_shift, m_w0, m_w_lora_up, m_a0, m_a_lora_up, m_k_k, m_k_a, m_r_k, m_lnx_g, m_lnx_b, m_conv_w, m_conv_b, m_cln_g, m_cln_b, m_w_pw2, m_b_pw2, m_w_out, m_norm_post_g, v_norm_pre_g, v_w_in, v_mu_shift, v_w0, v_w_lora_up, v_a0, v_a_lora_up, v_k_k, v_k_a, v_r_k, v_lnx_g, v_lnx_b, v_conv_w, v_conv_b, v_cln_g, v_cln_b, v_w_pw2, v_b_pw2, v_w_out, v_norm_post_g):
    args = dict(locals())
    t, d = x.shape[1], x.shape[2]
    hw = w0.shape[0]
    cw = conv_b.shape[0]
    nh, n = r_k.shape
    lw, la = w_lora_up.shape[0], a_lora_up.shape[0]
    taps = conv_w.shape[0]
    in_cols = w_in.shape[1] * N_DEV
    shift_cols = 3 * hw + lw + la
    assert in_cols == shift_cols + hw + 3 * cw and hw == cw and hw % LANE == 0 and LANE % nh == 0
    assert lw <= LORA_PAD and la <= LORA_PAD and taps - 1 <= CONV_HALO and hw % (2 * LORA_PAD) == 0
    q = hw // LANE
    gate_blk, gv_blk, gg_blk, gc_blk = 3, 4, 5, 6
    lo0 = 7 * hw
    lo_blk = lo0 // (2 * LORA_PAD)
    x2, tgt2 = x[0], loss_target[0]
    row = lambda v: v.reshape(1, -1)

    gathered = _all_gather(
        [_cast(w_in, MXU_DTYPE, "cast_w_in"), _cast(w_out, MXU_DTYPE, "cast_w_out"),
         _cast(w_pw2, MXU_DTYPE, "cast_w_pw2"), w_lora_up, a_lora_up, conv_w], "gather_weights")
    w_in_g, w_out_g, w_pw2_g, wup_w_g, wup_a_g, conv_w_g = gathered
    w_full = w_in_g.transpose(1, 0, 2).reshape(d, in_cols)
    c0 = shift_cols
    wp = jnp.concatenate([
        _to_t(w_full[:, 0:hw], nh, n), _to_t(w_full[:, hw:2 * hw], nh, n), _to_t(w_full[:, 2 * hw:3 * hw], nh, n),
        _to_t(w_full[:, c0:c0 + hw], nh, n), w_full[:, c0 + hw:],
        _pad_to(w_full[:, 3 * hw:3 * hw + lw], LORA_PAD, 1), _pad_to(w_full[:, 3 * hw + lw:c0], LORA_PAD, 1)], axis=1)
    w_out_f = w_out_g.reshape(N_DEV * w_out.shape[0], d)
    w_out_p = jnp.concatenate([_to_t(w_out_f[:hw], nh, n, axis=0), w_out_f[hw:]], axis=0)
    w_pw2_f = w_pw2_g.reshape(N_DEV * w_pw2.shape[0], cw)
    wup_w = _pad_to(_to_t(wup_w_g.transpose(1, 0, 2).reshape(lw, hw), nh, n), LORA_PAD, 0)
    wup_a = _pad_to(_to_t(wup_a_g.transpose(1, 0, 2).reshape(la, hw), nh, n), LORA_PAD, 0)
    conv_w_f = conv_w_g.transpose(1, 0, 2).reshape(taps, cw)
    mu_p = row(jnp.concatenate([
        _to_t(mu_shift[0:hw], nh, n), _to_t(mu_shift[hw:2 * hw], nh, n), _to_t(mu_shift[2 * hw:3 * hw], nh, n)]))
    mu_lo = row(jnp.concatenate([
        _pad_to(mu_shift[3 * hw:3 * hw + lw], LORA_PAD, 0), _pad_to(mu_shift[3 * hw + lw:], LORA_PAD, 0)]))
    tvec = lambda v: row(_to_t(v, nh, n))
    w0_t, a0_t, kk_t, ka_t, lg_t, lb_t = tvec(w0), tvec(a0), tvec(k_k), tvec(k_a), tvec(lnx_g), tvec(lnx_b)
    rk_t = row(r_k.T)
    head = jnp.arange(hw, dtype=jnp.int32) % nh
    e = (head[:, None] == jnp.arange(LANE, dtype=jnp.int32)[None, :]).astype(F32)
    et = e.T

    tiled = lambda v: jnp.tile(v.reshape(t, n, nh), (1, 1, LANE // nh))
    compact = lambda v: v.reshape(t, q, LANE)
    flat = lambda v: v.reshape(t, hw)

    h = _prenorm(x2, row(norm_pre_g))
    proj = _matmul(h, wp, "nn", F32, "mm_proj")
    r_a, w_a, kh_a, v_a, kn_a, b_a = _rwkv_pre(
        proj, mu_p, mu_lo, w0_t, a0_t, kk_t, ka_t, wup_w, wup_a, e, et, hw, lo_blk)
    r_tl, w_tl, kh_tl, kn_tl, b_tl = tiled(r_a), tiled(w_a), tiled(kh_a), tiled(kn_a), tiled(b_a)
    y_c, sa_c, ck = _wkv_fwd(w_tl, b_tl, kh_tl, r_tl, kn_tl, compact(v_a))
    y_a = flat(y_c)
    y_rwkv = _rwkv_post(y_a, r_a, kh_a, v_a, proj, lg_t, lb_t, rk_t, e, et, hw, n, gate_blk)
    cs = _conv_fwd(proj, conv_w_f, row(conv_b), row(cln_g), row(cln_b), cw, gv_blk, gg_blk)
    c2 = _matmul(cs, w_pw2_f, "nn", F32, "mm_pw2")
    y_conv = _conv_gate(c2, proj, row(b_pw2), cw, gc_blk)
    mix = jnp.concatenate([y_rwkv, y_conv], axis=1)
    out = _matmul(mix, w_out_p, "nn", F32, "mm_out")
    dout, dy, loss_part, d_post_g = _post_loss(out, x2, tgt2, row(norm_post_g))

    dmix = _matmul(dout, w_out_p, "nt", F32, "mm_dmix")
    d_w_out_p = _matmul(mix, dout, "tn", F32, "mm_dw_out")
    dc2, dgc, d_b_pw2 = _conv_gate_bwd(dmix, c2, proj, row(b_pw2), cw, 1, gc_blk)
    dcs = _matmul(dc2, w_pw2_f, "nt", F32, "mm_dcs")
    d_w_pw2 = _matmul(cs, dc2, "tn", F32, "mm_dw_pw2")
    dc, d_conv_w, d_conv_b, d_cln_g, d_cln_b = _conv_bwd_norm(
        proj, dcs, conv_w_f, row(conv_b), row(cln_g), row(cln_b), cw, gv_blk, gg_blk)
    dgv, dgg = _conv_bwd_glu(dc, proj, conv_w_f, cw, gv_blk, gg_blk)
    dgr, dy_rec, dr_bon, dkh_bon, dv_bon, d_lg_t, d_lb_t, d_rk_t = _rwkv_post_bwd(
        dmix, y_a, r_a, kh_a, v_a, proj, lg_t, lb_t, rk_t, e, et, hw, n, gate_blk)
    ck_i = ck.reshape(ck.shape[0], n, n, nh).transpose(0, 2, 1, 3).reshape(ck.shape)
    dv_c, dr_c, dw_c, db_c, dk_c, dkn_c = _wkv_bwd(
        r_tl, b_tl, kh_tl, w_tl, kn_tl, compact(dy_rec), compact(r_a), compact(w_a), compact(b_a),
        compact(kh_a), compact(kn_a), tiled(dy_rec), tiled(flat(sa_c)), tiled(v_a), ck_i, nh)
    dxs, dxs_lo, d_mu_p, d_mu_lo, d_w0_t, d_a0_t, d_kk_t, d_ka_t, d_wup_w, d_wup_a = _rwkv_pre_bwd(
        proj, mu_p, mu_lo, w0_t, a0_t, kk_t, ka_t, wup_w, wup_a, e, et, hw, lo_blk,
        flat(dr_c), flat(dw_c), flat(dk_c), flat(dv_c), flat(dkn_c), flat(db_c), dr_bon, dkh_bon, dv_bon)
    dshift = _shift_bwd(dxs, mu_p, "shift_bwd")
    dshift_lo = _shift_bwd(dxs_lo, mu_lo, "shift_bwd_lora")
    dproj = jnp.concatenate([dshift, dgr, dgv, dgg, dgc, dshift_lo], axis=1)
    d_wp = _matmul(h, dproj, "tn", F32, "mm_dw_in")
    dh = _matmul(dproj, wp, "nt", F32, "mm_dh")
    grad_x, d_pre_g = _prenorm_bwd(dh, x2, dy, row(norm_pre_g))

    d_w_full = jnp.concatenate([
        _from_t(d_wp[:, 0:hw], nh, n), _from_t(d_wp[:, hw:2 * hw], nh, n), _from_t(d_wp[:, 2 * hw:3 * hw], nh, n),
        d_wp[:, lo0:lo0 + lw], d_wp[:, lo0 + LORA_PAD:lo0 + LORA_PAD + la],
        _from_t(d_wp[:, 3 * hw:4 * hw], nh, n), d_wp[:, 4 * hw:lo0]], axis=1)
    sc = w_in.shape[1]
    d_w_in_parts = d_w_full.reshape(d, N_DEV, sc).transpose(1, 0, 2)
    d_w_out_f = jnp.concatenate([_from_t(d_w_out_p[:hw], nh, n, axis=0), d_w_out_p[hw:]], axis=0)
    d_w_out_parts = d_w_out_f.reshape((N_DEV,) + w_out.shape)
    d_w_pw2_parts = d_w_pw2.reshape((N_DEV,) + w_pw2.shape)
    colparts = lambda v: v.reshape(v.shape[0], N_DEV, v.shape[1] // N_DEV).transpose(1, 0, 2)
    d_wup_w_parts = colparts(_from_t(d_wup_w[:lw], nh, n))
    d_wup_a_parts = colparts(_from_t(d_wup_a[:la], nh, n))
    d_conv_w_parts = colparts(d_conv_w)
    d_mu = jnp.concatenate([
        _from_t(d_mu_p[0, 0:hw], nh, n), _from_t(d_mu_p[0, hw:2 * hw], nh, n), _from_t(d_mu_p[0, 2 * hw:3 * hw], nh, n),
        d_mu_lo[0, 0:lw], d_mu_lo[0, LORA_PAD:LORA_PAD + la]])
    ft = lambda v: _from_t(v[0], nh, n)
    small = {
        "norm_pre_g": d_pre_g[0], "mu_shift": d_mu, "w0": ft(d_w0_t), "a0": ft(d_a0_t), "k_k": ft(d_kk_t),
        "k_a": ft(d_ka_t), "r_k": d_rk_t[0].reshape(n, nh).T.reshape(-1), "lnx_g": ft(d_lg_t), "lnx_b": ft(d_lb_t),
        "conv_b": d_conv_b[0], "cln_g": d_cln_g[0], "cln_b": d_cln_b[0], "b_pw2": d_b_pw2[0],
        "norm_post_g": d_post_g[0]}
    small_names = list(small)
    packed = jnp.concatenate([small[k] for k in small_names] + [loss_part[0, 0:1]])
    plen = packed.shape[0]
    ppad = -(-plen // LANE) * LANE
    packed = _pad_to(packed, ppad, 0).reshape(1, ppad)

    me = 4 * lax.axis_index("x") + 2 * lax.axis_index("y") + lax.axis_index("c")
    own = lambda parts: lax.dynamic_index_in_dim(parts, me, 0, keepdims=False)
    wire = lambda parts, nm: _cast(parts.reshape(-1, parts.shape[-1]), WIRE_DTYPE, nm).reshape(parts.shape)
    recv = _exchange([wire(d_w_in_parts, "wire_w_in"), wire(d_w_out_parts, "wire_w_out"),
                      wire(d_w_pw2_parts, "wire_w_pw2"), d_wup_w_parts, d_wup_a_parts, d_conv_w_parts],
                     "exchange_grads")
    (packed_all,) = _all_gather([packed], "gather_small")

    res = {}
    sharded = [("w_in", d_w_in_parts), ("w_out", d_w_out_parts), ("w_pw2", d_w_pw2_parts),
               ("w_lora_up", d_wup_w_parts), ("a_lora_up", d_wup_a_parts), ("conv_w", d_conv_w_parts)]
    for (nm, parts), rc in zip(sharded, recv):
        res[nm] = _adamw(args[nm], args["m_" + nm], args["v_" + nm], rc, own(parts), "adamw_" + nm)
    w_small = _pad_to(jnp.concatenate([args[k].reshape(-1) for k in small_names]), ppad, 0).reshape(1, ppad)
    m_small = _pad_to(jnp.concatenate([args["m_" + k].reshape(-1) for k in small_names]), ppad, 0).reshape(1, ppad)
    v_small = _pad_to(jnp.concatenate([args["v_" + k].reshape(-1) for k in small_names]), ppad, 0).reshape(1, ppad)
    g_s, d_s, m_s, v_s = _adamw(w_small, m_small, v_small, packed_all, None, "adamw_small")
    off = 0
    for k in small_names:
        size = args[k].size
        res[k] = tuple(o[0, off:off + size].reshape(args[k].shape) for o in (g_s, d_s, m_s, v_s))
        off += size
    loss = g_s[0, plen - 1]

    order = ["norm_pre_g", "w_in", "mu_shift", "w0", "w_lora_up", "a0", "a_lora_up", "k_k", "k_a", "r_k",
             "lnx_g", "lnx_b", "conv_w", "conv_b", "cln_g", "cln_b", "w_pw2", "b_pw2", "w_out", "norm_post_g"]
    outs = [loss, grad_x[None]]
    for slot in range(4):
        outs += [res[k][slot] for k in order]
    return tuple(outs)
```

```python
import functools

import jax
import jax.numpy as jnp
from jax import lax
from jax.experimental import pallas as pl
from jax.experimental.pallas import tpu as pltpu

F32 = jnp.float32
MXU_DTYPE = jnp.bfloat16
WIRE_DTYPE = jnp.bfloat16
HI = lax.Precision.HIGHEST
SEG_PRECISION = lax.Precision.HIGH

NORM_EPS = 1e-6
LN_EPS = 1e-5
GN_EPS_PER_CHANNEL = 1e-5
KK_EPS = 1e-12
ADAM_LR = 0.001
ADAM_B1 = 0.9
ADAM_B2 = 0.999
ADAM_EPS = 1e-08
ADAM_WD = 0.01
ADAM_STEP = 10

LANE = 128
SUBLANE = 8
LORA_PAD = 128
CONV_HALO = 32
N_DEV = 8
VMEM_LIMIT = 56 * 1024 * 1024
WKV_CHUNK = 32
RWKV_ROWS = 64
MESH = pl.DeviceIdType.MESH


def _tile(n, target, mult):
    if n <= target:
        return n
    best = None
    for d in range(mult, target + 1, mult):
        if n % d == 0:
            best = d
    assert best is not None, (n, target, mult)
    return best


def _cparams(sem=None):
    return pltpu.CompilerParams(dimension_semantics=sem, vmem_limit_bytes=VMEM_LIMIT)


def _sigmoid(x):
    return 1.0 / (1.0 + jnp.exp(-x))


def _full(shape):
    nd = len(shape)
    return pl.BlockSpec(shape, lambda *_: (0,) * nd)


def _compact_spec(tt, q):
    return pl.BlockSpec((tt * q, LANE), lambda i: (i, 0))


def _load_compact(ref, tt, q):
    return jnp.concatenate([ref[pl.ds(p, tt, stride=q), :] for p in range(q)], axis=1)


def _store_compact(ref, val, tt, q):
    for p in range(q):
        ref[pl.ds(p, tt, stride=q), :] = val[:, p * LANE:(p + 1) * LANE]


def _matmul(a, b, mode, out_dtype, name, exchange=()):
    if mode == "nn":
        (m, k), (k2, n) = a.shape, b.shape
    elif mode == "nt":
        (m, k), (n, k2) = a.shape, b.shape
    else:
        (k, m), (k2, n) = a.shape, b.shape
    assert k == k2, (a.shape, b.shape, mode)
    tm, tn, tk = _tile(m, 1024, LANE), _tile(n, 768, LANE), _tile(k, 2048, LANE)
    nk = k // tk
    ne = len(exchange)
    grid = (m // tm, n // tn, nk)
    if mode == "nn":
        a_spec = pl.BlockSpec((tm, tk), lambda i, j, kk: (i, kk))
        b_spec = pl.BlockSpec((tk, tn), lambda i, j, kk: (kk, j))
        dims = (((1,), (0,)), ((), ()))
    elif mode == "nt":
        a_spec = pl.BlockSpec((tm, tk), lambda i, j, kk: (i, kk))
        b_spec = pl.BlockSpec((tn, tk), lambda i, j, kk: (j, kk))
        dims = (((1,), (1,)), ((), ()))
    else:
        a_spec = pl.BlockSpec((tk, tm), lambda i, j, kk: (kk, i))
        b_spec = pl.BlockSpec((tk, tn), lambda i, j, kk: (kk, j))
        dims = (((0,), (0,)), ((), ()))

    def body(*refs):
        a_ref, b_ref = refs[:2]
        parts = refs[2:2 + ne]
        o_ref = refs[2 + ne]
        recvs = refs[3 + ne:3 + 2 * ne]
        acc_ref = refs[3 + 2 * ne]
        i, j, kk = pl.program_id(0), pl.program_id(1), pl.program_id(2)
        if ne:
            copies = _exchange_copies(parts, recvs, *refs[4 + 2 * ne:])

            @pl.when((i == 0) & (j == 0) & (kk == 0))
            def _():
                for cp in copies:
                    cp.start()

        @pl.when(kk == 0)
        def _():
            acc_ref[...] = jnp.zeros_like(acc_ref)

        acc_ref[...] += lax.dot_general(a_ref[...], b_ref[...], dims, preferred_element_type=F32)

        @pl.when(kk == nk - 1)
        def _():
            o_ref[...] = acc_ref[...].astype(o_ref.dtype)

        if ne:
            @pl.when((i == grid[0] - 1) & (j == grid[1] - 1) & (kk == nk - 1))
            def _():
                for cp in copies:
                    cp.wait()

    res = pl.pallas_call(
        body, name=name,
        grid=grid,
        in_specs=[a_spec, b_spec] + [HBM_SPEC] * ne,
        out_specs=[pl.BlockSpec((tm, tn), lambda i, j, kk: (i, j))] + [HBM_SPEC] * ne,
        out_shape=[jax.ShapeDtypeStruct((m, n), out_dtype)] + _exchange_shapes(exchange),
        scratch_shapes=[pltpu.VMEM((tm, tn), F32)] + _exchange_sems(ne),
        compiler_params=_cparams(("arbitrary",) * 3 if ne else ("parallel", "parallel", "arbitrary")),
    )(a, b, *exchange)
    return res if ne else res[0]


def _prenorm(x, g):
    t, d = x.shape
    tt = _tile(t, 256, SUBLANE)

    def body(x_ref, g_ref, h_ref):
        xv = x_ref[...]
        rinv = lax.rsqrt(jnp.mean(xv * xv, axis=-1, keepdims=True) + NORM_EPS)
        h_ref[...] = (xv * rinv * g_ref[...]).astype(h_ref.dtype)

    return pl.pallas_call(
        body, name="prenorm", grid=(t // tt,),
        in_specs=[pl.BlockSpec((tt, d), lambda i: (i, 0)), _full((1, d))],
        out_specs=pl.BlockSpec((tt, d), lambda i: (i, 0)),
        out_shape=jax.ShapeDtypeStruct((t, d), MXU_DTYPE),
        compiler_params=_cparams(("parallel",)),
    )(x, g)


def _post_loss(out, x, target, g):
    t, d = out.shape
    tt = _tile(t, 128, SUBLANE)

    def body(o_ref, x_ref, t_ref, g_ref, dout_ref, dy_ref, loss_ref, dg_ref):
        i = pl.program_id(0)
        ov = o_ref[...]
        rinv = lax.rsqrt(jnp.mean(ov * ov, axis=-1, keepdims=True) + NORM_EPS)
        nv = ov * rinv
        gv = g_ref[...]
        err = x_ref[...] + nv * gv - t_ref[...]
        part = 0.5 * jnp.sum(jnp.mean(err * err, axis=-1, keepdims=True), axis=0, keepdims=True)
        dy = err * (1.0 / d)
        dy_ref[...] = dy
        dn = dy * gv
        dout = rinv * (dn - nv * jnp.mean(dn * nv, axis=-1, keepdims=True))
        dout_ref[...] = dout.astype(dout_ref.dtype)
        dg = jnp.sum(dy * nv, axis=0, keepdims=True)

        @pl.when(i == 0)
        def _():
            loss_ref[...] = jnp.zeros_like(loss_ref)
            dg_ref[...] = jnp.zeros_like(dg_ref)

        loss_ref[...] += jnp.broadcast_to(part, loss_ref.shape)
        dg_ref[...] += dg

    row = pl.BlockSpec((tt, d), lambda i: (i, 0))
    return pl.pallas_call(
        body, name="post_loss", grid=(t // tt,),
        in_specs=[row, row, row, _full((1, d))],
        out_specs=[row, row, _full((1, LANE)), _full((1, d))],
        out_shape=[jax.ShapeDtypeStruct((t, d), MXU_DTYPE), jax.ShapeDtypeStruct((t, d), F32),
                   jax.ShapeDtypeStruct((1, LANE), F32), jax.ShapeDtypeStruct((1, d), F32)],
        compiler_params=_cparams(("arbitrary",)),
    )(out, x, target, g)


def _prenorm_bwd(dh, x, dy, g):
    t, d = x.shape
    tt = _tile(t, 128, SUBLANE)

    def body(dh_ref, x_ref, dy_ref, g_ref, gx_ref, dg_ref):
        i = pl.program_id(0)
        xv = x_ref[...]
        rinv = lax.rsqrt(jnp.mean(xv * xv, axis=-1, keepdims=True) + NORM_EPS)
        nx = xv * rinv
        dhv = dh_ref[...]
        dnx = dhv * g_ref[...]
        dx = rinv * (dnx - nx * jnp.mean(dnx * nx, axis=-1, keepdims=True))
        gx_ref[...] = dy_ref[...] + dx

        @pl.when(i == 0)
        def _():
            dg_ref[...] = jnp.zeros_like(dg_ref)

        dg_ref[...] += jnp.sum(dhv * nx, axis=0, keepdims=True)

    row = pl.BlockSpec((tt, d), lambda i: (i, 0))
    return pl.pallas_call(
        body, name="prenorm_bwd", grid=(t // tt,),
        in_specs=[row, row, row, _full((1, d))],
        out_specs=[row, _full((1, d))],
        out_shape=[jax.ShapeDtypeStruct((t, d), F32), jax.ShapeDtypeStruct((1, d), F32)],
        compiler_params=_cparams(("arbitrary",)),
    )(dh, x, dy, g)


def _segsum(v, e_ref, et_ref):
    s = jnp.dot(v, e_ref[...], preferred_element_type=F32, precision=SEG_PRECISION)
    return jnp.dot(s, et_ref[...], preferred_element_type=F32, precision=SEG_PRECISION)


def _shifted(cur_ref, prev_ref, lo, hi, first):
    cur = cur_ref[:, lo:hi]
    last = jnp.where(first, 0.0, prev_ref[SUBLANE - 1:SUBLANE, lo:hi])
    prev = pltpu.roll(cur, 1, 0)
    rows = lax.broadcasted_iota(jnp.int32, cur.shape, 0)
    return cur, jnp.where(rows == 0, last, prev)


def _rwkv_mix(main_ref, mainp_ref, lo_ref, lop_ref, mu_ref, mulo_ref, w0_ref, a0_ref, kk_ref, ka_ref,
              wupw_ref, wupa_ref, e_ref, et_ref, hw, first):
    def xs(cur_ref, prev_ref, m_ref, lo, hi):
        cur, prev = _shifted(cur_ref, prev_ref, lo, hi, first)
        return cur + (prev - cur) * m_ref[:, lo:hi], prev - cur

    out = {}
    out["r"], out["r_d"] = xs(main_ref, mainp_ref, mu_ref, 0, hw)
    out["k"], out["k_d"] = xs(main_ref, mainp_ref, mu_ref, hw, 2 * hw)
    out["v"], out["v_d"] = xs(main_ref, mainp_ref, mu_ref, 2 * hw, 3 * hw)
    out["wl"], out["wl_d"] = xs(lo_ref, lop_ref, mulo_ref, 0, LORA_PAD)
    out["al"], out["al_d"] = xs(lo_ref, lop_ref, mulo_ref, LORA_PAD, 2 * LORA_PAD)
    th = jnp.tanh(out["wl"])
    zw = w0_ref[...] + jnp.dot(th, wupw_ref[...], preferred_element_type=F32, precision=HI)
    u = -zw
    softplus = jnp.maximum(u, 0.0) + jnp.log(1.0 + jnp.exp(-jnp.abs(u)))
    wlog = -softplus - 0.5
    ew = jnp.exp(wlog)
    za = a0_ref[...] + jnp.dot(out["al"], wupa_ref[...], preferred_element_type=F32, precision=HI)
    a = _sigmoid(za)
    kkr = out["k"] * kk_ref[...]
    nr = jnp.sqrt(_segsum(kkr * kkr, e_ref, et_ref))
    nrm = jnp.maximum(nr, KK_EPS)
    out.update(th=th, zw=zw, ew=ew, decay=jnp.exp(-ew), a=a, kkr=kkr, nr=nr, nrm=nrm, kk=kkr / nrm)
    out["kh"] = out["k"] * (1.0 + (a - 1.0) * ka_ref[...])
    return out


def _mix_specs(tt, hw, lo_blk):
    mw, lw2 = 3 * hw, 2 * LORA_PAD
    before = lambda i: jnp.maximum(i * (tt // SUBLANE) - 1, 0)
    vec = _full((1, hw))
    return [pl.BlockSpec((tt, mw), lambda i: (i, 0)), pl.BlockSpec((SUBLANE, mw), lambda i: (before(i), 0)),
            pl.BlockSpec((tt, lw2), lambda i: (i, lo_blk)), pl.BlockSpec((SUBLANE, lw2), lambda i: (before(i), lo_blk)),
            _full((1, mw)), _full((1, lw2)), vec, vec, vec, vec, _full((LORA_PAD, hw)), _full((LORA_PAD, hw)),
            _full((hw, LANE)), _full((LANE, hw))]


def _rwkv_pre(proj, mu, mu_lo, w0, a0, k_k, k_a, wup_w, wup_a, e, et, hw, lo_blk):
    t = proj.shape[0]
    tt = _tile(t, RWKV_ROWS, SUBLANE)

    def body(*refs):
        r_o, w_o, kh_o, v_o, kn_o, b_o = refs[-6:]
        f = _rwkv_mix(*refs[:-6], hw, pl.program_id(0) == 0)
        for ref, val in ((r_o, f["r"]), (w_o, f["decay"]), (kh_o, f["kh"]), (v_o, f["v"]), (kn_o, -f["kk"]),
                         (b_o, f["kk"] * f["a"])):
            _store_compact(ref, val, tt, q)

    q = hw // LANE
    return pl.pallas_call(
        body, name="rwkv_pre", grid=(t // tt,),
        in_specs=_mix_specs(tt, hw, lo_blk),
        out_specs=[_compact_spec(tt, q)] * 6,
        out_shape=[jax.ShapeDtypeStruct((t * q, LANE), F32)] * 6,
        compiler_params=_cparams(("parallel",)),
    )(proj, proj, proj, proj, mu, mu_lo, w0, a0, k_k, k_a, wup_w, wup_a, e, et)


def _rwkv_post_math(y, r, kh, v, g, lnx_g, lnx_b, r_k, e_ref, et_ref, n):
    mean = _segsum(y, e_ref, et_ref) * (1.0 / n)
    yc = y - mean
    var = _segsum(yc * yc, e_ref, et_ref) * (1.0 / n)
    rstd = lax.rsqrt(var + GN_EPS_PER_CHANNEL * n)
    yn = yc * rstd
    s = _segsum(r * kh * r_k, e_ref, et_ref)
    y3 = yn * lnx_g + lnx_b + s * v
    sg = _sigmoid(g)
    return yn, rstd, s, y3, sg


def _rwkv_post(y, r, kh, v, proj, lnx_g, lnx_b, r_k, e, et, hw, n, gate_blk):
    t = proj.shape[0]
    tt = _tile(t, RWKV_ROWS, SUBLANE)
    q = hw // LANE

    def body(y_ref, r_ref, kh_ref, v_ref, g_ref, lg_ref, lb_ref, rk_ref, e_ref, et_ref, o_ref):
        g = g_ref[...]
        y, r, kh, v = (_load_compact(ref, tt, q) for ref in (y_ref, r_ref, kh_ref, v_ref))
        _, _, _, y3, sg = _rwkv_post_math(y, r, kh, v, g, lg_ref[...], lb_ref[...], rk_ref[...], e_ref, et_ref, n)
        o_ref[...] = (y3 * (g * sg)).astype(o_ref.dtype)

    row = pl.BlockSpec((tt, hw), lambda i: (i, 0))
    comp = _compact_spec(tt, q)
    vec = _full((1, hw))
    return pl.pallas_call(
        body, name="rwkv_post", grid=(t // tt,),
        in_specs=[comp, comp, comp, comp, pl.BlockSpec((tt, hw), lambda i: (i, gate_blk)),
                  vec, vec, vec, _full((hw, LANE)), _full((LANE, hw))],
        out_specs=row,
        out_shape=jax.ShapeDtypeStruct((t, hw), MXU_DTYPE),
        compiler_params=_cparams(("parallel",)),
    )(y, r, kh, v, proj, lnx_g, lnx_b, r_k, e, et)


def _rwkv_post_bwd(dmix, y, r, kh, v, proj, lnx_g, lnx_b, r_k, e, et, hw, n, gate_blk):
    t = proj.shape[0]
    tt = _tile(t, RWKV_ROWS, SUBLANE)
    q = hw // LANE

    def body(dm_ref, y_ref, r_ref, kh_ref, v_ref, g_ref, lg_ref, lb_ref, rk_ref, e_ref, et_ref,
             dg_o, dy_o, dr_o, dkh_o, dv_o, dlg_o, dlb_o, drk_o):
        i = pl.program_id(0)
        g, rk, lg = g_ref[...], rk_ref[...], lg_ref[...]
        y, r, kh, v = (_load_compact(ref, tt, q) for ref in (y_ref, r_ref, kh_ref, v_ref))
        yn, rstd, s, y3, sg = _rwkv_post_math(y, r, kh, v, g, lg, lb_ref[...], rk, e_ref, et_ref, n)
        dyr = dm_ref[...]
        dy3 = dyr * (g * sg)
        dg_o[...] = (dyr * y3 * (sg * (1.0 + g * (1.0 - sg)))).astype(dg_o.dtype)
        ds = _segsum(dy3 * v, e_ref, et_ref)
        _store_compact(dv_o, dy3 * s, tt, q)
        _store_compact(dr_o, ds * kh * rk, tt, q)
        _store_compact(dkh_o, ds * r * rk, tt, q)
        dyn = dy3 * lg
        m1 = _segsum(dyn, e_ref, et_ref) * (1.0 / n)
        m2 = _segsum(dyn * yn, e_ref, et_ref) * (1.0 / n)
        _store_compact(dy_o, rstd * (dyn - m1 - yn * m2), tt, q)

        @pl.when(i == 0)
        def _():
            dlg_o[...] = jnp.zeros_like(dlg_o)
            dlb_o[...] = jnp.zeros_like(dlb_o)
            drk_o[...] = jnp.zeros_like(drk_o)

        dlg_o[...] += jnp.sum(dy3 * yn, axis=0, keepdims=True)
        dlb_o[...] += jnp.sum(dy3, axis=0, keepdims=True)
        drk_o[...] += jnp.sum(ds * r * kh, axis=0, keepdims=True)

    row = pl.BlockSpec((tt, hw), lambda i: (i, 0))
    comp = _compact_spec(tt, q)
    vec = _full((1, hw))
    rowf = jax.ShapeDtypeStruct((t * q, LANE), F32)
    vecf = jax.ShapeDtypeStruct((1, hw), F32)
    return pl.pallas_call(
        body, name="rwkv_post_bwd", grid=(t // tt,),
        in_specs=[row, comp, comp, comp, comp, pl.BlockSpec((tt, hw), lambda i: (i, gate_blk)),
                  vec, vec, vec, _full((hw, LANE)), _full((LANE, hw))],
        out_specs=[row, comp, comp, comp, comp, vec, vec, vec],
        out_shape=[jax.ShapeDtypeStruct((t, hw), MXU_DTYPE), rowf, rowf, rowf, rowf, vecf, vecf, vecf],
        compiler_params=_cparams(("arbitrary",)),
    )(dmix, y, r, kh, v, proj, lnx_g, lnx_b, r_k, e, et)


def _rwkv_pre_bwd(proj, mu, mu_lo, w0, a0, k_k, k_a, wup_w, wup_a, e, et, hw, lo_blk,
                  dr_rec, dw_rec, dkh_rec, dv_rec, dkn_rec, db_rec, dr_bon, dkh_bon, dv_bon):
    t = proj.shape[0]
    mw, lw2 = 3 * hw, 2 * LORA_PAD
    tt = _tile(t, RWKV_ROWS, SUBLANE)
    q = hw // LANE
    n_in = 14

    def body(*refs):
        mix_refs = refs[:n_in]
        drr, dwr, dkhr, dvr, dknr, db, drb, dkhb, dvb = (
            _load_compact(ref, tt, q) for ref in refs[n_in:n_in + 9])
        dxs_o, dxl_o, dmu_o, dmul_o, dw0_o, da0_o, dkk_o, dka_o, dwupw_o, dwupa_o = refs[n_in + 9:]
        kk_ref, ka_ref, wupw_ref, wupa_ref, e_ref, et_ref = mix_refs[8:14]
        i = pl.program_id(0)
        f = _rwkv_mix(*mix_refs, hw, i == 0)
        k, a, kk, nrm = f["k"], f["a"], f["kk"], f["nrm"]
        k_a, k_k = ka_ref[...], kk_ref[...]
        dr = drr + drb
        dkh = dkhr + dkhb
        dv = dvr + dvb
        da = db * kk + dkh * k * k_a
        dkk = db * a - dknr
        dk = dkh * (1.0 + (a - 1.0) * k_a)
        dka = jnp.sum(dkh * k * (a - 1.0), axis=0, keepdims=True)
        proj_kk = _segsum(dkk * kk, e_ref, et_ref)
        dkkr = jnp.where(f["nr"] > KK_EPS, (dkk - kk * proj_kk) / nrm, dkk * (1.0 / KK_EPS))
        dk = dk + dkkr * k_k
        dkk_w = jnp.sum(dkkr * k, axis=0, keepdims=True)
        dza = da * a * (1.0 - a)
        dzw = dwr * f["decay"] * (-f["ew"]) * _sigmoid(-f["zw"])
        nt_dims = (((1,), (1,)), ((), ()))
        tn_dims = (((0,), (0,)), ((), ()))
        dal = lax.dot_general(dza, wupa_ref[...], nt_dims, preferred_element_type=F32, precision=HI)
        dth = lax.dot_general(dzw, wupw_ref[...], nt_dims, preferred_element_type=F32, precision=HI)
        dwl = dth * (1.0 - f["th"] * f["th"])
        dxs_o[:, 0:hw] = dr
        dxs_o[:, hw:2 * hw] = dk
        dxs_o[:, 2 * hw:3 * hw] = dv
        dxl_o[:, 0:LORA_PAD] = dwl
        dxl_o[:, LORA_PAD:lw2] = dal

        @pl.when(i == 0)
        def _():
            for ref in (dmu_o, dmul_o, dw0_o, da0_o, dkk_o, dka_o, dwupw_o, dwupa_o):
                ref[...] = jnp.zeros_like(ref)

        def colsum(v):
            return jnp.sum(v, axis=0, keepdims=True)

        dmu_o[:, 0:hw] += colsum(dr * f["r_d"])
        dmu_o[:, hw:2 * hw] += colsum(dk * f["k_d"])
        dmu_o[:, 2 * hw:3 * hw] += colsum(dv * f["v_d"])
        dmul_o[:, 0:LORA_PAD] += colsum(dwl * f["wl_d"])
        dmul_o[:, LORA_PAD:lw2] += colsum(dal * f["al_d"])
        dw0_o[...] += colsum(dzw)
        da0_o[...] += colsum(dza)
        dkk_o[...] += dkk_w
        dka_o[...] += dka
        dwupw_o[...] += lax.dot_general(f["th"], dzw, tn_dims, preferred_element_type=F32, precision=HI)
        dwupa_o[...] += lax.dot_general(f["al"], dza, tn_dims, preferred_element_type=F32, precision=HI)

    vec = _full((1, hw))
    vecf = jax.ShapeDtypeStruct((1, hw), F32)
    return pl.pallas_call(
        body, name="rwkv_pre_bwd", grid=(t // tt,),
        in_specs=_mix_specs(tt, hw, lo_blk) + [_compact_spec(tt, q)] * 9,
        out_specs=[pl.BlockSpec((tt, mw), lambda i: (i, 0)), pl.BlockSpec((tt, lw2), lambda i: (i, 0)),
                   _full((1, mw)), _full((1, lw2)), vec, vec, vec, vec,
                   _full((LORA_PAD, hw)), _full((LORA_PAD, hw))],
        out_shape=[jax.ShapeDtypeStruct((t, mw), F32), jax.ShapeDtypeStruct((t, lw2), F32),
                   jax.ShapeDtypeStruct((1, mw), F32), jax.ShapeDtypeStruct((1, lw2), F32),
                   vecf, vecf, vecf, vecf,
                   jax.ShapeDtypeStruct((LORA_PAD, hw), F32), jax.ShapeDtypeStruct((LORA_PAD, hw), F32)],
        compiler_params=_cparams(("arbitrary",)),
    )(proj, proj, proj, proj, mu, mu_lo, w0, a0, k_k, k_a, wup_w, wup_a, e, et,
      dr_rec, dw_rec, dkh_rec, dv_rec, dkn_rec, db_rec, dr_bon, dkh_bon, dv_bon)


def _shift_bwd(dxs, mu, name):
    t, sw = dxs.shape
    tt = _tile(t, 256, SUBLANE)
    nblk = t // SUBLANE

    def body(d_ref, nxt_ref, mu_ref, o_ref):
        last = pl.program_id(0) == pl.num_programs(0) - 1
        cur = d_ref[...]
        first_next = jnp.where(last, 0.0, nxt_ref[0:1, :])
        nxt = pltpu.roll(cur, tt - 1, 0)
        rows = lax.broadcasted_iota(jnp.int32, cur.shape, 0)
        nxt = jnp.where(rows == tt - 1, first_next, nxt)
        m = mu_ref[...]
        o_ref[...] = (cur * (1.0 - m) + nxt * m).astype(o_ref.dtype)

    return pl.pallas_call(
        body, name=name, grid=(t // tt,),
        in_specs=[pl.BlockSpec((tt, sw), lambda i: (i, 0)),
                  pl.BlockSpec((SUBLANE, sw), lambda i: (jnp.minimum((i + 1) * (tt // SUBLANE), nblk - 1), 0)),
                  _full((1, sw))],
        out_specs=pl.BlockSpec((tt, sw), lambda i: (i, 0)),
        out_shape=jax.ShapeDtypeStruct((t, sw), MXU_DTYPE),
        compiler_params=_cparams(("parallel",)),
    )(dxs, dxs, mu)


def _tree_sum(parts):
    while len(parts) > 1:
        parts = [parts[p] + parts[p + 1] for p in range(0, len(parts) - 1, 2)] + ([parts[-1]] if len(parts) % 2 else [])
    return parts[0]


def _tile_rows(src_ref, dst_ref, tc, nh):
    rep = LANE // nh
    lane_group = lax.broadcasted_iota(jnp.int32, src_ref.shape[1:], 1) // nh

    def convert(ts, carry):
        v = src_ref[ts]
        for grp in range(rep):
            m = jnp.where(lane_group == grp, v, 0.0)
            shift = nh
            while shift < LANE:
                m = m + pltpu.roll(m, shift, 1)
                shift *= 2
            dst_ref[ts, grp] = m
        return carry

    lax.fori_loop(0, tc, convert, 0)


def _wkv_fwd(w_c, b_c, k_c, r_c, kn_c, v_c, nh):
    t, q, _ = v_c.shape
    rep = LANE // nh
    n = q * rep
    tc = _tile(t, WKV_CHUNK, 1)
    nacc = 4

    def body(w_ref, b_ref, k_ref, r_ref, kn_ref, v_ref, y_ref, sa_ref, ck_ref, s_ref, wt, bt, kt, rt, knt):
        @pl.when(pl.program_id(0) == 0)
        def _():
            s_ref[...] = jnp.zeros_like(s_ref)

        ck_ref[0] = s_ref[...]
        for src, dst in ((w_ref, wt), (b_ref, bt), (k_ref, kt), (r_ref, rt), (kn_ref, knt)):
            _tile_rows(src, dst, tc, nh)

        def row(ref, ts, j):
            return ref[ts, j % rep, pl.ds(j // rep, 1), :]

        def step(ts, carry):
            vt = v_ref[ts]
            acc = [None] * nacc
            for j in range(n):
                term = s_ref[j] * row(knt, ts, j)
                acc[j % nacc] = term if acc[j % nacc] is None else acc[j % nacc] + term
            sa = _tree_sum(acc)
            sa_ref[ts] = sa
            acc = [None] * nacc
            for j in range(n):
                sj = s_ref[j] * row(wt, ts, j) + sa * row(bt, ts, j) + vt * row(kt, ts, j)
                s_ref[j] = sj
                term = sj * row(rt, ts, j)
                acc[j % nacc] = term if acc[j % nacc] is None else acc[j % nacc] + term
            y_ref[ts] = _tree_sum(acc)
            return carry

        lax.fori_loop(0, tc, step, 0)

    comp = pl.BlockSpec((tc, q, LANE), lambda c: (c, 0, 0))
    return pl.pallas_call(
        body, name="wkv_fwd", grid=(t // tc,),
        in_specs=[comp] * 6,
        out_specs=[comp, comp, pl.BlockSpec((1, n, q, LANE), lambda c: (c, 0, 0, 0))],
        out_shape=[jax.ShapeDtypeStruct((t, q, LANE), F32), jax.ShapeDtypeStruct((t, q, LANE), F32),
                   jax.ShapeDtypeStruct((t // tc, n, q, LANE), F32)],
        scratch_shapes=[pltpu.VMEM((n, q, LANE), F32)] + [pltpu.VMEM((tc, rep, q, LANE), F32)] * 5,
        compiler_params=_cparams(("arbitrary",)),
    )(w_c, b_c, k_c, r_c, kn_c, v_c)


def _wkv_bwd(r_c, w_c, b_c, k_c, kn_c, dy_c, sa_c, v_c, ck_i, nh, exchange=()):
    t, q, _ = dy_c.shape
    rep = LANE // nh
    n = q * rep
    tc = _tile(t, WKV_CHUNK, 1)
    nc = t // tc
    nacc = 2
    ne = len(exchange)
    n_in = 9 + ne

    def body(*refs):
        rc_ref, wc_ref, bc_ref, kc_ref, knc_ref, dyc_ref, sac_ref, vc_ref, ck_ref = refs[:9]
        parts = refs[9:n_in]
        dv_o, dr_o, dw_o, db_o, dk_o, dkn_o = refs[n_in:n_in + 6]
        recvs = refs[n_in + 6:n_in + 6 + ne]
        hist, g_ref, gp_ref, dsat_ref, rt, wt, bt, kt, knt, dyt, sat, vt = refs[n_in + 6 + ne:n_in + 18 + ne]
        c = pl.program_id(0)
        if ne:
            copies = _exchange_copies(parts, recvs, *refs[n_in + 18 + ne:])

            @pl.when(c == 0)
            def _():
                for cp in copies:
                    cp.start()

        @pl.when(c == 0)
        def _():
            g_ref[...] = jnp.zeros_like(g_ref)
            gp_ref[...] = jnp.zeros_like(gp_ref)

        for src, dst in ((rc_ref, rt), (wc_ref, wt), (bc_ref, bt), (kc_ref, kt), (knc_ref, knt),
                         (dyc_ref, dyt), (sac_ref, sat), (vc_ref, vt)):
            _tile_rows(src, dst, tc, nh)

        def row(ref, ts, idx):
            return ref[ts, idx % rep, pl.ds(idx // rep, 1), :]

        hist[0] = ck_ref[0]

        def fstep(ts, carry):
            wv, bv, kv = wc_ref[ts], bc_ref[ts], kc_ref[ts]
            for i in range(n):
                hist[ts + 1, i] = hist[ts, i] * wv + row(sat, ts, i) * bv + row(vt, ts, i) * kv
            return carry

        lax.fori_loop(0, tc, fstep, 0)

        lane_group = lax.broadcasted_iota(jnp.int32, (q, LANE), 1) // nh

        def bstep(s, carry):
            ts = tc - 1 - s
            dy = dyc_ref[ts]
            acc_sa, acc_v = [None] * nacc, [None] * nacc
            for j in range(n):
                gj = g_ref[j] + dy * row(rt, ts, j)
                g_ref[j] = gj
                t1 = gj * row(bt, ts, j)
                t2 = gj * row(kt, ts, j)
                a = j % nacc
                acc_sa[a] = t1 if acc_sa[a] is None else acc_sa[a] + t1
                acc_v[a] = t2 if acc_v[a] is None else acc_v[a] + t2
            dsa = _tree_sum(acc_sa)
            dv_o[ts] = _tree_sum(acc_v)
            for j in range(n):
                g_ref[j] = g_ref[j] * row(wt, ts, j) + dsa * row(knt, ts, j)
            for grp in range(rep):
                m = jnp.where(lane_group == grp, dsa, 0.0)
                shift = nh
                while shift < LANE:
                    m = m + pltpu.roll(m, shift, 1)
                    shift *= 2
                dsat_ref[grp] = m
            rv, wv, knv = rc_ref[ts], wc_ref[ts], knc_ref[ts]
            names = ("dr", "dw", "db", "dk", "dkn")
            accs = {nm: [None] * nacc for nm in names}
            for i in range(n):
                dsai = dsat_ref[i % rep, pl.ds(i // rep, 1), :]
                dyi = row(dyt, ts, i)
                s_prev = hist[ts, i]
                gi = gp_ref[i] + dyi * rv
                terms = {"dr": hist[ts + 1, i] * dyi, "dw": gi * s_prev, "db": gi * row(sat, ts, i),
                         "dk": gi * row(vt, ts, i), "dkn": dsai * s_prev}
                a = i % nacc
                for nm in names:
                    accs[nm][a] = terms[nm] if accs[nm][a] is None else accs[nm][a] + terms[nm]
                gp_ref[i] = gi * wv + dsai * knv
            dr_o[ts] = _tree_sum(accs["dr"])
            dw_o[ts] = _tree_sum(accs["dw"])
            db_o[ts] = _tree_sum(accs["db"])
            dk_o[ts] = _tree_sum(accs["dk"])
            dkn_o[ts] = _tree_sum(accs["dkn"])
            return carry

        lax.fori_loop(0, tc, bstep, 0)

        if ne:
            @pl.when(c == nc - 1)
            def _():
                for cp in copies:
                    cp.wait()

    comp = pl.BlockSpec((tc, q, LANE), lambda c: (nc - 1 - c, 0, 0))
    outc = jax.ShapeDtypeStruct((t, q, LANE), F32)
    res = pl.pallas_call(
        body, name="wkv_bwd", grid=(nc,),
        in_specs=[comp] * 8 + [pl.BlockSpec((1, n, q, LANE), lambda c: (nc - 1 - c, 0, 0, 0))] + [HBM_SPEC] * ne,
        out_specs=[comp] * 6 + [HBM_SPEC] * ne,
        out_shape=[outc] * 6 + _exchange_shapes(exchange),
        scratch_shapes=[pltpu.VMEM((tc + 1, n, q, LANE), F32), pltpu.VMEM((n, q, LANE), F32),
                        pltpu.VMEM((n, q, LANE), F32), pltpu.VMEM((rep, q, LANE), F32)]
        + [pltpu.VMEM((tc, rep, q, LANE), F32)] * 8 + _exchange_sems(ne),
        compiler_params=_cparams(("arbitrary",)),
    )(r_c, w_c, b_c, k_c, kn_c, dy_c, sa_c, v_c, ck_i, *exchange)
    return res


def _conv_stage(gv_ref, gg_ref, gvh_ref, ggh_ref, cw_ref, cb_ref, lg_ref, lb_ref, ext_ref, first, tt, taps):
    u = gv_ref[...] * _sigmoid(gg_ref[...])
    uh = jnp.where(first, 0.0, gvh_ref[...] * _sigmoid(ggh_ref[...]))
    ext_ref[0:CONV_HALO, :] = uh
    ext_ref[CONV_HALO:CONV_HALO + tt, :] = u
    off = CONV_HALO - (taps - 1)
    c = cb_ref[...] + ext_ref[off:off + tt, :] * cw_ref[0:1, :]
    for j in range(1, taps):
        c = c + ext_ref[off + j:off + j + tt, :] * cw_ref[j:j + 1, :]
    mean = jnp.mean(c, axis=-1, keepdims=True)
    cc = c - mean
    rstd = lax.rsqrt(jnp.mean(cc * cc, axis=-1, keepdims=True) + LN_EPS)
    chat = cc * rstd
    cn = chat * lg_ref[...] + lb_ref[...]
    return chat, rstd, cn


def _conv_specs(t, tt, cw, taps, gv_blk, gg_blk):
    hb = tt // CONV_HALO
    return [pl.BlockSpec((tt, cw), lambda i: (i, gv_blk)), pl.BlockSpec((tt, cw), lambda i: (i, gg_blk)),
            pl.BlockSpec((CONV_HALO, cw), lambda i: (jnp.maximum(i * hb - 1, 0), gv_blk)),
            pl.BlockSpec((CONV_HALO, cw), lambda i: (jnp.maximum(i * hb - 1, 0), gg_blk)),
            _full((taps, cw)), _full((1, cw)), _full((1, cw)), _full((1, cw))]


def _conv_fwd(proj, conv_w, conv_b, cln_g, cln_b, cw, gv_blk, gg_blk):
    t = proj.shape[0]
    taps = conv_w.shape[0]
    tt = _tile(t, 128, CONV_HALO)

    def body(gv_ref, gg_ref, gvh_ref, ggh_ref, cw_ref, cb_ref, lg_ref, lb_ref, o_ref, ext_ref):
        _, _, cn = _conv_stage(gv_ref, gg_ref, gvh_ref, ggh_ref, cw_ref, cb_ref, lg_ref, lb_ref, ext_ref,
                               pl.program_id(0) == 0, tt, taps)
        o_ref[...] = (cn * _sigmoid(cn)).astype(o_ref.dtype)

    return pl.pallas_call(
        body, name="conv_fwd", grid=(t // tt,),
        in_specs=_conv_specs(t, tt, cw, taps, gv_blk, gg_blk),
        out_specs=pl.BlockSpec((tt, cw), lambda i: (i, 0)),
        out_shape=jax.ShapeDtypeStruct((t, cw), MXU_DTYPE),
        scratch_shapes=[pltpu.VMEM((CONV_HALO + tt, cw), F32)],
        compiler_params=_cparams(("parallel",)),
    )(proj, proj, proj, proj, conv_w, conv_b, cln_g, cln_b)


def _conv_gate(c2, proj, b_pw2, cw, gc_blk):
    t = c2.shape[0]
    tt = _tile(t, 256, SUBLANE)

    def body(c_ref, g_ref, b_ref, o_ref):
        g = g_ref[...]
        o_ref[...] = ((c_ref[...] + b_ref[...]) * (g * _sigmoid(g))).astype(o_ref.dtype)

    return pl.pallas_call(
        body, name="conv_gate", grid=(t // tt,),
        in_specs=[pl.BlockSpec((tt, cw), lambda i: (i, 0)), pl.BlockSpec((tt, cw), lambda i: (i, gc_blk)),
                  _full((1, cw))],
        out_specs=pl.BlockSpec((tt, cw), lambda i: (i, 0)),
        out_shape=jax.ShapeDtypeStruct((t, cw), MXU_DTYPE),
        compiler_params=_cparams(("parallel",)),
    )(c2, proj, b_pw2)


def _conv_gate_bwd(dmix, c2, proj, b_pw2, cw, dm_blk, gc_blk):
    t = c2.shape[0]
    tt = _tile(t, 256, SUBLANE)

    def body(dm_ref, c_ref, g_ref, b_ref, dc2_o, dg_o, db_o):
        g = g_ref[...]
        sg = _sigmoid(g)
        dyc = dm_ref[...]
        dc2 = dyc * (g * sg)
        dc2_o[...] = dc2.astype(dc2_o.dtype)
        dg_o[...] = (dyc * (c_ref[...] + b_ref[...]) * (sg * (1.0 + g * (1.0 - sg)))).astype(dg_o.dtype)

        @pl.when(pl.program_id(0) == 0)
        def _():
            db_o[...] = jnp.zeros_like(db_o)

        db_o[...] += jnp.sum(dc2, axis=0, keepdims=True)

    row = pl.BlockSpec((tt, cw), lambda i: (i, 0))
    return pl.pallas_call(
        body, name="conv_gate_bwd", grid=(t // tt,),
        in_specs=[pl.BlockSpec((tt, cw), lambda i: (i, dm_blk)), row,
                  pl.BlockSpec((tt, cw), lambda i: (i, gc_blk)), _full((1, cw))],
        out_specs=[row, row, _full((1, cw))],
        out_shape=[jax.ShapeDtypeStruct((t, cw), MXU_DTYPE), jax.ShapeDtypeStruct((t, cw), MXU_DTYPE),
                   jax.ShapeDtypeStruct((1, cw), F32)],
        compiler_params=_cparams(("arbitrary",)),
    )(dmix, c2, proj, b_pw2)


def _conv_bwd_norm(proj, dcs, conv_w, conv_b, cln_g, cln_b, cw, gv_blk, gg_blk):
    t = proj.shape[0]
    taps = conv_w.shape[0]
    tt = _tile(t, 128, CONV_HALO)

    def body(gv_ref, gg_ref, gvh_ref, ggh_ref, cw_ref, cb_ref, lg_ref, lb_ref, dcs_ref,
             dc_o, dcw_o, dcb_o, dlg_o, dlb_o, ext_ref):
        chat, rstd, cn = _conv_stage(gv_ref, gg_ref, gvh_ref, ggh_ref, cw_ref, cb_ref, lg_ref, lb_ref, ext_ref,
                                     pl.program_id(0) == 0, tt, taps)
        s = _sigmoid(cn)
        dcn = dcs_ref[...] * (s * (1.0 + cn * (1.0 - s)))
        dchat = dcn * lg_ref[...]
        dc = rstd * (dchat - jnp.mean(dchat, axis=-1, keepdims=True)
                     - chat * jnp.mean(dchat * chat, axis=-1, keepdims=True))
        dc_o[...] = dc

        @pl.when(pl.program_id(0) == 0)
        def _():
            for ref in (dcw_o, dcb_o, dlg_o, dlb_o):
                ref[...] = jnp.zeros_like(ref)

        dlg_o[...] += jnp.sum(dcn * chat, axis=0, keepdims=True)
        dlb_o[...] += jnp.sum(dcn, axis=0, keepdims=True)
        dcb_o[...] += jnp.sum(dc, axis=0, keepdims=True)
        off = CONV_HALO - (taps - 1)
        for j in range(taps):
            dcw_o[j:j + 1, :] += jnp.sum(ext_ref[off + j:off + j + tt, :] * dc, axis=0, keepdims=True)

    vec = _full((1, cw))
    vecf = jax.ShapeDtypeStruct((1, cw), F32)
    return pl.pallas_call(
        body, name="conv_bwd_norm", grid=(t // tt,),
        in_specs=_conv_specs(t, tt, cw, taps, gv_blk, gg_blk) + [pl.BlockSpec((tt, cw), lambda i: (i, 0))],
        out_specs=[pl.BlockSpec((tt, cw), lambda i: (i, 0)), _full((taps, cw)), vec, vec, vec],
        out_shape=[jax.ShapeDtypeStruct((t, cw), F32), jax.ShapeDtypeStruct((taps, cw), F32), vecf, vecf, vecf],
        scratch_shapes=[pltpu.VMEM((CONV_HALO + tt, cw), F32)],
        compiler_params=_cparams(("arbitrary",)),
    )(proj, proj, proj, proj, conv_w, conv_b, cln_g, cln_b, dcs)


def _conv_bwd_glu(dc, proj, conv_w, cw, gv_blk, gg_blk):
    t = dc.shape[0]
    taps = conv_w.shape[0]
    tt = _tile(t, 128, CONV_HALO)
    hb = tt // CONV_HALO
    nhalo = t // CONV_HALO

    def body(dc_ref, dch_ref, gv_ref, gg_ref, cw_ref, dgv_o, dgg_o, ext_ref):
        last = pl.program_id(0) == pl.num_programs(0) - 1
        ext_ref[0:tt, :] = dc_ref[...]
        ext_ref[tt:tt + CONV_HALO, :] = jnp.where(last, 0.0, dch_ref[...])
        du = ext_ref[taps - 1:taps - 1 + tt, :] * cw_ref[0:1, :]
        for j in range(1, taps):
            du = du + ext_ref[taps - 1 - j:taps - 1 - j + tt, :] * cw_ref[j:j + 1, :]
        sg = _sigmoid(gg_ref[...])
        dgv_o[...] = (du * sg).astype(dgv_o.dtype)
        dgg_o[...] = (du * gv_ref[...] * sg * (1.0 - sg)).astype(dgg_o.dtype)

    row = pl.BlockSpec((tt, cw), lambda i: (i, 0))
    return pl.pallas_call(
        body, name="conv_bwd_glu", grid=(t // tt,),
        in_specs=[row, pl.BlockSpec((CONV_HALO, cw), lambda i: (jnp.minimum((i + 1) * hb, nhalo - 1), 0)),
                  pl.BlockSpec((tt, cw), lambda i: (i, gv_blk)), pl.BlockSpec((tt, cw), lambda i: (i, gg_blk)),
                  _full((taps, cw))],
        out_specs=[row, row],
        out_shape=[jax.ShapeDtypeStruct((t, cw), MXU_DTYPE)] * 2,
        scratch_shapes=[pltpu.VMEM((tt + CONV_HALO, cw), F32)],
        compiler_params=_cparams(("parallel",)),
    )(dc, dc, proj, proj, conv_w)


HBM_SPEC = pl.BlockSpec(memory_space=pltpu.HBM)


def _all_gather(shards, name):
    na = len(shards)

    def body(*refs):
        ins, outs = refs[:na], refs[na:2 * na]
        send_sems, recv_sems, local_sems = refs[2 * na:]
        x, y, c = lax.axis_index("x"), lax.axis_index("y"), lax.axis_index("c")
        me, sibling = (x, y, c), (x, y, 1 - c)
        chips = [(1 - x, y), (x, 1 - y), (1 - x, 1 - y)]

        def slot(px, py, pc):
            return 4 * px + 2 * py + pc

        def copy(a, k, block, to, src=None):
            dst = outs[a].at[slot(*block)]
            return pltpu.make_async_remote_copy(
                src_ref=dst if src is None else src, dst_ref=dst,
                send_sem=send_sems.at[a, k], recv_sem=recv_sems.at[a, k],
                device_id=to, device_id_type=MESH)

        mine = [pltpu.make_async_copy(ins[a], outs[a].at[slot(*me)], local_sems.at[a]) for a in range(na)]
        for cp in mine:
            cp.start()
        first = []
        for a in range(na):
            first.append(copy(a, 0, me, sibling, src=ins[a]))
            first += [copy(a, 1 + j, me, (*chip, c), src=ins[a]) for j, chip in enumerate(chips)]
        for cp in first:
            cp.start()
        passed = []
        for j, chip in enumerate(chips):
            for a in range(na):
                copy(a, 1 + j, (*chip, c), me).wait_recv()
                fwd = copy(a, 4 + j, (*chip, c), sibling)
                fwd.start()
                passed.append(fwd)
        for a in range(na):
            copy(a, 0, sibling, me).wait_recv()
            for j, chip in enumerate(chips):
                copy(a, 4 + j, (*chip, 1 - c), me).wait_recv()
        for cp in first + passed:
            cp.wait_send()
        for cp in mine:
            cp.wait()

    return pl.pallas_call(
        body, name=name,
        in_specs=[HBM_SPEC] * na, out_specs=[HBM_SPEC] * na,
        out_shape=[jax.ShapeDtypeStruct((N_DEV,) + s.shape, s.dtype) for s in shards],
        scratch_shapes=[pltpu.SemaphoreType.DMA((na, 7)), pltpu.SemaphoreType.DMA((na, 7)),
                        pltpu.SemaphoreType.DMA((na,))],
        compiler_params=pltpu.CompilerParams(has_side_effects=True),
    )(*shards)


def _exchange_copies(ins, outs, send_sems, recv_sems):
    x, y, c = lax.axis_index("x"), lax.axis_index("y"), lax.axis_index("c")
    copies = []
    for k in range(1, N_DEV):
        px = 1 - x if k & 4 else x
        py = 1 - y if k & 2 else y
        pc = 1 - c if k & 1 else c
        for a in range(len(ins)):
            copies.append(pltpu.make_async_remote_copy(
                src_ref=ins[a].at[4 * px + 2 * py + pc], dst_ref=outs[a].at[k - 1],
                send_sem=send_sems.at[a, k - 1], recv_sem=recv_sems.at[a, k - 1],
                device_id=(px, py, pc), device_id_type=MESH))
    return copies


def _exchange_shapes(parts):
    return [jax.ShapeDtypeStruct((N_DEV - 1,) + p.shape[1:], p.dtype) for p in parts]


def _exchange_sems(na):
    if not na:
        return []
    return [pltpu.SemaphoreType.DMA((na, N_DEV - 1)), pltpu.SemaphoreType.DMA((na, N_DEV - 1))]


def _cast(v, dtype, name):
    r, c = v.shape
    tr = _tile(r, 256, SUBLANE)

    def body(i_ref, o_ref):
        o_ref[...] = i_ref[...].astype(o_ref.dtype)

    return pl.pallas_call(
        body, name=name, grid=(r // tr,),
        in_specs=[pl.BlockSpec((tr, c), lambda i: (i, 0))],
        out_specs=pl.BlockSpec((tr, c), lambda i: (i, 0)),
        out_shape=jax.ShapeDtypeStruct((r, c), dtype),
        compiler_params=_cparams(("parallel",)),
    )(v)


def _adamw(w, m, v, recv, own, name):
    r, c = w.shape
    ns = recv.shape[0]
    tr = _tile(r, 128, SUBLANE)
    c1 = 1.0 - ADAM_B1 ** ADAM_STEP
    c2 = 1.0 - ADAM_B2 ** ADAM_STEP

    def body(*refs):
        if own is None:
            w_ref, m_ref, v_ref, rc_ref = refs[:4]
            g = rc_ref[0].astype(F32)
            start = 1
        else:
            w_ref, m_ref, v_ref, rc_ref, own_ref = refs[:5]
            g = own_ref[...]
            start = 0
        g_o, d_o, m_o, v_o = refs[-4:]
        for s in range(start, ns):
            g = g + rc_ref[s].astype(F32)
        mn = ADAM_B1 * m_ref[...] + (1.0 - ADAM_B1) * g
        vn = ADAM_B2 * v_ref[...] + (1.0 - ADAM_B2) * (g * g)
        m_hat = mn / c1
        v_hat = vn / c2
        g_o[...] = g
        d_o[...] = -ADAM_LR * (m_hat / (jnp.sqrt(v_hat) + ADAM_EPS) + ADAM_WD * w_ref[...])
        m_o[...] = mn
        v_o[...] = vn

    row = pl.BlockSpec((tr, c), lambda i: (i, 0))
    ins = [w, m, v, recv] + ([] if own is None else [own])
    in_specs = [row, row, row, pl.BlockSpec((ns, tr, c), lambda i: (0, i, 0))] + ([] if own is None else [row])
    return pl.pallas_call(
        body, name=name, grid=(r // tr,),
        in_specs=in_specs, out_specs=[row] * 4,
        out_shape=[jax.ShapeDtypeStruct((r, c), F32)] * 4,
        compiler_params=_cparams(("parallel",)),
    )(*ins)


def _to_t(v, nh, n, axis=-1):
    v = jnp.moveaxis(v, axis, -1)
    v = v.reshape(v.shape[:-1] + (nh, n)).swapaxes(-1, -2).reshape(v.shape)
    return jnp.moveaxis(v, -1, axis)


def _from_t(v, nh, n, axis=-1):
    v = jnp.moveaxis(v, axis, -1)
    v = v.reshape(v.shape[:-1] + (n, nh)).swapaxes(-1, -2).reshape(v.shape)
    return jnp.moveaxis(v, -1, axis)


def _pad_to(v, size, axis):
    pad = [(0, 0)] * v.ndim
    pad[axis] = (0, size - v.shape[axis])
    return jnp.pad(v, pad)


def kernel(x, norm_pre_g, w_in, mu_shift, w0, w_lora_up, a0, a_lora_up, k_k, k_a, r_k, lnx_g, lnx_b, conv_w, conv_b, cln_g, cln_b, w_pw2, b_pw2, w_out, norm_post_g, loss_target, m_norm_pre_g, m_w_in, m_mu_shift, m_w0, m_w_lora_up, m_a0, m_a_lora_up, m_k_k, m_k_a, m_r_k, m_lnx_g, m_lnx_b, m_conv_w, m_conv_b, m_cln_g, m_cln_b, m_w_pw2, m_b_pw2, m_w_out, m_norm_post_g, v_norm_pre_g, v_w_in, v_mu_shift, v_w0, v_w_lora_up, v_a0, v_a_lora_up, v_k_k, v_k_a, v_r_k, v_lnx_g, v_lnx_b, v_conv_w, v_conv_b, v_cln_g, v_cln_b, v_w_pw2, v_b_pw2, v_w_out, v_norm_post_g):
    args = dict(locals())
    t, d = x.shape[1], x.shape[2]
    hw = w0.shape[0]
    cw = conv_b.shape[0]
    nh, n = r_k.shape
    lw, la = w_lora_up.shape[0], a_lora_up.shape[0]
    taps = conv_w.shape[0]
    in_cols = w_in.shape[1] * N_DEV
    shift_cols = 3 * hw + lw + la
    assert in_cols == shift_cols + hw + 3 * cw and hw == cw and hw % LANE == 0 and LANE % nh == 0
    assert lw <= LORA_PAD and la <= LORA_PAD and taps - 1 <= CONV_HALO and hw % (2 * LORA_PAD) == 0
    q = hw // LANE
    gate_blk, gv_blk, gg_blk, gc_blk = 3, 4, 5, 6
    lo0 = 7 * hw
    lo_blk = lo0 // (2 * LORA_PAD)
    x2, tgt2 = x[0], loss_target[0]
    row = lambda v: v.reshape(1, -1)

    gathered = _all_gather(
        [_cast(w_in, MXU_DTYPE, "cast_w_in"), _cast(w_out, MXU_DTYPE, "cast_w_out"),
         _cast(w_pw2, MXU_DTYPE, "cast_w_pw2"), w_lora_up, a_lora_up, conv_w], "gather_weights")
    w_in_g, w_out_g, w_pw2_g, wup_w_g, wup_a_g, conv_w_g = gathered
    w_full = w_in_g.transpose(1, 0, 2).reshape(d, in_cols)
    c0 = shift_cols
    wp = jnp.concatenate([
        _to_t(w_full[:, 0:hw], nh, n), _to_t(w_full[:, hw:2 * hw], nh, n), _to_t(w_full[:, 2 * hw:3 * hw], nh, n),
        _to_t(w_full[:, c0:c0 + hw], nh, n), w_full[:, c0 + hw:],
        _pad_to(w_full[:, 3 * hw:3 * hw + lw], LORA_PAD, 1), _pad_to(w_full[:, 3 * hw + lw:c0], LORA_PAD, 1)], axis=1)
    w_out_f = w_out_g.reshape(N_DEV * w_out.shape[0], d)
    w_out_p = jnp.concatenate([_to_t(w_out_f[:hw], nh, n, axis=0), w_out_f[hw:]], axis=0)
    w_pw2_f = w_pw2_g.reshape(N_DEV * w_pw2.shape[0], cw)
    wup_w = _pad_to(_to_t(wup_w_g.transpose(1, 0, 2).reshape(lw, hw), nh, n), LORA_PAD, 0)
    wup_a = _pad_to(_to_t(wup_a_g.transpose(1, 0, 2).reshape(la, hw), nh, n), LORA_PAD, 0)
    conv_w_f = conv_w_g.transpose(1, 0, 2).reshape(taps, cw)
    mu_p = row(jnp.concatenate([
        _to_t(mu_shift[0:hw], nh, n), _to_t(mu_shift[hw:2 * hw], nh, n), _to_t(mu_shift[2 * hw:3 * hw], nh, n)]))
    mu_lo = row(jnp.concatenate([
        _pad_to(mu_shift[3 * hw:3 * hw + lw], LORA_PAD, 0), _pad_to(mu_shift[3 * hw + lw:], LORA_PAD, 0)]))
    tvec = lambda v: row(_to_t(v, nh, n))
    w0_t, a0_t, kk_t, ka_t, lg_t, lb_t = tvec(w0), tvec(a0), tvec(k_k), tvec(k_a), tvec(lnx_g), tvec(lnx_b)
    rk_t = row(r_k.T)
    head = jnp.arange(hw, dtype=jnp.int32) % nh
    e = (head[:, None] == jnp.arange(LANE, dtype=jnp.int32)[None, :]).astype(F32)
    et = e.T

    c3 = lambda v: v.reshape(t, q, LANE)
    c2d = lambda v: v.reshape(t * q, LANE)
    me = 4 * lax.axis_index("x") + 2 * lax.axis_index("y") + lax.axis_index("c")
    own = lambda parts: lax.dynamic_index_in_dim(parts, me, 0, keepdims=False)
    wire = lambda parts, nm: _cast(parts.reshape(-1, parts.shape[-1]), WIRE_DTYPE, nm).reshape(parts.shape)

    h = _prenorm(x2, row(norm_pre_g))
    proj = _matmul(h, wp, "nn", F32, "mm_proj")
    r_a, w_a, kh_a, v_a, kn_a, b_a = _rwkv_pre(
        proj, mu_p, mu_lo, w0_t, a0_t, kk_t, ka_t, wup_w, wup_a, e, et, hw, lo_blk)
    y_c, sa_c, ck = _wkv_fwd(c3(w_a), c3(b_a), c3(kh_a), c3(r_a), c3(kn_a), c3(v_a), nh)
    y_a = c2d(y_c)
    y_rwkv = _rwkv_post(y_a, r_a, kh_a, v_a, proj, lg_t, lb_t, rk_t, e, et, hw, n, gate_blk)
    cs = _conv_fwd(proj, conv_w_f, row(conv_b), row(cln_g), row(cln_b), cw, gv_blk, gg_blk)
    c2 = _matmul(cs, w_pw2_f, "nn", F32, "mm_pw2")
    y_conv = _conv_gate(c2, proj, row(b_pw2), cw, gc_blk)
    mix = jnp.concatenate([y_rwkv, y_conv], axis=1)
    out = _matmul(mix, w_out_p, "nn", F32, "mm_out")
    dout, dy, loss_part, d_post_g = _post_loss(out, x2, tgt2, row(norm_post_g))

    dmix = _matmul(dout, w_out_p, "nt", F32, "mm_dmix")
    d_w_out_p = _matmul(mix, dout, "tn", F32, "mm_dw_out")
    dc2, dgc, d_b_pw2 = _conv_gate_bwd(dmix, c2, proj, row(b_pw2), cw, 1, gc_blk)
    dcs = _matmul(dc2, w_pw2_f, "nt", F32, "mm_dcs")
    d_w_pw2 = _matmul(cs, dc2, "tn", F32, "mm_dw_pw2")
    dc, d_conv_w, d_conv_b, d_cln_g, d_cln_b = _conv_bwd_norm(
        proj, dcs, conv_w_f, row(conv_b), row(cln_g), row(cln_b), cw, gv_blk, gg_blk)
    dgv, dgg = _conv_bwd_glu(dc, proj, conv_w_f, cw, gv_blk, gg_blk)
    dgr, dy_rec, dr_bon, dkh_bon, dv_bon, d_lg_t, d_lb_t, d_rk_t = _rwkv_post_bwd(
        dmix, y_a, r_a, kh_a, v_a, proj, lg_t, lb_t, rk_t, e, et, hw, n, gate_blk)
    ck_i = ck.reshape(ck.shape[0], n, n, nh).transpose(0, 2, 1, 3).reshape(ck.shape)
    d_w_out_f = jnp.concatenate([_from_t(d_w_out_p[:hw], nh, n, axis=0), d_w_out_p[hw:]], axis=0)
    d_w_out_parts = d_w_out_f.reshape((N_DEV,) + w_out.shape)
    d_w_pw2_parts = d_w_pw2.reshape((N_DEV,) + w_pw2.shape)
    dv_c, dr_c, dw_c, db_c, dk_c, dkn_c, recv_w_out, recv_w_pw2 = _wkv_bwd(
        c3(r_a), c3(w_a), c3(b_a), c3(kh_a), c3(kn_a), c3(dy_rec), sa_c, c3(v_a), ck_i, nh,
        exchange=[wire(d_w_out_parts, "wire_w_out"), wire(d_w_pw2_parts, "wire_w_pw2")])
    dxs, dxs_lo, d_mu_p, d_mu_lo, d_w0_t, d_a0_t, d_kk_t, d_ka_t, d_wup_w, d_wup_a = _rwkv_pre_bwd(
        proj, mu_p, mu_lo, w0_t, a0_t, kk_t, ka_t, wup_w, wup_a, e, et, hw, lo_blk,
        c2d(dr_c), c2d(dw_c), c2d(dk_c), c2d(dv_c), c2d(dkn_c), c2d(db_c), dr_bon, dkh_bon, dv_bon)
    dshift = _shift_bwd(dxs, mu_p, "shift_bwd")
    dshift_lo = _shift_bwd(dxs_lo, mu_lo, "shift_bwd_lora")
    dproj = jnp.concatenate([dshift, dgr, dgv, dgg, dgc, dshift_lo], axis=1)
    d_wp = _matmul(h, dproj, "tn", F32, "mm_dw_in")

    d_w_full = jnp.concatenate([
        _from_t(d_wp[:, 0:hw], nh, n), _from_t(d_wp[:, hw:2 * hw], nh, n), _from_t(d_wp[:, 2 * hw:3 * hw], nh, n),
        d_wp[:, lo0:lo0 + lw], d_wp[:, lo0 + LORA_PAD:lo0 + LORA_PAD + la],
        _from_t(d_wp[:, 3 * hw:4 * hw], nh, n), d_wp[:, 4 * hw:lo0]], axis=1)
    sc = w_in.shape[1]
    d_w_in_parts = d_w_full.reshape(d, N_DEV, sc).transpose(1, 0, 2)
    colparts = lambda v: v.reshape(v.shape[0], N_DEV, v.shape[1] // N_DEV).transpose(1, 0, 2)
    d_wup_w_parts = colparts(_from_t(d_wup_w[:lw], nh, n))
    d_wup_a_parts = colparts(_from_t(d_wup_a[:la], nh, n))
    d_conv_w_parts = colparts(d_conv_w)
    dh, recv_w_in, recv_wup_w, recv_wup_a, recv_conv_w = _matmul(
        dproj, wp, "nt", F32, "mm_dh",
        exchange=[wire(d_w_in_parts, "wire_w_in"), d_wup_w_parts, d_wup_a_parts, d_conv_w_parts])
    grad_x, d_pre_g = _prenorm_bwd(dh, x2, dy, row(norm_pre_g))
    d_mu = jnp.concatenate([
        _from_t(d_mu_p[0, 0:hw], nh, n), _from_t(d_mu_p[0, hw:2 * hw], nh, n), _from_t(d_mu_p[0, 2 * hw:3 * hw], nh, n),
        d_mu_lo[0, 0:lw], d_mu_lo[0, LORA_PAD:LORA_PAD + la]])
    ft = lambda v: _from_t(v[0], nh, n)
    small = {
        "norm_pre_g": d_pre_g[0], "mu_shift": d_mu, "w0": ft(d_w0_t), "a0": ft(d_a0_t), "k_k": ft(d_kk_t),
        "k_a": ft(d_ka_t), "r_k": d_rk_t[0].reshape(n, nh).T.reshape(-1), "lnx_g": ft(d_lg_t), "lnx_b": ft(d_lb_t),
        "conv_b": d_conv_b[0], "cln_g": d_cln_g[0], "cln_b": d_cln_b[0], "b_pw2": d_b_pw2[0],
        "norm_post_g": d_post_g[0]}
    small_names = list(small)
    packed = jnp.concatenate([small[k] for k in small_names] + [loss_part[0, 0:1]])
    plen = packed.shape[0]
    ppad = -(-plen // LANE) * LANE
    packed = _pad_to(packed, ppad, 0).reshape(1, ppad)

    (packed_all,) = _all_gather([packed], "gather_small")

    res = {}
    sharded = [("w_in", d_w_in_parts, recv_w_in), ("w_out", d_w_out_parts, recv_w_out),
               ("w_pw2", d_w_pw2_parts, recv_w_pw2), ("w_lora_up", d_wup_w_parts, recv_wup_w),
               ("a_lora_up", d_wup_a_parts, recv_wup_a), ("conv_w", d_conv_w_parts, recv_conv_w)]
    for nm, parts, rc in sharded:
        res[nm] = _adamw(args[nm], args["m_" + nm], args["v_" + nm], rc, own(parts), "adamw_" + nm)
    w_small = _pad_to(jnp.concatenate([args[k].reshape(-1) for k in small_names]), ppad, 0).reshape(1, ppad)
    m_small = _pad_to(jnp.concatenate([args["m_" + k].reshape(-1) for k in small_names]), ppad, 0).reshape(1, ppad)
    v_small = _pad_to(jnp.concatenate([args["v_" + k].reshape(-1) for k in small_names]), ppad, 0).reshape(1, ppad)
    g_s, d_s, m_s, v_s = _adamw(w_small, m_small, v_small, packed_all, None, "adamw_small")
    off = 0
    for k in small_names:
        size = args[k].size
        res[k] = tuple(o[0, off:off + size].reshape(args[k].shape) for o in (g_s, d_s, m_s, v_s))
        off += size
    loss = g_s[0, plen - 1]

    order = ["norm_pre_g", "w_in", "mu_shift", "w0", "w_lora_up", "a0", "a_lora_up", "k_k", "k_a", "r_k",
             "lnx_g", "lnx_b", "conv_w", "conv_b", "cln_g", "cln_b", "w_pw2", "b_pw2", "w_out", "norm_post_g"]
    outs = [loss, grad_x[None]]
    for slot in range(4):
        outs += [res[k][slot] for k in order]
    return tuple(outs)
```

```python
import functools

import jax
import jax.numpy as jnp
from jax import lax
from jax.experimental import pallas as pl
from jax.experimental.pallas import tpu as pltpu

F32 = jnp.float32
MXU_DTYPE = jnp.bfloat16
WIRE_DTYPE = jnp.bfloat16
HI = lax.Precision.HIGHEST
SEG_PRECISION = lax.Precision.HIGH

NORM_EPS = 1e-6
LN_EPS = 1e-5
GN_EPS_PER_CHANNEL = 1e-5
KK_EPS = 1e-12
ADAM_LR = 0.001
ADAM_B1 = 0.9
ADAM_B2 = 0.999
ADAM_EPS = 1e-08
ADAM_WD = 0.01
ADAM_STEP = 10

LANE = 128
SUBLANE = 8
LORA_PAD = 128
CONV_HALO = 32
N_DEV = 8
VMEM_LIMIT = 56 * 1024 * 1024
WKV_CHUNK = 16
RWKV_ROWS = 64
MESH = pl.DeviceIdType.MESH


def _tile(n, target, mult):
    if n <= target:
        return n
    best = None
    for d in range(mult, target + 1, mult):
        if n % d == 0:
            best = d
    assert best is not None, (n, target, mult)
    return best


def _cparams(sem=None):
    return pltpu.CompilerParams(dimension_semantics=sem, vmem_limit_bytes=VMEM_LIMIT)


def _sigmoid(x):
    return 1.0 / (1.0 + jnp.exp(-x))


def _full(shape):
    nd = len(shape)
    return pl.BlockSpec(shape, lambda *_: (0,) * nd)


def _compact_spec(tt, q):
    return pl.BlockSpec((tt * q, LANE), lambda i: (i, 0))


def _load_compact(ref, tt, q):
    return jnp.concatenate([ref[pl.ds(p, tt, stride=q), :] for p in range(q)], axis=1)


def _store_compact(ref, val, tt, q):
    for p in range(q):
        ref[pl.ds(p, tt, stride=q), :] = val[:, p * LANE:(p + 1) * LANE]


def _matmul(a, b, mode, out_dtype, name, exchange=()):
    if mode == "nn":
        (m, k), (k2, n) = a.shape, b.shape
    elif mode == "nt":
        (m, k), (n, k2) = a.shape, b.shape
    else:
        (k, m), (k2, n) = a.shape, b.shape
    assert k == k2, (a.shape, b.shape, mode)
    tm, tn, tk = _tile(m, 1024, LANE), _tile(n, 768, LANE), _tile(k, 2048, LANE)
    nk = k // tk
    ne = len(exchange)
    grid = (m // tm, n // tn, nk)
    if mode == "nn":
        a_spec = pl.BlockSpec((tm, tk), lambda i, j, kk: (i, kk))
        b_spec = pl.BlockSpec((tk, tn), lambda i, j, kk: (kk, j))
        dims = (((1,), (0,)), ((), ()))
    elif mode == "nt":
        a_spec = pl.BlockSpec((tm, tk), lambda i, j, kk: (i, kk))
        b_spec = pl.BlockSpec((tn, tk), lambda i, j, kk: (j, kk))
        dims = (((1,), (1,)), ((), ()))
    else:
        a_spec = pl.BlockSpec((tk, tm), lambda i, j, kk: (kk, i))
        b_spec = pl.BlockSpec((tk, tn), lambda i, j, kk: (kk, j))
        dims = (((0,), (0,)), ((), ()))

    def body(*refs):
        a_ref, b_ref = refs[:2]
        parts = refs[2:2 + ne]
        o_ref = refs[2 + ne]
        recvs = refs[3 + ne:3 + 2 * ne]
        acc_ref = refs[3 + 2 * ne]
        i, j, kk = pl.program_id(0), pl.program_id(1), pl.program_id(2)
        if ne:
            copies = _exchange_copies(parts, recvs, *refs[4 + 2 * ne:])

            @pl.when((i == 0) & (j == 0) & (kk == 0))
            def _():
                for cp in copies:
                    cp.start()

        @pl.when(kk == 0)
        def _():
            acc_ref[...] = jnp.zeros_like(acc_ref)

        acc_ref[...] += lax.dot_general(a_ref[...], b_ref[...], dims, preferred_element_type=F32)

        @pl.when(kk == nk - 1)
        def _():
            o_ref[...] = acc_ref[...].astype(o_ref.dtype)

        if ne:
            @pl.when((i == grid[0] - 1) & (j == grid[1] - 1) & (kk == nk - 1))
            def _():
                for cp in copies:
                    cp.wait()

    res = pl.pallas_call(
        body, name=name,
        grid=grid,
        in_specs=[a_spec, b_spec] + [HBM_SPEC] * ne,
        out_specs=[pl.BlockSpec((tm, tn), lambda i, j, kk: (i, j))] + [HBM_SPEC] * ne,
        out_shape=[jax.ShapeDtypeStruct((m, n), out_dtype)] + _exchange_shapes(exchange),
        scratch_shapes=[pltpu.VMEM((tm, tn), F32)] + _exchange_sems(ne),
        compiler_params=_cparams(("arbitrary",) * 3 if ne else ("parallel", "parallel", "arbitrary")),
    )(a, b, *exchange)
    return res if ne else res[0]


def _prenorm(x, g):
    t, d = x.shape
    tt = _tile(t, 256, SUBLANE)

    def body(x_ref, g_ref, h_ref):
        xv = x_ref[...]
        rinv = lax.rsqrt(jnp.mean(xv * xv, axis=-1, keepdims=True) + NORM_EPS)
        h_ref[...] = (xv * rinv * g_ref[...]).astype(h_ref.dtype)

    return pl.pallas_call(
        body, name="prenorm", grid=(t // tt,),
        in_specs=[pl.BlockSpec((tt, d), lambda i: (i, 0)), _full((1, d))],
        out_specs=pl.BlockSpec((tt, d), lambda i: (i, 0)),
        out_shape=jax.ShapeDtypeStruct((t, d), MXU_DTYPE),
        compiler_params=_cparams(("parallel",)),
    )(x, g)


def _post_loss(out, x, target, g):
    t, d = out.shape
    tt = _tile(t, 128, SUBLANE)

    def body(o_ref, x_ref, t_ref, g_ref, dout_ref, dy_ref, loss_ref, dg_ref):
        i = pl.program_id(0)
        ov = o_ref[...]
        rinv = lax.rsqrt(jnp.mean(ov * ov, axis=-1, keepdims=True) + NORM_EPS)
        nv = ov * rinv
        gv = g_ref[...]
        err = x_ref[...] + nv * gv - t_ref[...]
        part = 0.5 * jnp.sum(jnp.mean(err * err, axis=-1, keepdims=True), axis=0, keepdims=True)
        dy = err * (1.0 / d)
        dy_ref[...] = dy
        dn = dy * gv
        dout = rinv * (dn - nv * jnp.mean(dn * nv, axis=-1, keepdims=True))
        dout_ref[...] = dout.astype(dout_ref.dtype)
        dg = jnp.sum(dy * nv, axis=0, keepdims=True)

        @pl.when(i == 0)
        def _():
            loss_ref[...] = jnp.zeros_like(loss_ref)
            dg_ref[...] = jnp.zeros_like(dg_ref)

        loss_ref[...] += jnp.broadcast_to(part, loss_ref.shape)
        dg_ref[...] += dg

    row = pl.BlockSpec((tt, d), lambda i: (i, 0))
    return pl.pallas_call(
        body, name="post_loss", grid=(t // tt,),
        in_specs=[row, row, row, _full((1, d))],
        out_specs=[row, row, _full((1, LANE)), _full((1, d))],
        out_shape=[jax.ShapeDtypeStruct((t, d), MXU_DTYPE), jax.ShapeDtypeStruct((t, d), F32),
                   jax.ShapeDtypeStruct((1, LANE), F32), jax.ShapeDtypeStruct((1, d), F32)],
        compiler_params=_cparams(("arbitrary",)),
    )(out, x, target, g)


def _prenorm_bwd(dh, x, dy, g):
    t, d = x.shape
    tt = _tile(t, 128, SUBLANE)

    def body(dh_ref, x_ref, dy_ref, g_ref, gx_ref, dg_ref):
        i = pl.program_id(0)
        xv = x_ref[...]
        rinv = lax.rsqrt(jnp.mean(xv * xv, axis=-1, keepdims=True) + NORM_EPS)
        nx = xv * rinv
        dhv = dh_ref[...]
        dnx = dhv * g_ref[...]
        dx = rinv * (dnx - nx * jnp.mean(dnx * nx, axis=-1, keepdims=True))
        gx_ref[...] = dy_ref[...] + dx

        @pl.when(i == 0)
        def _():
            dg_ref[...] = jnp.zeros_like(dg_ref)

        dg_ref[...] += jnp.sum(dhv * nx, axis=0, keepdims=True)

    row = pl.BlockSpec((tt, d), lambda i: (i, 0))
    return pl.pallas_call(
        body, name="prenorm_bwd", grid=(t // tt,),
        in_specs=[row, row, row, _full((1, d))],
        out_specs=[row, _full((1, d))],
        out_shape=[jax.ShapeDtypeStruct((t, d), F32), jax.ShapeDtypeStruct((1, d), F32)],
        compiler_params=_cparams(("arbitrary",)),
    )(dh, x, dy, g)


def _segsum(v, e_ref, et_ref):
    s = jnp.dot(v, e_ref[...], preferred_element_type=F32, precision=SEG_PRECISION)
    return jnp.dot(s, et_ref[...], preferred_element_type=F32, precision=SEG_PRECISION)


def _shifted(cur_ref, prev_ref, lo, hi, first):
    cur = cur_ref[:, lo:hi]
    last = jnp.where(first, 0.0, prev_ref[SUBLANE - 1:SUBLANE, lo:hi])
    prev = pltpu.roll(cur, 1, 0)
    rows = lax.broadcasted_iota(jnp.int32, cur.shape, 0)
    return cur, jnp.where(rows == 0, last, prev)


def _rwkv_mix(main_ref, mainp_ref, lo_ref, lop_ref, mu_ref, mulo_ref, w0_ref, a0_ref, kk_ref, ka_ref,
              wupw_ref, wupa_ref, e_ref, et_ref, hw, first):
    def xs(cur_ref, prev_ref, m_ref, lo, hi):
        cur, prev = _shifted(cur_ref, prev_ref, lo, hi, first)
        return cur + (prev - cur) * m_ref[:, lo:hi], prev - cur

    out = {}
    out["r"], out["r_d"] = xs(main_ref, mainp_ref, mu_ref, 0, hw)
    out["k"], out["k_d"] = xs(main_ref, mainp_ref, mu_ref, hw, 2 * hw)
    out["v"], out["v_d"] = xs(main_ref, mainp_ref, mu_ref, 2 * hw, 3 * hw)
    out["wl"], out["wl_d"] = xs(lo_ref, lop_ref, mulo_ref, 0, LORA_PAD)
    out["al"], out["al_d"] = xs(lo_ref, lop_ref, mulo_ref, LORA_PAD, 2 * LORA_PAD)
    th = jnp.tanh(out["wl"])
    zw = w0_ref[...] + jnp.dot(th, wupw_ref[...], preferred_element_type=F32, precision=HI)
    u = -zw
    softplus = jnp.maximum(u, 0.0) + jnp.log(1.0 + jnp.exp(-jnp.abs(u)))
    wlog = -softplus - 0.5
    ew = jnp.exp(wlog)
    za = a0_ref[...] + jnp.dot(out["al"], wupa_ref[...], preferred_element_type=F32, precision=HI)
    a = _sigmoid(za)
    kkr = out["k"] * kk_ref[...]
    nr = jnp.sqrt(_segsum(kkr * kkr, e_ref, et_ref))
    nrm = jnp.maximum(nr, KK_EPS)
    out.update(th=th, zw=zw, ew=ew, decay=jnp.exp(-ew), a=a, kkr=kkr, nr=nr, nrm=nrm, kk=kkr / nrm)
    out["kh"] = out["k"] * (1.0 + (a - 1.0) * ka_ref[...])
    return out


def _mix_specs(tt, hw, lo_blk):
    mw, lw2 = 3 * hw, 2 * LORA_PAD
    before = lambda i: jnp.maximum(i * (tt // SUBLANE) - 1, 0)
    vec = _full((1, hw))
    return [pl.BlockSpec((tt, mw), lambda i: (i, 0)), pl.BlockSpec((SUBLANE, mw), lambda i: (before(i), 0)),
            pl.BlockSpec((tt, lw2), lambda i: (i, lo_blk)), pl.BlockSpec((SUBLANE, lw2), lambda i: (before(i), lo_blk)),
            _full((1, mw)), _full((1, lw2)), vec, vec, vec, vec, _full((LORA_PAD, hw)), _full((LORA_PAD, hw)),
            _full((hw, LANE)), _full((LANE, hw))]


def _rwkv_pre(proj, mu, mu_lo, w0, a0, k_k, k_a, wup_w, wup_a, e, et, hw, lo_blk):
    t = proj.shape[0]
    tt = _tile(t, RWKV_ROWS, SUBLANE)

    def body(*refs):
        r_o, w_o, kh_o, v_o, kn_o, b_o = refs[-6:]
        f = _rwkv_mix(*refs[:-6], hw, pl.program_id(0) == 0)
        for ref, val in ((r_o, f["r"]), (w_o, f["decay"]), (kh_o, f["kh"]), (v_o, f["v"]), (kn_o, -f["kk"]),
                         (b_o, f["kk"] * f["a"])):
            _store_compact(ref, val, tt, q)

    q = hw // LANE
    return pl.pallas_call(
        body, name="rwkv_pre", grid=(t // tt,),
        in_specs=_mix_specs(tt, hw, lo_blk),
        out_specs=[_compact_spec(tt, q)] * 6,
        out_shape=[jax.ShapeDtypeStruct((t * q, LANE), F32)] * 6,
        compiler_params=_cparams(("parallel",)),
    )(proj, proj, proj, proj, mu, mu_lo, w0, a0, k_k, k_a, wup_w, wup_a, e, et)


def _rwkv_post_math(y, r, kh, v, g, lnx_g, lnx_b, r_k, e_ref, et_ref, n):
    mean = _segsum(y, e_ref, et_ref) * (1.0 / n)
    yc = y - mean
    var = _segsum(yc * yc, e_ref, et_ref) * (1.0 / n)
    rstd = lax.rsqrt(var + GN_EPS_PER_CHANNEL * n)
    yn = yc * rstd
    s = _segsum(r * kh * r_k, e_ref, et_ref)
    y3 = yn * lnx_g + lnx_b + s * v
    sg = _sigmoid(g)
    return yn, rstd, s, y3, sg


def _rwkv_post(y, r, kh, v, proj, lnx_g, lnx_b, r_k, e, et, hw, n, gate_blk):
    t = proj.shape[0]
    tt = _tile(t, RWKV_ROWS, SUBLANE)
    q = hw // LANE

    def body(y_ref, r_ref, kh_ref, v_ref, g_ref, lg_ref, lb_ref, rk_ref, e_ref, et_ref, o_ref):
        g = g_ref[...]
        y, r, kh, v = (_load_compact(ref, tt, q) for ref in (y_ref, r_ref, kh_ref, v_ref))
        _, _, _, y3, sg = _rwkv_post_math(y, r, kh, v, g, lg_ref[...], lb_ref[...], rk_ref[...], e_ref, et_ref, n)
        o_ref[...] = (y3 * (g * sg)).astype(o_ref.dtype)

    row = pl.BlockSpec((tt, hw), lambda i: (i, 0))
    comp = _compact_spec(tt, q)
    vec = _full((1, hw))
    return pl.pallas_call(
        body, name="rwkv_post", grid=(t // tt,),
        in_specs=[comp, comp, comp, comp, pl.BlockSpec((tt, hw), lambda i: (i, gate_blk)),
                  vec, vec, vec, _full((hw, LANE)), _full((LANE, hw))],
        out_specs=row,
        out_shape=jax.ShapeDtypeStruct((t, hw), MXU_DTYPE),
        compiler_params=_cparams(("parallel",)),
    )(y, r, kh, v, proj, lnx_g, lnx_b, r_k, e, et)


def _rwkv_post_bwd(dmix, y, r, kh, v, proj, lnx_g, lnx_b, r_k, e, et, hw, n, gate_blk):
    t = proj.shape[0]
    tt = _tile(t, RWKV_ROWS, SUBLANE)
    q = hw // LANE

    def body(dm_ref, y_ref, r_ref, kh_ref, v_ref, g_ref, lg_ref, lb_ref, rk_ref, e_ref, et_ref,
             dg_o, dy_o, dr_o, dkh_o, dv_o, dlg_o, dlb_o, drk_o):
        i = pl.program_id(0)
        g, rk, lg = g_ref[...], rk_ref[...], lg_ref[...]
        y, r, kh, v = (_load_compact(ref, tt, q) for ref in (y_ref, r_ref, kh_ref, v_ref))
        yn, rstd, s, y3, sg = _rwkv_post_math(y, r, kh, v, g, lg, lb_ref[...], rk, e_ref, et_ref, n)
        dyr = dm_ref[...]
        dy3 = dyr * (g * sg)
        dg_o[...] = (dyr * y3 * (sg * (1.0 + g * (1.0 - sg)))).astype(dg_o.dtype)
        ds = _segsum(dy3 * v, e_ref, et_ref)
        _store_compact(dv_o, dy3 * s, tt, q)
        _store_compact(dr_o, ds * kh * rk, tt, q)
        _store_compact(dkh_o, ds * r * rk, tt, q)
        dyn = dy3 * lg
        m1 = _segsum(dyn, e_ref, et_ref) * (1.0 / n)
        m2 = _segsum(dyn * yn, e_ref, et_ref) * (1.0 / n)
        _store_compact(dy_o, rstd * (dyn - m1 - yn * m2), tt, q)

        @pl.when(i == 0)
        def _():
            dlg_o[...] = jnp.zeros_like(dlg_o)
            dlb_o[...] = jnp.zeros_like(dlb_o)
            drk_o[...] = jnp.zeros_like(drk_o)

        dlg_o[...] += jnp.sum(dy3 * yn, axis=0, keepdims=True)
        dlb_o[...] += jnp.sum(dy3, axis=0, keepdims=True)
        drk_o[...] += jnp.sum(ds * r * kh, axis=0, keepdims=True)

    row = pl.BlockSpec((tt, hw), lambda i: (i, 0))
    comp = _compact_spec(tt, q)
    vec = _full((1, hw))
    rowf = jax.ShapeDtypeStruct((t * q, LANE), F32)
    vecf = jax.ShapeDtypeStruct((1, hw), F32)
    return pl.pallas_call(
        body, name="rwkv_post_bwd", grid=(t // tt,),
        in_specs=[row, comp, comp, comp, comp, pl.BlockSpec((tt, hw), lambda i: (i, gate_blk)),
                  vec, vec, vec, _full((hw, LANE)), _full((LANE, hw))],
        out_specs=[row, comp, comp, comp, comp, vec, vec, vec],
        out_shape=[jax.ShapeDtypeStruct((t, hw), MXU_DTYPE), rowf, rowf, rowf, rowf, vecf, vecf, vecf],
        compiler_params=_cparams(("arbitrary",)),
    )(dmix, y, r, kh, v, proj, lnx_g, lnx_b, r_k, e, et)


def _rwkv_pre_bwd(proj, mu, mu_lo, w0, a0, k_k, k_a, wup_w, wup_a, e, et, hw, lo_blk,
                  dr_rec, dw_rec, dkh_rec, dv_rec, dkn_rec, db_rec, dr_bon, dkh_bon, dv_bon):
    t = proj.shape[0]
    mw, lw2 = 3 * hw, 2 * LORA_PAD
    tt = _tile(t, RWKV_ROWS, SUBLANE)
    q = hw // LANE
    n_in = 14

    def body(*refs):
        mix_refs = refs[:n_in]
        drr, dwr, dkhr, dvr, dknr, db, drb, dkhb, dvb = (
            _load_compact(ref, tt, q) for ref in refs[n_in:n_in + 9])
        dxs_o, dxl_o, dmu_o, dmul_o, dw0_o, da0_o, dkk_o, dka_o, dwupw_o, dwupa_o = refs[n_in + 9:]
        kk_ref, ka_ref, wupw_ref, wupa_ref, e_ref, et_ref = mix_refs[8:14]
        i = pl.program_id(0)
        f = _rwkv_mix(*mix_refs, hw, i == 0)
        k, a, kk, nrm = f["k"], f["a"], f["kk"], f["nrm"]
        k_a, k_k = ka_ref[...], kk_ref[...]
        dr = drr + drb
        dkh = dkhr + dkhb
        dv = dvr + dvb
        da = db * kk + dkh * k * k_a
        dkk = db * a - dknr
        dk = dkh * (1.0 + (a - 1.0) * k_a)
        dka = jnp.sum(dkh * k * (a - 1.0), axis=0, keepdims=True)
        proj_kk = _segsum(dkk * kk, e_ref, et_ref)
        dkkr = jnp.where(f["nr"] > KK_EPS, (dkk - kk * proj_kk) / nrm, dkk * (1.0 / KK_EPS))
        dk = dk + dkkr * k_k
        dkk_w = jnp.sum(dkkr * k, axis=0, keepdims=True)
        dza = da * a * (1.0 - a)
        dzw = dwr * f["decay"] * (-f["ew"]) * _sigmoid(-f["zw"])
        nt_dims = (((1,), (1,)), ((), ()))
        tn_dims = (((0,), (0,)), ((), ()))
        dal = lax.dot_general(dza, wupa_ref[...], nt_dims, preferred_element_type=F32, precision=HI)
        dth = lax.dot_general(dzw, wupw_ref[...], nt_dims, preferred_element_type=F32, precision=HI)
        dwl = dth * (1.0 - f["th"] * f["th"])
        dxs_o[:, 0:hw] = dr
        dxs_o[:, hw:2 * hw] = dk
        dxs_o[:, 2 * hw:3 * hw] = dv
        dxl_o[:, 0:LORA_PAD] = dwl
        dxl_o[:, LORA_PAD:lw2] = dal

        @pl.when(i == 0)
        def _():
            for ref in (dmu_o, dmul_o, dw0_o, da0_o, dkk_o, dka_o, dwupw_o, dwupa_o):
                ref[...] = jnp.zeros_like(ref)

        def colsum(v):
            return jnp.sum(v, axis=0, keepdims=True)

        dmu_o[:, 0:hw] += colsum(dr * f["r_d"])
        dmu_o[:, hw:2 * hw] += colsum(dk * f["k_d"])
        dmu_o[:, 2 * hw:3 * hw] += colsum(dv * f["v_d"])
        dmul_o[:, 0:LORA_PAD] += colsum(dwl * f["wl_d"])
        dmul_o[:, LORA_PAD:lw2] += colsum(dal * f["al_d"])
        dw0_o[...] += colsum(dzw)
        da0_o[...] += colsum(dza)
        dkk_o[...] += dkk_w
        dka_o[...] += dka
        dwupw_o[...] += lax.dot_general(f["th"], dzw, tn_dims, preferred_element_type=F32, precision=HI)
        dwupa_o[...] += lax.dot_general(f["al"], dza, tn_dims, preferred_element_type=F32, precision=HI)

    vec = _full((1, hw))
    vecf = jax.ShapeDtypeStruct((1, hw), F32)
    return pl.pallas_call(
        body, name="rwkv_pre_bwd", grid=(t // tt,),
        in_specs=_mix_specs(tt, hw, lo_blk) + [_compact_spec(tt, q)] * 9,
        out_specs=[pl.BlockSpec((tt, mw), lambda i: (i, 0)), pl.BlockSpec((tt, lw2), lambda i: (i, 0)),
                   _full((1, mw)), _full((1, lw2)), vec, vec, vec, vec,
                   _full((LORA_PAD, hw)), _full((LORA_PAD, hw))],
        out_shape=[jax.ShapeDtypeStruct((t, mw), F32), jax.ShapeDtypeStruct((t, lw2), F32),
                   jax.ShapeDtypeStruct((1, mw), F32), jax.ShapeDtypeStruct((1, lw2), F32),
                   vecf, vecf, vecf, vecf,
                   jax.ShapeDtypeStruct((LORA_PAD, hw), F32), jax.ShapeDtypeStruct((LORA_PAD, hw), F32)],
        compiler_params=_cparams(("arbitrary",)),
    )(proj, proj, proj, proj, mu, mu_lo, w0, a0, k_k, k_a, wup_w, wup_a, e, et,
      dr_rec, dw_rec, dkh_rec, dv_rec, dkn_rec, db_rec, dr_bon, dkh_bon, dv_bon)


def _shift_bwd(dxs, mu, name):
    t, sw = dxs.shape
    tt = _tile(t, 256, SUBLANE)
    nblk = t // SUBLANE

    def body(d_ref, nxt_ref, mu_ref, o_ref):
        last = pl.program_id(0) == pl.num_programs(0) - 1
        cur = d_ref[...]
        first_next = jnp.where(last, 0.0, nxt_ref[0:1, :])
        nxt = pltpu.roll(cur, tt - 1, 0)
        rows = lax.broadcasted_iota(jnp.int32, cur.shape, 0)
        nxt = jnp.where(rows == tt - 1, first_next, nxt)
        m = mu_ref[...]
        o_ref[...] = (cur * (1.0 - m) + nxt * m).astype(o_ref.dtype)

    return pl.pallas_call(
        body, name=name, grid=(t // tt,),
        in_specs=[pl.BlockSpec((tt, sw), lambda i: (i, 0)),
                  pl.BlockSpec((SUBLANE, sw), lambda i: (jnp.minimum((i + 1) * (tt // SUBLANE), nblk - 1), 0)),
                  _full((1, sw))],
        out_specs=pl.BlockSpec((tt, sw), lambda i: (i, 0)),
        out_shape=jax.ShapeDtypeStruct((t, sw), MXU_DTYPE),
        compiler_params=_cparams(("parallel",)),
    )(dxs, dxs, mu)


def _tree_sum(parts):
    while len(parts) > 1:
        parts = [parts[p] + parts[p + 1] for p in range(0, len(parts) - 1, 2)] + ([parts[-1]] if len(parts) % 2 else [])
    return parts[0]


def _tile_rows(src_ref, dst_ref, tc, nh):
    def convert(ts, carry):
        _tile_step(src_ref, dst_ref, ts, nh)
        return carry

    lax.fori_loop(0, tc, convert, 0)


def _tile_step(src_ref, dst_ref, ts, nh):
    for grp, m in enumerate(_tiled(src_ref[ts], nh)):
        dst_ref[ts, grp] = m


def _tiled(v, nh):
    rep = LANE // nh
    lane_group = lax.broadcasted_iota(jnp.int32, v.shape, 1) // nh
    rolled = [v] + [pltpu.roll(v, k * nh, 1) for k in range(1, rep)]
    out = []
    for grp in range(rep):
        m = rolled[(0 - grp) % rep]
        for g in range(1, rep):
            m = jnp.where(lane_group == g, rolled[(g - grp) % rep], m)
        out.append(m)
    return out


def _wkv_fwd(w_c, b_c, k_c, r_c, kn_c, v_c, nh):
    t, q, _ = v_c.shape
    rep = LANE // nh
    n = q * rep
    tc = _tile(t, WKV_CHUNK, 1)
    nc = t // tc
    nacc = 4
    nv = 5

    def body(*refs):
        cur, v_ref, nxt = refs[:nv], refs[nv], refs[nv + 1:2 * nv + 1]
        y_ref, sa_ref, ck_ref, s_ref = refs[2 * nv + 1:2 * nv + 5]
        tiles_even, tiles_odd = refs[2 * nv + 5:3 * nv + 5], refs[3 * nv + 5:]
        c = pl.program_id(0)

        @pl.when(c == 0)
        def _():
            s_ref[...] = jnp.zeros_like(s_ref)
            for src, dst in zip(cur, tiles_even):
                _tile_rows(src, dst, tc, nh)

        ck_ref[0] = s_ref[...]

        def row(ref, ts, j):
            return ref[ts, j % rep, pl.ds(j // rep, 1), :]

        def run(mine, ahead_tiles):
            wt, bt, kt, rt, knt = mine

            def step(ts, carry):
                for src, dst in zip(nxt, ahead_tiles):
                    _tile_step(src, dst, ts, nh)
                vt = v_ref[ts]
                acc = [None] * nacc
                for j in range(n):
                    term = s_ref[j] * row(knt, ts, j)
                    acc[j % nacc] = term if acc[j % nacc] is None else acc[j % nacc] + term
                sa = _tree_sum(acc)
                sa_ref[ts] = sa
                acc = [None] * nacc
                for j in range(n):
                    sj = s_ref[j] * row(wt, ts, j) + sa * row(bt, ts, j) + vt * row(kt, ts, j)
                    s_ref[j] = sj
                    term = sj * row(rt, ts, j)
                    acc[j % nacc] = term if acc[j % nacc] is None else acc[j % nacc] + term
                y_ref[ts] = _tree_sum(acc)
                return carry

            lax.fori_loop(0, tc, step, 0)

        @pl.when(c % 2 == 0)
        def _():
            run(tiles_even, tiles_odd)

        @pl.when(c % 2 == 1)
        def _():
            run(tiles_odd, tiles_even)

    comp = pl.BlockSpec((tc, q, LANE), lambda c: (c, 0, 0))
    ahead = pl.BlockSpec((tc, q, LANE), lambda c: (jnp.minimum(c + 1, nc - 1), 0, 0))
    return pl.pallas_call(
        body, name="wkv_fwd", grid=(nc,),
        in_specs=[comp] * (nv + 1) + [ahead] * nv,
        out_specs=[comp, comp, pl.BlockSpec((1, n, q, LANE), lambda c: (c, 0, 0, 0))],
        out_shape=[jax.ShapeDtypeStruct((t, q, LANE), F32), jax.ShapeDtypeStruct((t, q, LANE), F32),
                   jax.ShapeDtypeStruct((nc, n, q, LANE), F32)],
        scratch_shapes=[pltpu.VMEM((n, q, LANE), F32)] + [pltpu.VMEM((tc, rep, q, LANE), F32)] * (2 * nv),
        compiler_params=_cparams(("arbitrary",)),
    )(w_c, b_c, k_c, r_c, kn_c, v_c, w_c, b_c, k_c, r_c, kn_c)


def _wkv_bwd(r_c, w_c, b_c, k_c, kn_c, dy_c, sa_c, v_c, ck_i, nh, exchange=()):
    t, q, _ = dy_c.shape
    rep = LANE // nh
    n = q * rep
    tc = _tile(t, WKV_CHUNK, 1)
    nc = t // tc
    nacc = 2
    ne = len(exchange)
    nv = 8
    n_in = 2 * nv + 1 + ne

    def body(*refs):
        cur, ck_ref, nxt = refs[:nv], refs[nv], refs[nv + 1:2 * nv + 1]
        rc_ref, wc_ref, bc_ref, kc_ref, knc_ref, dyc_ref = cur[:6]
        parts = refs[2 * nv + 1:n_in]
        dv_o, dr_o, dw_o, db_o, dk_o, dkn_o = refs[n_in:n_in + 6]
        recvs = refs[n_in + 6:n_in + 6 + ne]
        hist, g_ref, gp_ref, dsat_ref = refs[n_in + 6 + ne:n_in + 10 + ne]
        tiles_even = refs[n_in + 10 + ne:n_in + 10 + ne + nv]
        tiles_odd = refs[n_in + 10 + ne + nv:n_in + 10 + ne + 2 * nv]
        c = pl.program_id(0)
        if ne:
            copies = _exchange_copies(parts, recvs, *refs[n_in + 10 + ne + 2 * nv:])

            @pl.when(c == 0)
            def _():
                for cp in copies:
                    cp.start()

        @pl.when(c == 0)
        def _():
            g_ref[...] = jnp.zeros_like(g_ref)
            gp_ref[...] = jnp.zeros_like(gp_ref)
            for src, dst in zip(cur, tiles_even):
                _tile_rows(src, dst, tc, nh)

        def row(ref, ts, idx):
            return ref[ts, idx % rep, pl.ds(idx // rep, 1), :]

        hist[0] = ck_ref[0]

        def run(mine, ahead_tiles):
            rt, wt, bt, kt, knt, dyt, sat, vt = mine

            def fstep(ts, carry):
                wv, bv, kv = wc_ref[ts], bc_ref[ts], kc_ref[ts]
                for i in range(n):
                    hist[ts + 1, i] = hist[ts, i] * wv + row(sat, ts, i) * bv + row(vt, ts, i) * kv
                return carry

            lax.fori_loop(0, tc, fstep, 0)

            def bstep(s, carry):
                ts = tc - 1 - s
                for src, dst in zip(nxt, ahead_tiles):
                    _tile_step(src, dst, ts, nh)
                dy = dyc_ref[ts]
                acc_sa, acc_v = [None] * nacc, [None] * nacc
                for j in range(n):
                    gj = g_ref[j] + dy * row(rt, ts, j)
                    g_ref[j] = gj
                    t1 = gj * row(bt, ts, j)
                    t2 = gj * row(kt, ts, j)
                    a = j % nacc
                    acc_sa[a] = t1 if acc_sa[a] is None else acc_sa[a] + t1
                    acc_v[a] = t2 if acc_v[a] is None else acc_v[a] + t2
                dsa = _tree_sum(acc_sa)
                dv_o[ts] = _tree_sum(acc_v)
                for j in range(n):
                    g_ref[j] = g_ref[j] * row(wt, ts, j) + dsa * row(knt, ts, j)
                for grp, m in enumerate(_tiled(dsa, nh)):
                    dsat_ref[grp] = m
                rv, wv, knv = rc_ref[ts], wc_ref[ts], knc_ref[ts]
                names = ("dr", "dw", "db", "dk", "dkn")
                accs = {nm: [None] * nacc for nm in names}
                for i in range(n):
                    dsai = dsat_ref[i % rep, pl.ds(i // rep, 1), :]
                    dyi = row(dyt, ts, i)
                    s_prev = hist[ts, i]
                    gi = gp_ref[i] + dyi * rv
                    terms = {"dr": hist[ts + 1, i] * dyi, "dw": gi * s_prev, "db": gi * row(sat, ts, i),
                             "dk": gi * row(vt, ts, i), "dkn": dsai * s_prev}
                    a = i % nacc
                    for nm in names:
                        accs[nm][a] = terms[nm] if accs[nm][a] is None else accs[nm][a] + terms[nm]
                    gp_ref[i] = gi * wv + dsai * knv
                dr_o[ts] = _tree_sum(accs["dr"])
                dw_o[ts] = _tree_sum(accs["dw"])
                db_o[ts] = _tree_sum(accs["db"])
                dk_o[ts] = _tree_sum(accs["dk"])
                dkn_o[ts] = _tree_sum(accs["dkn"])
                return carry

            lax.fori_loop(0, tc, bstep, 0)

        @pl.when(c % 2 == 0)
        def _():
            run(tiles_even, tiles_odd)

        @pl.when(c % 2 == 1)
        def _():
            run(tiles_odd, tiles_even)

        if ne:
            @pl.when(c == nc - 1)
            def _():
                for cp in copies:
                    cp.wait()

    comp = pl.BlockSpec((tc, q, LANE), lambda c: (nc - 1 - c, 0, 0))
    ahead = pl.BlockSpec((tc, q, LANE), lambda c: (jnp.maximum(nc - 2 - c, 0), 0, 0))
    outc = jax.ShapeDtypeStruct((t, q, LANE), F32)
    vectors = (r_c, w_c, b_c, k_c, kn_c, dy_c, sa_c, v_c)
    return pl.pallas_call(
        body, name="wkv_bwd", grid=(nc,),
        in_specs=[comp] * nv + [pl.BlockSpec((1, n, q, LANE), lambda c: (nc - 1 - c, 0, 0, 0))] + [ahead] * nv
        + [HBM_SPEC] * ne,
        out_specs=[comp] * 6 + [HBM_SPEC] * ne,
        out_shape=[outc] * 6 + _exchange_shapes(exchange),
        scratch_shapes=[pltpu.VMEM((tc + 1, n, q, LANE), F32), pltpu.VMEM((n, q, LANE), F32),
                        pltpu.VMEM((n, q, LANE), F32), pltpu.VMEM((rep, q, LANE), F32)]
        + [pltpu.VMEM((tc, rep, q, LANE), F32)] * (2 * nv) + _exchange_sems(ne),
        compiler_params=_cparams(("arbitrary",)),
    )(*vectors, ck_i, *vectors, *exchange)


def _conv_stage(gv_ref, gg_ref, gvh_ref, ggh_ref, cw_ref, cb_ref, lg_ref, lb_ref, ext_ref, first, tt, taps):
    u = gv_ref[...] * _sigmoid(gg_ref[...])
    uh = jnp.where(first, 0.0, gvh_ref[...] * _sigmoid(ggh_ref[...]))
    ext_ref[0:CONV_HALO, :] = uh
    ext_ref[CONV_HALO:CONV_HALO + tt, :] = u
    off = CONV_HALO - (taps - 1)
    c = cb_ref[...] + ext_ref[off:off + tt, :] * cw_ref[0:1, :]
    for j in range(1, taps):
        c = c + ext_ref[off + j:off + j + tt, :] * cw_ref[j:j + 1, :]
    mean = jnp.mean(c, axis=-1, keepdims=True)
    cc = c - mean
    rstd = lax.rsqrt(jnp.mean(cc * cc, axis=-1, keepdims=True) + LN_EPS)
    chat = cc * rstd
    cn = chat * lg_ref[...] + lb_ref[...]
    return chat, rstd, cn


def _conv_specs(t, tt, cw, taps, gv_blk, gg_blk):
    hb = tt // CONV_HALO
    return [pl.BlockSpec((tt, cw), lambda i: (i, gv_blk)), pl.BlockSpec((tt, cw), lambda i: (i, gg_blk)),
            pl.BlockSpec((CONV_HALO, cw), lambda i: (jnp.maximum(i * hb - 1, 0), gv_blk)),
            pl.BlockSpec((CONV_HALO, cw), lambda i: (jnp.maximum(i * hb - 1, 0), gg_blk)),
            _full((taps, cw)), _full((1, cw)), _full((1, cw)), _full((1, cw))]


def _conv_fwd(proj, conv_w, conv_b, cln_g, cln_b, cw, gv_blk, gg_blk):
    t = proj.shape[0]
    taps = conv_w.shape[0]
    tt = _tile(t, 128, CONV_HALO)

    def body(gv_ref, gg_ref, gvh_ref, ggh_ref, cw_ref, cb_ref, lg_ref, lb_ref, o_ref, ext_ref):
        _, _, cn = _conv_stage(gv_ref, gg_ref, gvh_ref, ggh_ref, cw_ref, cb_ref, lg_ref, lb_ref, ext_ref,
                               pl.program_id(0) == 0, tt, taps)
        o_ref[...] = (cn * _sigmoid(cn)).astype(o_ref.dtype)

    return pl.pallas_call(
        body, name="conv_fwd", grid=(t // tt,),
        in_specs=_conv_specs(t, tt, cw, taps, gv_blk, gg_blk),
        out_specs=pl.BlockSpec((tt, cw), lambda i: (i, 0)),
        out_shape=jax.ShapeDtypeStruct((t, cw), MXU_DTYPE),
        scratch_shapes=[pltpu.VMEM((CONV_HALO + tt, cw), F32)],
        compiler_params=_cparams(("parallel",)),
    )(proj, proj, proj, proj, conv_w, conv_b, cln_g, cln_b)


def _conv_gate(c2, proj, b_pw2, cw, gc_blk):
    t = c2.shape[0]
    tt = _tile(t, 256, SUBLANE)

    def body(c_ref, g_ref, b_ref, o_ref):
        g = g_ref[...]
        o_ref[...] = ((c_ref[...] + b_ref[...]) * (g * _sigmoid(g))).astype(o_ref.dtype)

    return pl.pallas_call(
        body, name="conv_gate", grid=(t // tt,),
        in_specs=[pl.BlockSpec((tt, cw), lambda i: (i, 0)), pl.BlockSpec((tt, cw), lambda i: (i, gc_blk)),
                  _full((1, cw))],
        out_specs=pl.BlockSpec((tt, cw), lambda i: (i, 0)),
        out_shape=jax.ShapeDtypeStruct((t, cw), MXU_DTYPE),
        compiler_params=_cparams(("parallel",)),
    )(c2, proj, b_pw2)


def _conv_gate_bwd(dmix, c2, proj, b_pw2, cw, dm_blk, gc_blk):
    t = c2.shape[0]
    tt = _tile(t, 256, SUBLANE)

    def body(dm_ref, c_ref, g_ref, b_ref, dc2_o, dg_o, db_o):
        g = g_ref[...]
        sg = _sigmoid(g)
        dyc = dm_ref[...]
        dc2 = dyc * (g * sg)
        dc2_o[...] = dc2.astype(dc2_o.dtype)
        dg_o[...] = (dyc * (c_ref[...] + b_ref[...]) * (sg * (1.0 + g * (1.0 - sg)))).astype(dg_o.dtype)

        @pl.when(pl.program_id(0) == 0)
        def _():
            db_o[...] = jnp.zeros_like(db_o)

        db_o[...] += jnp.sum(dc2, axis=0, keepdims=True)

    row = pl.BlockSpec((tt, cw), lambda i: (i, 0))
    return pl.pallas_call(
        body, name="conv_gate_bwd", grid=(t // tt,),
        in_specs=[pl.BlockSpec((tt, cw), lambda i: (i, dm_blk)), row,
                  pl.BlockSpec((tt, cw), lambda i: (i, gc_blk)), _full((1, cw))],
        out_specs=[row, row, _full((1, cw))],
        out_shape=[jax.ShapeDtypeStruct((t, cw), MXU_DTYPE), jax.ShapeDtypeStruct((t, cw), MXU_DTYPE),
                   jax.ShapeDtypeStruct((1, cw), F32)],
        compiler_params=_cparams(("arbitrary",)),
    )(dmix, c2, proj, b_pw2)


def _conv_bwd_norm(proj, dcs, conv_w, conv_b, cln_g, cln_b, cw, gv_blk, gg_blk):
    t = proj.shape[0]
    taps = conv_w.shape[0]
    tt = _tile(t, 128, CONV_HALO)

    def body(gv_ref, gg_ref, gvh_ref, ggh_ref, cw_ref, cb_ref, lg_ref, lb_ref, dcs_ref,
             dc_o, dcw_o, dcb_o, dlg_o, dlb_o, ext_ref):
        chat, rstd, cn = _conv_stage(gv_ref, gg_ref, gvh_ref, ggh_ref, cw_ref, cb_ref, lg_ref, lb_ref, ext_ref,
                                     pl.program_id(0) == 0, tt, taps)
        s = _sigmoid(cn)
        dcn = dcs_ref[...] * (s * (1.0 + cn * (1.0 - s)))
        dchat = dcn * lg_ref[...]
        dc = rstd * (dchat - jnp.mean(dchat, axis=-1, keepdims=True)
                     - chat * jnp.mean(dchat * chat, axis=-1, keepdims=True))
        dc_o[...] = dc

        @pl.when(pl.program_id(0) == 0)
        def _():
            for ref in (dcw_o, dcb_o, dlg_o, dlb_o):
                ref[...] = jnp.zeros_like(ref)

        dlg_o[...] += jnp.sum(dcn * chat, axis=0, keepdims=True)
        dlb_o[...] += jnp.sum(dcn, axis=0, keepdims=True)
        dcb_o[...] += jnp.sum(dc, axis=0, keepdims=True)
        off = CONV_HALO - (taps - 1)
        for j in range(taps):
            dcw_o[j:j + 1, :] += jnp.sum(ext_ref[off + j:off + j + tt, :] * dc, axis=0, keepdims=True)

    vec = _full((1, cw))
    vecf = jax.ShapeDtypeStruct((1, cw), F32)
    return pl.pallas_call(
        body, name="conv_bwd_norm", grid=(t // tt,),
        in_specs=_conv_specs(t, tt, cw, taps, gv_blk, gg_blk) + [pl.BlockSpec((tt, cw), lambda i: (i, 0))],
        out_specs=[pl.BlockSpec((tt, cw), lambda i: (i, 0)), _full((taps, cw)), vec, vec, vec],
        out_shape=[jax.ShapeDtypeStruct((t, cw), F32), jax.ShapeDtypeStruct((taps, cw), F32), vecf, vecf, vecf],
        scratch_shapes=[pltpu.VMEM((CONV_HALO + tt, cw), F32)],
        compiler_params=_cparams(("arbitrary",)),
    )(proj, proj, proj, proj, conv_w, conv_b, cln_g, cln_b, dcs)


def _conv_bwd_glu(dc, proj, conv_w, cw, gv_blk, gg_blk):
    t = dc.shape[0]
    taps = conv_w.shape[0]
    tt = _tile(t, 128, CONV_HALO)
    hb = tt // CONV_HALO
    nhalo = t // CONV_HALO

    def body(dc_ref, dch_ref, gv_ref, gg_ref, cw_ref, dgv_o, dgg_o, ext_ref):
        last = pl.program_id(0) == pl.num_programs(0) - 1
        ext_ref[0:tt, :] = dc_ref[...]
        ext_ref[tt:tt + CONV_HALO, :] = jnp.where(last, 0.0, dch_ref[...])
        du = ext_ref[taps - 1:taps - 1 + tt, :] * cw_ref[0:1, :]
        for j in range(1, taps):
            du = du + ext_ref[taps - 1 - j:taps - 1 - j + tt, :] * cw_ref[j:j + 1, :]
        sg = _sigmoid(gg_ref[...])
        dgv_o[...] = (du * sg).astype(dgv_o.dtype)
        dgg_o[...] = (du * gv_ref[...] * sg * (1.0 - sg)).astype(dgg_o.dtype)

    row = pl.BlockSpec((tt, cw), lambda i: (i, 0))
    return pl.pallas_call(
        body, name="conv_bwd_glu", grid=(t // tt,),
        in_specs=[row, pl.BlockSpec((CONV_HALO, cw), lambda i: (jnp.minimum((i + 1) * hb, nhalo - 1), 0)),
                  pl.BlockSpec((tt, cw), lambda i: (i, gv_blk)), pl.BlockSpec((tt, cw), lambda i: (i, gg_blk)),
                  _full((taps, cw))],
        out_specs=[row, row],
        out_shape=[jax.ShapeDtypeStruct((t, cw), MXU_DTYPE)] * 2,
        scratch_shapes=[pltpu.VMEM((tt + CONV_HALO, cw), F32)],
        compiler_params=_cparams(("parallel",)),
    )(dc, dc, proj, proj, conv_w)


HBM_SPEC = pl.BlockSpec(memory_space=pltpu.HBM)


def _all_gather(shards, name):
    na = len(shards)

    def body(*refs):
        ins, outs = refs[:na], refs[na:2 * na]
        send_sems, recv_sems, local_sems = refs[2 * na:]
        x, y, c = lax.axis_index("x"), lax.axis_index("y"), lax.axis_index("c")
        me, sibling = (x, y, c), (x, y, 1 - c)
        chips = [(1 - x, y), (x, 1 - y), (1 - x, 1 - y)]

        def slot(px, py, pc):
            return 4 * px + 2 * py + pc

        def copy(a, k, block, to, src=None):
            dst = outs[a].at[slot(*block)]
            return pltpu.make_async_remote_copy(
                src_ref=dst if src is None else src, dst_ref=dst,
                send_sem=send_sems.at[a, k], recv_sem=recv_sems.at[a, k],
                device_id=to, device_id_type=MESH)

        mine = [pltpu.make_async_copy(ins[a], outs[a].at[slot(*me)], local_sems.at[a]) for a in range(na)]
        for cp in mine:
            cp.start()
        first = []
        for a in range(na):
            first.append(copy(a, 0, me, sibling, src=ins[a]))
            first += [copy(a, 1 + j, me, (*chip, c), src=ins[a]) for j, chip in enumerate(chips)]
        for cp in first:
            cp.start()
        passed = []
        for j, chip in enumerate(chips):
            for a in range(na):
                copy(a, 1 + j, (*chip, c), me).wait_recv()
                fwd = copy(a, 4 + j, (*chip, c), sibling)
                fwd.start()
                passed.append(fwd)
        for a in range(na):
            copy(a, 0, sibling, me).wait_recv()
            for j, chip in enumerate(chips):
                copy(a, 4 + j, (*chip, 1 - c), me).wait_recv()
        for cp in first + passed:
            cp.wait_send()
        for cp in mine:
            cp.wait()

    return pl.pallas_call(
        body, name=name,
        in_specs=[HBM_SPEC] * na, out_specs=[HBM_SPEC] * na,
        out_shape=[jax.ShapeDtypeStruct((N_DEV,) + s.shape, s.dtype) for s in shards],
        scratch_shapes=[pltpu.SemaphoreType.DMA((na, 7)), pltpu.SemaphoreType.DMA((na, 7)),
                        pltpu.SemaphoreType.DMA((na,))],
        compiler_params=pltpu.CompilerParams(has_side_effects=True),
    )(*shards)


def _exchange_copies(ins, outs, send_sems, recv_sems):
    x, y, c = lax.axis_index("x"), lax.axis_index("y"), lax.axis_index("c")
    copies = []
    for k in range(1, N_DEV):
        px = 1 - x if k & 4 else x
        py = 1 - y if k & 2 else y
        pc = 1 - c if k & 1 else c
        for a in range(len(ins)):
            copies.append(pltpu.make_async_remote_copy(
                src_ref=ins[a].at[4 * px + 2 * py + pc], dst_ref=outs[a].at[k - 1],
                send_sem=send_sems.at[a, k - 1], recv_sem=recv_sems.at[a, k - 1],
                device_id=(px, py, pc), device_id_type=MESH))
    return copies


def _exchange_shapes(parts):
    return [jax.ShapeDtypeStruct((N_DEV - 1,) + p.shape[1:], p.dtype) for p in parts]


def _exchange_sems(na):
    if not na:
        return []
    return [pltpu.SemaphoreType.DMA((na, N_DEV - 1)), pltpu.SemaphoreType.DMA((na, N_DEV - 1))]


def _cast(v, dtype, name):
    r, c = v.shape
    tr = _tile(r, 256, SUBLANE)

    def body(i_ref, o_ref):
        o_ref[...] = i_ref[...].astype(o_ref.dtype)

    return pl.pallas_call(
        body, name=name, grid=(r // tr,),
        in_specs=[pl.BlockSpec((tr, c), lambda i: (i, 0))],
        out_specs=pl.BlockSpec((tr, c), lambda i: (i, 0)),
        out_shape=jax.ShapeDtypeStruct((r, c), dtype),
        compiler_params=_cparams(("parallel",)),
    )(v)


def _adamw(w, m, v, recv, own, name):
    r, c = w.shape
    ns = recv.shape[0]
    tr = _tile(r, 128, SUBLANE)
    c1 = 1.0 - ADAM_B1 ** ADAM_STEP
    c2 = 1.0 - ADAM_B2 ** ADAM_STEP

    def body(*refs):
        if own is None:
            w_ref, m_ref, v_ref, rc_ref = refs[:4]
            g = rc_ref[0].astype(F32)
            start = 1
        else:
            w_ref, m_ref, v_ref, rc_ref, own_ref = refs[:5]
            g = own_ref[...]
            start = 0
        g_o, d_o, m_o, v_o = refs[-4:]
        for s in range(start, ns):
            g = g + rc_ref[s].astype(F32)
        mn = ADAM_B1 * m_ref[...] + (1.0 - ADAM_B1) * g
        vn = ADAM_B2 * v_ref[...] + (1.0 - ADAM_B2) * (g * g)
        m_hat = mn / c1
        v_hat = vn / c2
        g_o[...] = g
        d_o[...] = -ADAM_LR * (m_hat / (jnp.sqrt(v_hat) + ADAM_EPS) + ADAM_WD * w_ref[...])
        m_o[...] = mn
        v_o[...] = vn

    row = pl.BlockSpec((tr, c), lambda i: (i, 0))
    ins = [w, m, v, recv] + ([] if own is None else [own])
    in_specs = [row, row, row, pl.BlockSpec((ns, tr, c), lambda i: (0, i, 0))] + ([] if own is None else [row])
    return pl.pallas_call(
        body, name=name, grid=(r // tr,),
        in_specs=in_specs, out_specs=[row] * 4,
        out_shape=[jax.ShapeDtypeStruct((r, c), F32)] * 4,
        compiler_params=_cparams(("parallel",)),
    )(*ins)


def _to_t(v, nh, n, axis=-1):
    v = jnp.moveaxis(v, axis, -1)
    v = v.reshape(v.shape[:-1] + (nh, n)).swapaxes(-1, -2).reshape(v.shape)
    return jnp.moveaxis(v, -1, axis)


def _from_t(v, nh, n, axis=-1):
    v = jnp.moveaxis(v, axis, -1)
    v = v.reshape(v.shape[:-1] + (n, nh)).swapaxes(-1, -2).reshape(v.shape)
    return jnp.moveaxis(v, -1, axis)


def _pad_to(v, size, axis):
    pad = [(0, 0)] * v.ndim
    pad[axis] = (0, size - v.shape[axis])
    return jnp.pad(v, pad)


def kernel(x, norm_pre_g, w_in, mu_shift, w0, w_lora_up, a0, a_lora_up, k_k, k_a, r_k, lnx_g, lnx_b, conv_w, conv_b, cln_g, cln_b, w_pw2, b_pw2, w_out, norm_post_g, loss_target, m_norm_pre_g, m_w_in, m_mu_shift, m_w0, m_w_lora_up, m_a0, m_a_lora_up, m_k_k, m_k_a, m_r_k, m_lnx_g, m_lnx_b, m_conv_w, m_conv_b, m_cln_g, m_cln_b, m_w_pw2, m_b_pw2, m_w_out, m_norm_post_g, v_norm_pre_g, v_w_in, v_mu_shift, v_w0, v_w_lora_up, v_a0, v_a_lora_up, v_k_k, v_k_a, v_r_k, v_lnx_g, v_lnx_b, v_conv_w, v_conv_b, v_cln_g, v_cln_b, v_w_pw2, v_b_pw2, v_w_out, v_norm_post_g):
    args = dict(locals())
    t, d = x.shape[1], x.shape[2]
    hw = w0.shape[0]
    cw = conv_b.shape[0]
    nh, n = r_k.shape
    lw, la = w_lora_up.shape[0], a_lora_up.shape[0]
    taps = conv_w.shape[0]
    in_cols = w_in.shape[1] * N_DEV
    shift_cols = 3 * hw + lw + la
    assert in_cols == shift_cols + hw + 3 * cw and hw == cw and hw % LANE == 0 and LANE % nh == 0
    assert lw <= LORA_PAD and la <= LORA_PAD and taps - 1 <= CONV_HALO and hw % (2 * LORA_PAD) == 0
    q = hw // LANE
    gate_blk, gv_blk, gg_blk, gc_blk = 3, 4, 5, 6
    lo0 = 7 * hw
    lo_blk = lo0 // (2 * LORA_PAD)
    x2, tgt2 = x[0], loss_target[0]
    row = lambda v: v.reshape(1, -1)

    gathered = _all_gather(
        [_cast(w_in, MXU_DTYPE, "cast_w_in"), _cast(w_out, MXU_DTYPE, "cast_w_out"),
         _cast(w_pw2, MXU_DTYPE, "cast_w_pw2"), w_lora_up, a_lora_up, conv_w], "gather_weights")
    w_in_g, w_out_g, w_pw2_g, wup_w_g, wup_a_g, conv_w_g = gathered
    w_full = w_in_g.transpose(1, 0, 2).reshape(d, in_cols)
    c0 = shift_cols
    wp = jnp.concatenate([
        _to_t(w_full[:, 0:hw], nh, n), _to_t(w_full[:, hw:2 * hw], nh, n), _to_t(w_full[:, 2 * hw:3 * hw], nh, n),
        _to_t(w_full[:, c0:c0 + hw], nh, n), w_full[:, c0 + hw:],
        _pad_to(w_full[:, 3 * hw:3 * hw + lw], LORA_PAD, 1), _pad_to(w_full[:, 3 * hw + lw:c0], LORA_PAD, 1)], axis=1)
    w_out_f = w_out_g.reshape(N_DEV * w_out.shape[0], d)
    w_out_p = jnp.concatenate([_to_t(w_out_f[:hw], nh, n, axis=0), w_out_f[hw:]], axis=0)
    w_pw2_f = w_pw2_g.reshape(N_DEV * w_pw2.shape[0], cw)
    wup_w = _pad_to(_to_t(wup_w_g.transpose(1, 0, 2).reshape(lw, hw), nh, n), LORA_PAD, 0)
    wup_a = _pad_to(_to_t(wup_a_g.transpose(1, 0, 2).reshape(la, hw), nh, n), LORA_PAD, 0)
    conv_w_f = conv_w_g.transpose(1, 0, 2).reshape(taps, cw)
    mu_p = row(jnp.concatenate([
        _to_t(mu_shift[0:hw], nh, n), _to_t(mu_shift[hw:2 * hw], nh, n), _to_t(mu_shift[2 * hw:3 * hw], nh, n)]))
    mu_lo = row(jnp.concatenate([
        _pad_to(mu_shift[3 * hw:3 * hw + lw], LORA_PAD, 0), _pad_to(mu_shift[3 * hw + lw:], LORA_PAD, 0)]))
    tvec = lambda v: row(_to_t(v, nh, n))
    w0_t, a0_t, kk_t, ka_t, lg_t, lb_t = tvec(w0), tvec(a0), tvec(k_k), tvec(k_a), tvec(lnx_g), tvec(lnx_b)
    rk_t = row(r_k.T)
    head = jnp.arange(hw, dtype=jnp.int32) % nh
    e = (head[:, None] == jnp.arange(LANE, dtype=jnp.int32)[None, :]).astype(F32)
    et = e.T

    c3 = lambda v: v.reshape(t, q, LANE)
    c2d = lambda v: v.reshape(t * q, LANE)
    me = 4 * lax.axis_index("x") + 2 * lax.axis_index("y") + lax.axis_index("c")
    own = lambda parts: lax.dynamic_index_in_dim(parts, me, 0, keepdims=False)
    wire = lambda parts, nm: _cast(parts.reshape(-1, parts.shape[-1]), WIRE_DTYPE, nm).reshape(parts.shape)

    h = _prenorm(x2, row(norm_pre_g))
    proj = _matmul(h, wp, "nn", F32, "mm_proj")
    r_a, w_a, kh_a, v_a, kn_a, b_a = _rwkv_pre(
        proj, mu_p, mu_lo, w0_t, a0_t, kk_t, ka_t, wup_w, wup_a, e, et, hw, lo_blk)
    y_c, sa_c, ck = _wkv_fwd(c3(w_a), c3(b_a), c3(kh_a), c3(r_a), c3(kn_a), c3(v_a), nh)
    y_a = c2d(y_c)
    y_rwkv = _rwkv_post(y_a, r_a, kh_a, v_a, proj, lg_t, lb_t, rk_t, e, et, hw, n, gate_blk)
    cs = _conv_fwd(proj, conv_w_f, row(conv_b), row(cln_g), row(cln_b), cw, gv_blk, gg_blk)
    c2 = _matmul(cs, w_pw2_f, "nn", F32, "mm_pw2")
    y_conv = _conv_gate(c2, proj, row(b_pw2), cw, gc_blk)
    mix = jnp.concatenate([y_rwkv, y_conv], axis=1)
    out = _matmul(mix, w_out_p, "nn", F32, "mm_out")
    dout, dy, loss_part, d_post_g = _post_loss(out, x2, tgt2, row(norm_post_g))

    dmix = _matmul(dout, w_out_p, "nt", F32, "mm_dmix")
    d_w_out_p = _matmul(mix, dout, "tn", F32, "mm_dw_out")
    dc2, dgc, d_b_pw2 = _conv_gate_bwd(dmix, c2, proj, row(b_pw2), cw, 1, gc_blk)
    dcs = _matmul(dc2, w_pw2_f, "nt", F32, "mm_dcs")
    d_w_pw2 = _matmul(cs, dc2, "tn", F32, "mm_dw_pw2")
    dc, d_conv_w, d_conv_b, d_cln_g, d_cln_b = _conv_bwd_norm(
        proj, dcs, conv_w_f, row(conv_b), row(cln_g), row(cln_b), cw, gv_blk, gg_blk)
    dgv, dgg = _conv_bwd_glu(dc, proj, conv_w_f, cw, gv_blk, gg_blk)
    dgr, dy_rec, dr_bon, dkh_bon, dv_bon, d_lg_t, d_lb_t, d_rk_t = _rwkv_post_bwd(
        dmix, y_a, r_a, kh_a, v_a, proj, lg_t, lb_t, rk_t, e, et, hw, n, gate_blk)
    ck_i = ck.reshape(ck.shape[0], n, n, nh).transpose(0, 2, 1, 3).reshape(ck.shape)
    d_w_out_f = jnp.concatenate([_from_t(d_w_out_p[:hw], nh, n, axis=0), d_w_out_p[hw:]], axis=0)
    d_w_out_parts = d_w_out_f.reshape((N_DEV,) + w_out.shape)
    d_w_pw2_parts = d_w_pw2.reshape((N_DEV,) + w_pw2.shape)
    dv_c, dr_c, dw_c, db_c, dk_c, dkn_c, recv_w_out, recv_w_pw2 = _wkv_bwd(
        c3(r_a), c3(w_a), c3(b_a), c3(kh_a), c3(kn_a), c3(dy_rec), sa_c, c3(v_a), ck_i, nh,
        exchange=[wire(d_w_out_parts, "wire_w_out"), wire(d_w_pw2_parts, "wire_w_pw2")])
    dxs, dxs_lo, d_mu_p, d_mu_lo, d_w0_t, d_a0_t, d_kk_t, d_ka_t, d_wup_w, d_wup_a = _rwkv_pre_bwd(
        proj, mu_p, mu_lo, w0_t, a0_t, kk_t, ka_t, wup_w, wup_a, e, et, hw, lo_blk,
        c2d(dr_c), c2d(dw_c), c2d(dk_c), c2d(dv_c), c2d(dkn_c), c2d(db_c), dr_bon, dkh_bon, dv_bon)
    dshift = _shift_bwd(dxs, mu_p, "shift_bwd")
    dshift_lo = _shift_bwd(dxs_lo, mu_lo, "shift_bwd_lora")
    dproj = jnp.concatenate([dshift, dgr, dgv, dgg, dgc, dshift_lo], axis=1)
    d_wp = _matmul(h, dproj, "tn", F32, "mm_dw_in")

    d_w_full = jnp.concatenate([
        _from_t(d_wp[:, 0:hw], nh, n), _from_t(d_wp[:, hw:2 * hw], nh, n), _from_t(d_wp[:, 2 * hw:3 * hw], nh, n),
        d_wp[:, lo0:lo0 + lw], d_wp[:, lo0 + LORA_PAD:lo0 + LORA_PAD + la],
        _from_t(d_wp[:, 3 * hw:4 * hw], nh, n), d_wp[:, 4 * hw:lo0]], axis=1)
    sc = w_in.shape[1]
    d_w_in_parts = d_w_full.reshape(d, N_DEV, sc).transpose(1, 0, 2)
    colparts = lambda v: v.reshape(v.shape[0], N_DEV, v.shape[1] // N_DEV).transpose(1, 0, 2)
    d_wup_w_parts = colparts(_from_t(d_wup_w[:lw], nh, n))
    d_wup_a_parts = colparts(_from_t(d_wup_a[:la], nh, n))
    d_conv_w_parts = colparts(d_conv_w)
    dh, recv_w_in, recv_wup_w, recv_wup_a, recv_conv_w = _matmul(
        dproj, wp, "nt", F32, "mm_dh",
        exchange=[wire(d_w_in_parts, "wire_w_in"), d_wup_w_parts, d_wup_a_parts, d_conv_w_parts])
    grad_x, d_pre_g = _prenorm_bwd(dh, x2, dy, row(norm_pre_g))
    d_mu = jnp.concatenate([
        _from_t(d_mu_p[0, 0:hw], nh, n), _from_t(d_mu_p[0, hw:2 * hw], nh, n), _from_t(d_mu_p[0, 2 * hw:3 * hw], nh, n),
        d_mu_lo[0, 0:lw], d_mu_lo[0, LORA_PAD:LORA_PAD + la]])
    ft = lambda v: _from_t(v[0], nh, n)
    small = {
        "norm_pre_g": d_pre_g[0], "mu_shift": d_mu, "w0": ft(d_w0_t), "a0": ft(d_a0_t), "k_k": ft(d_kk_t),
        "k_a": ft(d_ka_t), "r_k": d_rk_t[0].reshape(n, nh).T.reshape(-1), "lnx_g": ft(d_lg_t), "lnx_b": ft(d_lb_t),
        "conv_b": d_conv_b[0], "cln_g": d_cln_g[0], "cln_b": d_cln_b[0], "b_pw2": d_b_pw2[0],
        "norm_post_g": d_post_g[0]}
    small_names = list(small)
    packed = jnp.concatenate([small[k] for k in small_names] + [loss_part[0, 0:1]])
    plen = packed.shape[0]
    ppad = -(-plen // LANE) * LANE
    packed = _pad_to(packed, ppad, 0).reshape(1, ppad)

    (packed_all,) = _all_gather([packed], "gather_small")

    res = {}
    sharded = [("w_in", d_w_in_parts, recv_w_in), ("w_out", d_w_out_parts, recv_w_out),
               ("w_pw2", d_w_pw2_parts, recv_w_pw2), ("w_lora_up", d_wup_w_parts, recv_wup_w),
               ("a_lora_up", d_wup_a_parts, recv_wup_a), ("conv_w", d_conv_w_parts, recv_conv_w)]
    for nm, parts, rc in sharded:
        res[nm] = _adamw(args[nm], args["m_" + nm], args["v_" + nm], rc, own(parts), "adamw_" + nm)
    w_small = _pad_to(jnp.concatenate([args[k].reshape(-1) for k in small_names]), ppad, 0).reshape(1, ppad)
    m_small = _pad_to(jnp.concatenate([args["m_" + k].reshape(-1) for k in small_names]), ppad, 0).reshape(1, ppad)
    v_small = _pad_to(jnp.concatenate([args["v_" + k].reshape(-1) for k in small_names]), ppad, 0).reshape(1, ppad)
    g_s, d_s, m_s, v_s = _adamw(w_small, m_small, v_small, packed_all, None, "adamw_small")
    off = 0
    for k in small_names:
        size = args[k].size
        res[k] = tuple(o[0, off:off + size].reshape(args[k].shape) for o in (g_s, d_s, m_s, v_s))
        off += size
    loss = g_s[0, plen - 1]

    order = ["norm_pre_g", "w_in", "mu_shift", "w0", "w_lora_up", "a0", "a_lora_up", "k_k", "k_a", "r_k",
             "lnx_g", "lnx_b", "conv_w", "conv_b", "cln_g", "cln_b", "w_pw2", "b_pw2", "w_out", "norm_post_g"]
    outs = [loss, grad_x[None]]
    for slot in range(4):
        outs += [res[k][slot] for k in order]
    return tuple(outs)
```

```python
import functools

import jax
import jax.numpy as jnp
from jax import lax
from jax.experimental import pallas as pl
from jax.experimental.pallas import tpu as pltpu

F32 = jnp.float32
MXU_DTYPE = jnp.bfloat16
WIRE_DTYPE = jnp.bfloat16
HI = lax.Precision.HIGHEST
SEG_PRECISION = lax.Precision.HIGH

NORM_EPS = 1e-6
LN_EPS = 1e-5
GN_EPS_PER_CHANNEL = 1e-5
KK_EPS = 1e-12
ADAM_LR = 0.001
ADAM_B1 = 0.9
ADAM_B2 = 0.999
ADAM_EPS = 1e-08
ADAM_WD = 0.01
ADAM_STEP = 10

LANE = 128
SUBLANE = 8
LORA_PAD = 128
CONV_HALO = 32
N_DEV = 8
VMEM_LIMIT = 56 * 1024 * 1024
WKV_CHUNK = 16
RWKV_ROWS = 64
MESH = pl.DeviceIdType.MESH


def _tile(n, target, mult):
    if n <= target:
        return n
    best = None
    for d in range(mult, target + 1, mult):
        if n % d == 0:
            best = d
    assert best is not None, (n, target, mult)
    return best


def _cparams(sem=None):
    return pltpu.CompilerParams(dimension_semantics=sem, vmem_limit_bytes=VMEM_LIMIT)


def _sigmoid(x):
    return 1.0 / (1.0 + jnp.exp(-x))


def _full(shape):
    nd = len(shape)
    return pl.BlockSpec(shape, lambda *_: (0,) * nd)


def _compact_spec(tt, q):
    return pl.BlockSpec((tt * q, LANE), lambda i: (i, 0))


def _load_compact(ref, tt, q):
    return jnp.concatenate([ref[pl.ds(p, tt, stride=q), :] for p in range(q)], axis=1)


def _store_compact(ref, val, tt, q):
    for p in range(q):
        ref[pl.ds(p, tt, stride=q), :] = val[:, p * LANE:(p + 1) * LANE]


def _matmul(a, b, mode, out_dtype, name, exchange=(), pattern="direct"):
    if mode == "nn":
        (m, k), (k2, n) = a.shape, b.shape
    elif mode == "nt":
        (m, k), (n, k2) = a.shape, b.shape
    else:
        (k, m), (k2, n) = a.shape, b.shape
    assert k == k2, (a.shape, b.shape, mode)
    tm, tn, tk = _tile(m, 1024, LANE), _tile(n, 768, LANE), _tile(k, 2048, LANE)
    nk = k // tk
    ne = len(exchange)
    grid = (m // tm, n // tn, nk)
    if mode == "nn":
        a_spec = pl.BlockSpec((tm, tk), lambda i, j, kk: (i, kk))
        b_spec = pl.BlockSpec((tk, tn), lambda i, j, kk: (kk, j))
        dims = (((1,), (0,)), ((), ()))
    elif mode == "nt":
        a_spec = pl.BlockSpec((tm, tk), lambda i, j, kk: (i, kk))
        b_spec = pl.BlockSpec((tn, tk), lambda i, j, kk: (j, kk))
        dims = (((1,), (1,)), ((), ()))
    else:
        a_spec = pl.BlockSpec((tk, tm), lambda i, j, kk: (kk, i))
        b_spec = pl.BlockSpec((tk, tn), lambda i, j, kk: (kk, j))
        dims = (((0,), (0,)), ((), ()))

    def body(*refs):
        a_ref, b_ref = refs[:2]
        parts = refs[2:2 + ne]
        o_ref = refs[2 + ne]
        recvs = refs[3 + ne:3 + 2 * ne]
        acc_ref = refs[3 + 2 * ne]
        i, j, kk = pl.program_id(0), pl.program_id(1), pl.program_id(2)
        if ne:
            copies = _exchange_copies(parts, recvs, *refs[4 + 2 * ne:], pattern=pattern)

            @pl.when((i == 0) & (j == 0) & (kk == 0))
            def _():
                for cp in copies:
                    cp.start()

        @pl.when(kk == 0)
        def _():
            acc_ref[...] = jnp.zeros_like(acc_ref)

        acc_ref[...] += lax.dot_general(a_ref[...], b_ref[...], dims, preferred_element_type=F32)

        @pl.when(kk == nk - 1)
        def _():
            o_ref[...] = acc_ref[...].astype(o_ref.dtype)

        if ne:
            @pl.when((i == grid[0] - 1) & (j == grid[1] - 1) & (kk == nk - 1))
            def _():
                for cp in copies:
                    cp.wait()

    res = pl.pallas_call(
        body, name=name,
        grid=grid,
        in_specs=[a_spec, b_spec] + [HBM_SPEC] * ne,
        out_specs=[pl.BlockSpec((tm, tn), lambda i, j, kk: (i, j))] + [HBM_SPEC] * ne,
        out_shape=[jax.ShapeDtypeStruct((m, n), out_dtype)] + _exchange_shapes(exchange, pattern),
        scratch_shapes=[pltpu.VMEM((tm, tn), F32)] + _exchange_sems(ne, pattern),
        compiler_params=_cparams(("arbitrary",) * 3 if ne else ("parallel", "parallel", "arbitrary")),
    )(a, b, *exchange)
    return res if ne else res[0]


def _prenorm(x, g):
    t, d = x.shape
    tt = _tile(t, 256, SUBLANE)

    def body(x_ref, g_ref, h_ref):
        xv = x_ref[...]
        rinv = lax.rsqrt(jnp.mean(xv * xv, axis=-1, keepdims=True) + NORM_EPS)
        h_ref[...] = (xv * rinv * g_ref[...]).astype(h_ref.dtype)

    return pl.pallas_call(
        body, name="prenorm", grid=(t // tt,),
        in_specs=[pl.BlockSpec((tt, d), lambda i: (i, 0)), _full((1, d))],
        out_specs=pl.BlockSpec((tt, d), lambda i: (i, 0)),
        out_shape=jax.ShapeDtypeStruct((t, d), MXU_DTYPE),
        compiler_params=_cparams(("parallel",)),
    )(x, g)


def _post_loss(out, x, target, g):
    t, d = out.shape
    tt = _tile(t, 128, SUBLANE)

    def body(o_ref, x_ref, t_ref, g_ref, dout_ref, dy_ref, loss_ref, dg_ref):
        i = pl.program_id(0)
        ov = o_ref[...]
        rinv = lax.rsqrt(jnp.mean(ov * ov, axis=-1, keepdims=True) + NORM_EPS)
        nv = ov * rinv
        gv = g_ref[...]
        err = x_ref[...] + nv * gv - t_ref[...]
        part = 0.5 * jnp.sum(jnp.mean(err * err, axis=-1, keepdims=True), axis=0, keepdims=True)
        dy = err * (1.0 / d)
        dy_ref[...] = dy
        dn = dy * gv
        dout = rinv * (dn - nv * jnp.mean(dn * nv, axis=-1, keepdims=True))
        dout_ref[...] = dout.astype(dout_ref.dtype)
        dg = jnp.sum(dy * nv, axis=0, keepdims=True)

        @pl.when(i == 0)
        def _():
            loss_ref[...] = jnp.zeros_like(loss_ref)
            dg_ref[...] = jnp.zeros_like(dg_ref)

        loss_ref[...] += jnp.broadcast_to(part, loss_ref.shape)
        dg_ref[...] += dg

    row = pl.BlockSpec((tt, d), lambda i: (i, 0))
    return pl.pallas_call(
        body, name="post_loss", grid=(t // tt,),
        in_specs=[row, row, row, _full((1, d))],
        out_specs=[row, row, _full((1, LANE)), _full((1, d))],
        out_shape=[jax.ShapeDtypeStruct((t, d), MXU_DTYPE), jax.ShapeDtypeStruct((t, d), F32),
                   jax.ShapeDtypeStruct((1, LANE), F32), jax.ShapeDtypeStruct((1, d), F32)],
        compiler_params=_cparams(("arbitrary",)),
    )(out, x, target, g)


def _prenorm_bwd(dh, x, dy, g):
    t, d = x.shape
    tt = _tile(t, 128, SUBLANE)

    def body(dh_ref, x_ref, dy_ref, g_ref, gx_ref, dg_ref):
        i = pl.program_id(0)
        xv = x_ref[...]
        rinv = lax.rsqrt(jnp.mean(xv * xv, axis=-1, keepdims=True) + NORM_EPS)
        nx = xv * rinv
        dhv = dh_ref[...]
        dnx = dhv * g_ref[...]
        dx = rinv * (dnx - nx * jnp.mean(dnx * nx, axis=-1, keepdims=True))
        gx_ref[...] = dy_ref[...] + dx

        @pl.when(i == 0)
        def _():
            dg_ref[...] = jnp.zeros_like(dg_ref)

        dg_ref[...] += jnp.sum(dhv * nx, axis=0, keepdims=True)

    row = pl.BlockSpec((tt, d), lambda i: (i, 0))
    return pl.pallas_call(
        body, name="prenorm_bwd", grid=(t // tt,),
        in_specs=[row, row, row, _full((1, d))],
        out_specs=[row, _full((1, d))],
        out_shape=[jax.ShapeDtypeStruct((t, d), F32), jax.ShapeDtypeStruct((1, d), F32)],
        compiler_params=_cparams(("arbitrary",)),
    )(dh, x, dy, g)


def _segsum(v, e_ref, et_ref):
    s = jnp.dot(v, e_ref[...], preferred_element_type=F32, precision=SEG_PRECISION)
    return jnp.dot(s, et_ref[...], preferred_element_type=F32, precision=SEG_PRECISION)


def _shifted(cur_ref, prev_ref, lo, hi, first):
    cur = cur_ref[:, lo:hi]
    last = jnp.where(first, 0.0, prev_ref[SUBLANE - 1:SUBLANE, lo:hi])
    prev = pltpu.roll(cur, 1, 0)
    rows = lax.broadcasted_iota(jnp.int32, cur.shape, 0)
    return cur, jnp.where(rows == 0, last, prev)


def _rwkv_mix(main_ref, mainp_ref, lo_ref, lop_ref, mu_ref, mulo_ref, w0_ref, a0_ref, kk_ref, ka_ref,
              wupw_ref, wupa_ref, e_ref, et_ref, hw, first):
    def xs(cur_ref, prev_ref, m_ref, lo, hi):
        cur, prev = _shifted(cur_ref, prev_ref, lo, hi, first)
        return cur + (prev - cur) * m_ref[:, lo:hi], prev - cur

    out = {}
    out["r"], out["r_d"] = xs(main_ref, mainp_ref, mu_ref, 0, hw)
    out["k"], out["k_d"] = xs(main_ref, mainp_ref, mu_ref, hw, 2 * hw)
    out["v"], out["v_d"] = xs(main_ref, mainp_ref, mu_ref, 2 * hw, 3 * hw)
    out["wl"], out["wl_d"] = xs(lo_ref, lop_ref, mulo_ref, 0, LORA_PAD)
    out["al"], out["al_d"] = xs(lo_ref, lop_ref, mulo_ref, LORA_PAD, 2 * LORA_PAD)
    th = jnp.tanh(out["wl"])
    zw = w0_ref[...] + jnp.dot(th, wupw_ref[...], preferred_element_type=F32, precision=HI)
    u = -zw
    softplus = jnp.maximum(u, 0.0) + jnp.log(1.0 + jnp.exp(-jnp.abs(u)))
    wlog = -softplus - 0.5
    ew = jnp.exp(wlog)
    za = a0_ref[...] + jnp.dot(out["al"], wupa_ref[...], preferred_element_type=F32, precision=HI)
    a = _sigmoid(za)
    kkr = out["k"] * kk_ref[...]
    nr = jnp.sqrt(_segsum(kkr * kkr, e_ref, et_ref))
    nrm = jnp.maximum(nr, KK_EPS)
    out.update(th=th, zw=zw, ew=ew, decay=jnp.exp(-ew), a=a, kkr=kkr, nr=nr, nrm=nrm, kk=kkr / nrm)
    out["kh"] = out["k"] * (1.0 + (a - 1.0) * ka_ref[...])
    return out


def _mix_specs(tt, hw, lo_blk):
    mw, lw2 = 3 * hw, 2 * LORA_PAD
    before = lambda i: jnp.maximum(i * (tt // SUBLANE) - 1, 0)
    vec = _full((1, hw))
    return [pl.BlockSpec((tt, mw), lambda i: (i, 0)), pl.BlockSpec((SUBLANE, mw), lambda i: (before(i), 0)),
            pl.BlockSpec((tt, lw2), lambda i: (i, lo_blk)), pl.BlockSpec((SUBLANE, lw2), lambda i: (before(i), lo_blk)),
            _full((1, mw)), _full((1, lw2)), vec, vec, vec, vec, _full((LORA_PAD, hw)), _full((LORA_PAD, hw)),
            _full((hw, LANE)), _full((LANE, hw))]


def _rwkv_pre(proj, mu, mu_lo, w0, a0, k_k, k_a, wup_w, wup_a, e, et, hw, lo_blk):
    t = proj.shape[0]
    tt = _tile(t, RWKV_ROWS, SUBLANE)

    def body(*refs):
        r_o, w_o, kh_o, v_o, kn_o, b_o = refs[-6:]
        f = _rwkv_mix(*refs[:-6], hw, pl.program_id(0) == 0)
        for ref, val in ((r_o, f["r"]), (w_o, f["decay"]), (kh_o, f["kh"]), (v_o, f["v"]), (kn_o, -f["kk"]),
                         (b_o, f["kk"] * f["a"])):
            _store_compact(ref, val, tt, q)

    q = hw // LANE
    return pl.pallas_call(
        body, name="rwkv_pre", grid=(t // tt,),
        in_specs=_mix_specs(tt, hw, lo_blk),
        out_specs=[_compact_spec(tt, q)] * 6,
        out_shape=[jax.ShapeDtypeStruct((t * q, LANE), F32)] * 6,
        compiler_params=_cparams(("parallel",)),
    )(proj, proj, proj, proj, mu, mu_lo, w0, a0, k_k, k_a, wup_w, wup_a, e, et)


def _rwkv_post_math(y, r, kh, v, g, lnx_g, lnx_b, r_k, e_ref, et_ref, n):
    mean = _segsum(y, e_ref, et_ref) * (1.0 / n)
    yc = y - mean
    var = _segsum(yc * yc, e_ref, et_ref) * (1.0 / n)
    rstd = lax.rsqrt(var + GN_EPS_PER_CHANNEL * n)
    yn = yc * rstd
    s = _segsum(r * kh * r_k, e_ref, et_ref)
    y3 = yn * lnx_g + lnx_b + s * v
    sg = _sigmoid(g)
    return yn, rstd, s, y3, sg


def _rwkv_post(y, r, kh, v, proj, lnx_g, lnx_b, r_k, e, et, hw, n, gate_blk):
    t = proj.shape[0]
    tt = _tile(t, RWKV_ROWS, SUBLANE)
    q = hw // LANE

    def body(y_ref, r_ref, kh_ref, v_ref, g_ref, lg_ref, lb_ref, rk_ref, e_ref, et_ref, o_ref):
        g = g_ref[...]
        y, r, kh, v = (_load_compact(ref, tt, q) for ref in (y_ref, r_ref, kh_ref, v_ref))
        _, _, _, y3, sg = _rwkv_post_math(y, r, kh, v, g, lg_ref[...], lb_ref[...], rk_ref[...], e_ref, et_ref, n)
        o_ref[...] = (y3 * (g * sg)).astype(o_ref.dtype)

    row = pl.BlockSpec((tt, hw), lambda i: (i, 0))
    comp = _compact_spec(tt, q)
    vec = _full((1, hw))
    return pl.pallas_call(
        body, name="rwkv_post", grid=(t // tt,),
        in_specs=[comp, comp, comp, comp, pl.BlockSpec((tt, hw), lambda i: (i, gate_blk)),
                  vec, vec, vec, _full((hw, LANE)), _full((LANE, hw))],
        out_specs=row,
        out_shape=jax.ShapeDtypeStruct((t, hw), MXU_DTYPE),
        compiler_params=_cparams(("parallel",)),
    )(y, r, kh, v, proj, lnx_g, lnx_b, r_k, e, et)


def _rwkv_post_bwd(dmix, y, r, kh, v, proj, lnx_g, lnx_b, r_k, e, et, hw, n, gate_blk):
    t = proj.shape[0]
    tt = _tile(t, RWKV_ROWS, SUBLANE)
    q = hw // LANE

    def body(dm_ref, y_ref, r_ref, kh_ref, v_ref, g_ref, lg_ref, lb_ref, rk_ref, e_ref, et_ref,
             dg_o, dy_o, dr_o, dkh_o, dv_o, dlg_o, dlb_o, drk_o):
        i = pl.program_id(0)
        g, rk, lg = g_ref[...], rk_ref[...], lg_ref[...]
        y, r, kh, v = (_load_compact(ref, tt, q) for ref in (y_ref, r_ref, kh_ref, v_ref))
        yn, rstd, s, y3, sg = _rwkv_post_math(y, r, kh, v, g, lg, lb_ref[...], rk, e_ref, et_ref, n)
        dyr = dm_ref[...]
        dy3 = dyr * (g * sg)
        dg_o[...] = (dyr * y3 * (sg * (1.0 + g * (1.0 - sg)))).astype(dg_o.dtype)
        ds = _segsum(dy3 * v, e_ref, et_ref)
        _store_compact(dv_o, dy3 * s, tt, q)
        _store_compact(dr_o, ds * kh * rk, tt, q)
        _store_compact(dkh_o, ds * r * rk, tt, q)
        dyn = dy3 * lg
        m1 = _segsum(dyn, e_ref, et_ref) * (1.0 / n)
        m2 = _segsum(dyn * yn, e_ref, et_ref) * (1.0 / n)
        _store_compact(dy_o, rstd * (dyn - m1 - yn * m2), tt, q)

        @pl.when(i == 0)
        def _():
            dlg_o[...] = jnp.zeros_like(dlg_o)
            dlb_o[...] = jnp.zeros_like(dlb_o)
            drk_o[...] = jnp.zeros_like(drk_o)

        dlg_o[...] += jnp.sum(dy3 * yn, axis=0, keepdims=True)
        dlb_o[...] += jnp.sum(dy3, axis=0, keepdims=True)
        drk_o[...] += jnp.sum(ds * r * kh, axis=0, keepdims=True)

    row = pl.BlockSpec((tt, hw), lambda i: (i, 0))
    comp = _compact_spec(tt, q)
    vec = _full((1, hw))
    rowf = jax.ShapeDtypeStruct((t * q, LANE), F32)
    vecf = jax.ShapeDtypeStruct((1, hw), F32)
    return pl.pallas_call(
        body, name="rwkv_post_bwd", grid=(t // tt,),
        in_specs=[row, comp, comp, comp, comp, pl.BlockSpec((tt, hw), lambda i: (i, gate_blk)),
                  vec, vec, vec, _full((hw, LANE)), _full((LANE, hw))],
        out_specs=[row, comp, comp, comp, comp, vec, vec, vec],
        out_shape=[jax.ShapeDtypeStruct((t, hw), MXU_DTYPE), rowf, rowf, rowf, rowf, vecf, vecf, vecf],
        compiler_params=_cparams(("arbitrary",)),
    )(dmix, y, r, kh, v, proj, lnx_g, lnx_b, r_k, e, et)


def _rwkv_pre_bwd(proj, mu, mu_lo, w0, a0, k_k, k_a, wup_w, wup_a, e, et, hw, lo_blk,
                  dr_rec, dw_rec, dkh_rec, dv_rec, dkn_rec, db_rec, dr_bon, dkh_bon, dv_bon):
    t = proj.shape[0]
    mw, lw2 = 3 * hw, 2 * LORA_PAD
    tt = _tile(t, RWKV_ROWS, SUBLANE)
    q = hw // LANE
    n_in = 14

    def body(*refs):
        mix_refs = refs[:n_in]
        drr, dwr, dkhr, dvr, dknr, db, drb, dkhb, dvb = (
            _load_compact(ref, tt, q) for ref in refs[n_in:n_in + 9])
        dxs_o, dxl_o, dmu_o, dmul_o, dw0_o, da0_o, dkk_o, dka_o, dwupw_o, dwupa_o = refs[n_in + 9:]
        kk_ref, ka_ref, wupw_ref, wupa_ref, e_ref, et_ref = mix_refs[8:14]
        i = pl.program_id(0)
        f = _rwkv_mix(*mix_refs, hw, i == 0)
        k, a, kk, nrm = f["k"], f["a"], f["kk"], f["nrm"]
        k_a, k_k = ka_ref[...], kk_ref[...]
        dr = drr + drb
        dkh = dkhr + dkhb
        dv = dvr + dvb
        da = db * kk + dkh * k * k_a
        dkk = db * a - dknr
        dk = dkh * (1.0 + (a - 1.0) * k_a)
        dka = jnp.sum(dkh * k * (a - 1.0), axis=0, keepdims=True)
        proj_kk = _segsum(dkk * kk, e_ref, et_ref)
        dkkr = jnp.where(f["nr"] > KK_EPS, (dkk - kk * proj_kk) / nrm, dkk * (1.0 / KK_EPS))
        dk = dk + dkkr * k_k
        dkk_w = jnp.sum(dkkr * k, axis=0, keepdims=True)
        dza = da * a * (1.0 - a)
        dzw = dwr * f["decay"] * (-f["ew"]) * _sigmoid(-f["zw"])
        nt_dims = (((1,), (1,)), ((), ()))
        tn_dims = (((0,), (0,)), ((), ()))
        dal = lax.dot_general(dza, wupa_ref[...], nt_dims, preferred_element_type=F32, precision=HI)
        dth = lax.dot_general(dzw, wupw_ref[...], nt_dims, preferred_element_type=F32, precision=HI)
        dwl = dth * (1.0 - f["th"] * f["th"])
        dxs_o[:, 0:hw] = dr
        dxs_o[:, hw:2 * hw] = dk
        dxs_o[:, 2 * hw:3 * hw] = dv
        dxl_o[:, 0:LORA_PAD] = dwl
        dxl_o[:, LORA_PAD:lw2] = dal

        @pl.when(i == 0)
        def _():
            for ref in (dmu_o, dmul_o, dw0_o, da0_o, dkk_o, dka_o, dwupw_o, dwupa_o):
                ref[...] = jnp.zeros_like(ref)

        def colsum(v):
            return jnp.sum(v, axis=0, keepdims=True)

        dmu_o[:, 0:hw] += colsum(dr * f["r_d"])
        dmu_o[:, hw:2 * hw] += colsum(dk * f["k_d"])
        dmu_o[:, 2 * hw:3 * hw] += colsum(dv * f["v_d"])
        dmul_o[:, 0:LORA_PAD] += colsum(dwl * f["wl_d"])
        dmul_o[:, LORA_PAD:lw2] += colsum(dal * f["al_d"])
        dw0_o[...] += colsum(dzw)
        da0_o[...] += colsum(dza)
        dkk_o[...] += dkk_w
        dka_o[...] += dka
        dwupw_o[...] += lax.dot_general(f["th"], dzw, tn_dims, preferred_element_type=F32, precision=HI)
        dwupa_o[...] += lax.dot_general(f["al"], dza, tn_dims, preferred_element_type=F32, precision=HI)

    vec = _full((1, hw))
    vecf = jax.ShapeDtypeStruct((1, hw), F32)
    return pl.pallas_call(
        body, name="rwkv_pre_bwd", grid=(t // tt,),
        in_specs=_mix_specs(tt, hw, lo_blk) + [_compact_spec(tt, q)] * 9,
        out_specs=[pl.BlockSpec((tt, mw), lambda i: (i, 0)), pl.BlockSpec((tt, lw2), lambda i: (i, 0)),
                   _full((1, mw)), _full((1, lw2)), vec, vec, vec, vec,
                   _full((LORA_PAD, hw)), _full((LORA_PAD, hw))],
        out_shape=[jax.ShapeDtypeStruct((t, mw), F32), jax.ShapeDtypeStruct((t, lw2), F32),
                   jax.ShapeDtypeStruct((1, mw), F32), jax.ShapeDtypeStruct((1, lw2), F32),
                   vecf, vecf, vecf, vecf,
                   jax.ShapeDtypeStruct((LORA_PAD, hw), F32), jax.ShapeDtypeStruct((LORA_PAD, hw), F32)],
        compiler_params=_cparams(("arbitrary",)),
    )(proj, proj, proj, proj, mu, mu_lo, w0, a0, k_k, k_a, wup_w, wup_a, e, et,
      dr_rec, dw_rec, dkh_rec, dv_rec, dkn_rec, db_rec, dr_bon, dkh_bon, dv_bon)


def _shift_bwd(dxs, mu, name):
    t, sw = dxs.shape
    tt = _tile(t, 256, SUBLANE)
    nblk = t // SUBLANE

    def body(d_ref, nxt_ref, mu_ref, o_ref):
        last = pl.program_id(0) == pl.num_programs(0) - 1
        cur = d_ref[...]
        first_next = jnp.where(last, 0.0, nxt_ref[0:1, :])
        nxt = pltpu.roll(cur, tt - 1, 0)
        rows = lax.broadcasted_iota(jnp.int32, cur.shape, 0)
        nxt = jnp.where(rows == tt - 1, first_next, nxt)
        m = mu_ref[...]
        o_ref[...] = (cur * (1.0 - m) + nxt * m).astype(o_ref.dtype)

    return pl.pallas_call(
        body, name=name, grid=(t // tt,),
        in_specs=[pl.BlockSpec((tt, sw), lambda i: (i, 0)),
                  pl.BlockSpec((SUBLANE, sw), lambda i: (jnp.minimum((i + 1) * (tt // SUBLANE), nblk - 1), 0)),
                  _full((1, sw))],
        out_specs=pl.BlockSpec((tt, sw), lambda i: (i, 0)),
        out_shape=jax.ShapeDtypeStruct((t, sw), MXU_DTYPE),
        compiler_params=_cparams(("parallel",)),
    )(dxs, dxs, mu)


def _tree_sum(parts):
    while len(parts) > 1:
        parts = [parts[p] + parts[p + 1] for p in range(0, len(parts) - 1, 2)] + ([parts[-1]] if len(parts) % 2 else [])
    return parts[0]


def _tile_rows(src_ref, dst_ref, tc, nh):
    def convert(ts, carry):
        _tile_step(src_ref, dst_ref, ts, nh)
        return carry

    lax.fori_loop(0, tc, convert, 0)


def _tile_step(src_ref, dst_ref, ts, nh):
    for grp, m in enumerate(_tiled(src_ref[ts], nh)):
        dst_ref[ts, grp] = m


def _tiled(v, nh):
    rep = LANE // nh
    lane_group = lax.broadcasted_iota(jnp.int32, v.shape, 1) // nh
    rolled = [v] + [pltpu.roll(v, k * nh, 1) for k in range(1, rep)]
    out = []
    for grp in range(rep):
        m = rolled[(0 - grp) % rep]
        for g in range(1, rep):
            m = jnp.where(lane_group == g, rolled[(g - grp) % rep], m)
        out.append(m)
    return out


def _wkv_fwd(w_c, b_c, k_c, r_c, kn_c, v_c, nh):
    t, q, _ = v_c.shape
    rep = LANE // nh
    n = q * rep
    tc = _tile(t, WKV_CHUNK, 1)
    nc = t // tc
    nacc = 4
    nv = 5

    def body(*refs):
        cur, v_ref, nxt = refs[:nv], refs[nv], refs[nv + 1:2 * nv + 1]
        y_ref, sa_ref, ck_ref, s_ref = refs[2 * nv + 1:2 * nv + 5]
        tiles_even, tiles_odd = refs[2 * nv + 5:3 * nv + 5], refs[3 * nv + 5:]
        c = pl.program_id(0)

        @pl.when(c == 0)
        def _():
            s_ref[...] = jnp.zeros_like(s_ref)
            for src, dst in zip(cur, tiles_even):
                _tile_rows(src, dst, tc, nh)

        ck_ref[0] = s_ref[...]

        def row(ref, ts, j):
            return ref[ts, j % rep, pl.ds(j // rep, 1), :]

        def run(mine, ahead_tiles):
            wt, bt, kt, rt, knt = mine

            def step(ts, carry):
                for src, dst in zip(nxt, ahead_tiles):
                    _tile_step(src, dst, ts, nh)
                vt = v_ref[ts]
                acc = [None] * nacc
                for j in range(n):
                    term = s_ref[j] * row(knt, ts, j)
                    acc[j % nacc] = term if acc[j % nacc] is None else acc[j % nacc] + term
                sa = _tree_sum(acc)
                sa_ref[ts] = sa
                acc = [None] * nacc
                for j in range(n):
                    sj = s_ref[j] * row(wt, ts, j) + sa * row(bt, ts, j) + vt * row(kt, ts, j)
                    s_ref[j] = sj
                    term = sj * row(rt, ts, j)
                    acc[j % nacc] = term if acc[j % nacc] is None else acc[j % nacc] + term
                y_ref[ts] = _tree_sum(acc)
                return carry

            lax.fori_loop(0, tc, step, 0)

        @pl.when(c % 2 == 0)
        def _():
            run(tiles_even, tiles_odd)

        @pl.when(c % 2 == 1)
        def _():
            run(tiles_odd, tiles_even)

    comp = pl.BlockSpec((tc, q, LANE), lambda c: (c, 0, 0))
    ahead = pl.BlockSpec((tc, q, LANE), lambda c: (jnp.minimum(c + 1, nc - 1), 0, 0))
    return pl.pallas_call(
        body, name="wkv_fwd", grid=(nc,),
        in_specs=[comp] * (nv + 1) + [ahead] * nv,
        out_specs=[comp, comp, pl.BlockSpec((1, n, q, LANE), lambda c: (c, 0, 0, 0))],
        out_shape=[jax.ShapeDtypeStruct((t, q, LANE), F32), jax.ShapeDtypeStruct((t, q, LANE), F32),
                   jax.ShapeDtypeStruct((nc, n, q, LANE), F32)],
        scratch_shapes=[pltpu.VMEM((n, q, LANE), F32)] + [pltpu.VMEM((tc, rep, q, LANE), F32)] * (2 * nv),
        compiler_params=_cparams(("arbitrary",)),
    )(w_c, b_c, k_c, r_c, kn_c, v_c, w_c, b_c, k_c, r_c, kn_c)


def _wkv_bwd(r_c, w_c, b_c, k_c, kn_c, dy_c, sa_c, v_c, ck_i, nh, exchange=()):
    t, q, _ = dy_c.shape
    rep = LANE // nh
    n = q * rep
    tc = _tile(t, WKV_CHUNK, 1)
    nc = t // tc
    nacc = 2
    ne = len(exchange)
    nv = 8
    n_in = 2 * nv + 1 + ne

    def body(*refs):
        cur, ck_ref, nxt = refs[:nv], refs[nv], refs[nv + 1:2 * nv + 1]
        rc_ref, wc_ref, bc_ref, kc_ref, knc_ref, dyc_ref = cur[:6]
        parts = refs[2 * nv + 1:n_in]
        dv_o, dr_o, dw_o, db_o, dk_o, dkn_o = refs[n_in:n_in + 6]
        recvs = refs[n_in + 6:n_in + 6 + ne]
        hist, g_ref, gp_ref, dsat_ref = refs[n_in + 6 + ne:n_in + 10 + ne]
        tiles_even = refs[n_in + 10 + ne:n_in + 10 + ne + nv]
        tiles_odd = refs[n_in + 10 + ne + nv:n_in + 10 + ne + 2 * nv]
        c = pl.program_id(0)
        if ne:
            copies = _exchange_copies(parts, recvs, *refs[n_in + 10 + ne + 2 * nv:])

            @pl.when(c == 0)
            def _():
                for cp in copies:
                    cp.start()

        @pl.when(c == 0)
        def _():
            g_ref[...] = jnp.zeros_like(g_ref)
            gp_ref[...] = jnp.zeros_like(gp_ref)
            for src, dst in zip(cur, tiles_even):
                _tile_rows(src, dst, tc, nh)

        def row(ref, ts, idx):
            return ref[ts, idx % rep, pl.ds(idx // rep, 1), :]

        hist[0] = ck_ref[0]

        def run(mine, ahead_tiles):
            rt, wt, bt, kt, knt, dyt, sat, vt = mine

            def fstep(ts, carry):
                wv, bv, kv = wc_ref[ts], bc_ref[ts], kc_ref[ts]
                for i in range(n):
                    hist[ts + 1, i] = hist[ts, i] * wv + row(sat, ts, i) * bv + row(vt, ts, i) * kv
                return carry

            lax.fori_loop(0, tc, fstep, 0)

            def bstep(s, carry):
                ts = tc - 1 - s
                for src, dst in zip(nxt, ahead_tiles):
                    _tile_step(src, dst, ts, nh)
                dy = dyc_ref[ts]
                acc_sa, acc_v = [None] * nacc, [None] * nacc
                for j in range(n):
                    gj = g_ref[j] + dy * row(rt, ts, j)
                    g_ref[j] = gj
                    t1 = gj * row(bt, ts, j)
                    t2 = gj * row(kt, ts, j)
                    a = j % nacc
                    acc_sa[a] = t1 if acc_sa[a] is None else acc_sa[a] + t1
                    acc_v[a] = t2 if acc_v[a] is None else acc_v[a] + t2
                dsa = _tree_sum(acc_sa)
                dv_o[ts] = _tree_sum(acc_v)
                for j in range(n):
                    g_ref[j] = g_ref[j] * row(wt, ts, j) + dsa * row(knt, ts, j)
                for grp, m in enumerate(_tiled(dsa, nh)):
                    dsat_ref[grp] = m
                rv, wv, knv = rc_ref[ts], wc_ref[ts], knc_ref[ts]
                names = ("dr", "dw", "db", "dk", "dkn")
                accs = {nm: [None] * nacc for nm in names}
                for i in range(n):
                    dsai = dsat_ref[i % rep, pl.ds(i // rep, 1), :]
                    dyi = row(dyt, ts, i)
                    s_prev = hist[ts, i]
                    gi = gp_ref[i] + dyi * rv
                    terms = {"dr": hist[ts + 1, i] * dyi, "dw": gi * s_prev, "db": gi * row(sat, ts, i),
                             "dk": gi * row(vt, ts, i), "dkn": dsai * s_prev}
                    a = i % nacc
                    for nm in names:
                        accs[nm][a] = terms[nm] if accs[nm][a] is None else accs[nm][a] + terms[nm]
                    gp_ref[i] = gi * wv + dsai * knv
                dr_o[ts] = _tree_sum(accs["dr"])
                dw_o[ts] = _tree_sum(accs["dw"])
                db_o[ts] = _tree_sum(accs["db"])
                dk_o[ts] = _tree_sum(accs["dk"])
                dkn_o[ts] = _tree_sum(accs["dkn"])
                return carry

            lax.fori_loop(0, tc, bstep, 0)

        @pl.when(c % 2 == 0)
        def _():
            run(tiles_even, tiles_odd)

        @pl.when(c % 2 == 1)
        def _():
            run(tiles_odd, tiles_even)

        if ne:
            @pl.when(c == nc - 1)
            def _():
                for cp in copies:
                    cp.wait()

    comp = pl.BlockSpec((tc, q, LANE), lambda c: (nc - 1 - c, 0, 0))
    ahead = pl.BlockSpec((tc, q, LANE), lambda c: (jnp.maximum(nc - 2 - c, 0), 0, 0))
    outc = jax.ShapeDtypeStruct((t, q, LANE), F32)
    vectors = (r_c, w_c, b_c, k_c, kn_c, dy_c, sa_c, v_c)
    return pl.pallas_call(
        body, name="wkv_bwd", grid=(nc,),
        in_specs=[comp] * nv + [pl.BlockSpec((1, n, q, LANE), lambda c: (nc - 1 - c, 0, 0, 0))] + [ahead] * nv
        + [HBM_SPEC] * ne,
        out_specs=[comp] * 6 + [HBM_SPEC] * ne,
        out_shape=[outc] * 6 + _exchange_shapes(exchange),
        scratch_shapes=[pltpu.VMEM((tc + 1, n, q, LANE), F32), pltpu.VMEM((n, q, LANE), F32),
                        pltpu.VMEM((n, q, LANE), F32), pltpu.VMEM((rep, q, LANE), F32)]
        + [pltpu.VMEM((tc, rep, q, LANE), F32)] * (2 * nv) + _exchange_sems(ne),
        compiler_params=_cparams(("arbitrary",)),
    )(*vectors, ck_i, *vectors, *exchange)


def _conv_stage(gv_ref, gg_ref, gvh_ref, ggh_ref, cw_ref, cb_ref, lg_ref, lb_ref, ext_ref, first, tt, taps):
    u = gv_ref[...] * _sigmoid(gg_ref[...])
    uh = jnp.where(first, 0.0, gvh_ref[...] * _sigmoid(ggh_ref[...]))
    ext_ref[0:CONV_HALO, :] = uh
    ext_ref[CONV_HALO:CONV_HALO + tt, :] = u
    off = CONV_HALO - (taps - 1)
    c = cb_ref[...] + ext_ref[off:off + tt, :] * cw_ref[0:1, :]
    for j in range(1, taps):
        c = c + ext_ref[off + j:off + j + tt, :] * cw_ref[j:j + 1, :]
    mean = jnp.mean(c, axis=-1, keepdims=True)
    cc = c - mean
    rstd = lax.rsqrt(jnp.mean(cc * cc, axis=-1, keepdims=True) + LN_EPS)
    chat = cc * rstd
    cn = chat * lg_ref[...] + lb_ref[...]
    return chat, rstd, cn


def _conv_specs(t, tt, cw, taps, gv_blk, gg_blk):
    hb = tt // CONV_HALO
    return [pl.BlockSpec((tt, cw), lambda i: (i, gv_blk)), pl.BlockSpec((tt, cw), lambda i: (i, gg_blk)),
            pl.BlockSpec((CONV_HALO, cw), lambda i: (jnp.maximum(i * hb - 1, 0), gv_blk)),
            pl.BlockSpec((CONV_HALO, cw), lambda i: (jnp.maximum(i * hb - 1, 0), gg_blk)),
            _full((taps, cw)), _full((1, cw)), _full((1, cw)), _full((1, cw))]


def _conv_fwd(proj, conv_w, conv_b, cln_g, cln_b, cw, gv_blk, gg_blk):
    t = proj.shape[0]
    taps = conv_w.shape[0]
    tt = _tile(t, 128, CONV_HALO)

    def body(gv_ref, gg_ref, gvh_ref, ggh_ref, cw_ref, cb_ref, lg_ref, lb_ref, o_ref, ext_ref):
        _, _, cn = _conv_stage(gv_ref, gg_ref, gvh_ref, ggh_ref, cw_ref, cb_ref, lg_ref, lb_ref, ext_ref,
                               pl.program_id(0) == 0, tt, taps)
        o_ref[...] = (cn * _sigmoid(cn)).astype(o_ref.dtype)

    return pl.pallas_call(
        body, name="conv_fwd", grid=(t // tt,),
        in_specs=_conv_specs(t, tt, cw, taps, gv_blk, gg_blk),
        out_specs=pl.BlockSpec((tt, cw), lambda i: (i, 0)),
        out_shape=jax.ShapeDtypeStruct((t, cw), MXU_DTYPE),
        scratch_shapes=[pltpu.VMEM((CONV_HALO + tt, cw), F32)],
        compiler_params=_cparams(("parallel",)),
    )(proj, proj, proj, proj, conv_w, conv_b, cln_g, cln_b)


def _conv_gate(c2, proj, b_pw2, cw, gc_blk):
    t = c2.shape[0]
    tt = _tile(t, 256, SUBLANE)

    def body(c_ref, g_ref, b_ref, o_ref):
        g = g_ref[...]
        o_ref[...] = ((c_ref[...] + b_ref[...]) * (g * _sigmoid(g))).astype(o_ref.dtype)

    return pl.pallas_call(
        body, name="conv_gate", grid=(t // tt,),
        in_specs=[pl.BlockSpec((tt, cw), lambda i: (i, 0)), pl.BlockSpec((tt, cw), lambda i: (i, gc_blk)),
                  _full((1, cw))],
        out_specs=pl.BlockSpec((tt, cw), lambda i: (i, 0)),
        out_shape=jax.ShapeDtypeStruct((t, cw), MXU_DTYPE),
        compiler_params=_cparams(("parallel",)),
    )(c2, proj, b_pw2)


def _conv_gate_bwd(dmix, c2, proj, b_pw2, cw, dm_blk, gc_blk):
    t = c2.shape[0]
    tt = _tile(t, 256, SUBLANE)

    def body(dm_ref, c_ref, g_ref, b_ref, dc2_o, dg_o, db_o):
        g = g_ref[...]
        sg = _sigmoid(g)
        dyc = dm_ref[...]
        dc2 = dyc * (g * sg)
        dc2_o[...] = dc2.astype(dc2_o.dtype)
        dg_o[...] = (dyc * (c_ref[...] + b_ref[...]) * (sg * (1.0 + g * (1.0 - sg)))).astype(dg_o.dtype)

        @pl.when(pl.program_id(0) == 0)
        def _():
            db_o[...] = jnp.zeros_like(db_o)

        db_o[...] += jnp.sum(dc2, axis=0, keepdims=True)

    row = pl.BlockSpec((tt, cw), lambda i: (i, 0))
    return pl.pallas_call(
        body, name="conv_gate_bwd", grid=(t // tt,),
        in_specs=[pl.BlockSpec((tt, cw), lambda i: (i, dm_blk)), row,
                  pl.BlockSpec((tt, cw), lambda i: (i, gc_blk)), _full((1, cw))],
        out_specs=[row, row, _full((1, cw))],
        out_shape=[jax.ShapeDtypeStruct((t, cw), MXU_DTYPE), jax.ShapeDtypeStruct((t, cw), MXU_DTYPE),
                   jax.ShapeDtypeStruct((1, cw), F32)],
        compiler_params=_cparams(("arbitrary",)),
    )(dmix, c2, proj, b_pw2)


def _conv_bwd_norm(proj, dcs, conv_w, conv_b, cln_g, cln_b, cw, gv_blk, gg_blk):
    t = proj.shape[0]
    taps = conv_w.shape[0]
    tt = _tile(t, 128, CONV_HALO)

    def body(gv_ref, gg_ref, gvh_ref, ggh_ref, cw_ref, cb_ref, lg_ref, lb_ref, dcs_ref,
             dc_o, dcw_o, dcb_o, dlg_o, dlb_o, ext_ref):
        chat, rstd, cn = _conv_stage(gv_ref, gg_ref, gvh_ref, ggh_ref, cw_ref, cb_ref, lg_ref, lb_ref, ext_ref,
                                     pl.program_id(0) == 0, tt, taps)
        s = _sigmoid(cn)
        dcn = dcs_ref[...] * (s * (1.0 + cn * (1.0 - s)))
        dchat = dcn * lg_ref[...]
        dc = rstd * (dchat - jnp.mean(dchat, axis=-1, keepdims=True)
                     - chat * jnp.mean(dchat * chat, axis=-1, keepdims=True))
        dc_o[...] = dc

        @pl.when(pl.program_id(0) == 0)
        def _():
            for ref in (dcw_o, dcb_o, dlg_o, dlb_o):
                ref[...] = jnp.zeros_like(ref)

        dlg_o[...] += jnp.sum(dcn * chat, axis=0, keepdims=True)
        dlb_o[...] += jnp.sum(dcn, axis=0, keepdims=True)
        dcb_o[...] += jnp.sum(dc, axis=0, keepdims=True)
        off = CONV_HALO - (taps - 1)
        for j in range(taps):
            dcw_o[j:j + 1, :] += jnp.sum(ext_ref[off + j:off + j + tt, :] * dc, axis=0, keepdims=True)

    vec = _full((1, cw))
    vecf = jax.ShapeDtypeStruct((1, cw), F32)
    return pl.pallas_call(
        body, name="conv_bwd_norm", grid=(t // tt,),
        in_specs=_conv_specs(t, tt, cw, taps, gv_blk, gg_blk) + [pl.BlockSpec((tt, cw), lambda i: (i, 0))],
        out_specs=[pl.BlockSpec((tt, cw), lambda i: (i, 0)), _full((taps, cw)), vec, vec, vec],
        out_shape=[jax.ShapeDtypeStruct((t, cw), F32), jax.ShapeDtypeStruct((taps, cw), F32), vecf, vecf, vecf],
        scratch_shapes=[pltpu.VMEM((CONV_HALO + tt, cw), F32)],
        compiler_params=_cparams(("arbitrary",)),
    )(proj, proj, proj, proj, conv_w, conv_b, cln_g, cln_b, dcs)


def _conv_bwd_glu(dc, proj, conv_w, cw, gv_blk, gg_blk):
    t = dc.shape[0]
    taps = conv_w.shape[0]
    tt = _tile(t, 128, CONV_HALO)
    hb = tt // CONV_HALO
    nhalo = t // CONV_HALO

    def body(dc_ref, dch_ref, gv_ref, gg_ref, cw_ref, dgv_o, dgg_o, ext_ref):
        last = pl.program_id(0) == pl.num_programs(0) - 1
        ext_ref[0:tt, :] = dc_ref[...]
        ext_ref[tt:tt + CONV_HALO, :] = jnp.where(last, 0.0, dch_ref[...])
        du = ext_ref[taps - 1:taps - 1 + tt, :] * cw_ref[0:1, :]
        for j in range(1, taps):
            du = du + ext_ref[taps - 1 - j:taps - 1 - j + tt, :] * cw_ref[j:j + 1, :]
        sg = _sigmoid(gg_ref[...])
        dgv_o[...] = (du * sg).astype(dgv_o.dtype)
        dgg_o[...] = (du * gv_ref[...] * sg * (1.0 - sg)).astype(dgg_o.dtype)

    row = pl.BlockSpec((tt, cw), lambda i: (i, 0))
    return pl.pallas_call(
        body, name="conv_bwd_glu", grid=(t // tt,),
        in_specs=[row, pl.BlockSpec((CONV_HALO, cw), lambda i: (jnp.minimum((i + 1) * hb, nhalo - 1), 0)),
                  pl.BlockSpec((tt, cw), lambda i: (i, gv_blk)), pl.BlockSpec((tt, cw), lambda i: (i, gg_blk)),
                  _full((taps, cw))],
        out_specs=[row, row],
        out_shape=[jax.ShapeDtypeStruct((t, cw), MXU_DTYPE)] * 2,
        scratch_shapes=[pltpu.VMEM((tt + CONV_HALO, cw), F32)],
        compiler_params=_cparams(("parallel",)),
    )(dc, dc, proj, proj, conv_w)


HBM_SPEC = pl.BlockSpec(memory_space=pltpu.HBM)


def _all_gather(shards, name):
    na = len(shards)

    def body(*refs):
        ins, outs = refs[:na], refs[na:2 * na]
        send_sems, recv_sems, local_sems = refs[2 * na:]
        x, y, c = lax.axis_index("x"), lax.axis_index("y"), lax.axis_index("c")
        me, sibling = (x, y, c), (x, y, 1 - c)
        chips = [(1 - x, y), (x, 1 - y), (1 - x, 1 - y)]

        def slot(px, py, pc):
            return 4 * px + 2 * py + pc

        def copy(a, k, block, to, src=None):
            dst = outs[a].at[slot(*block)]
            return pltpu.make_async_remote_copy(
                src_ref=dst if src is None else src, dst_ref=dst,
                send_sem=send_sems.at[a, k], recv_sem=recv_sems.at[a, k],
                device_id=to, device_id_type=MESH)

        mine = [pltpu.make_async_copy(ins[a], outs[a].at[slot(*me)], local_sems.at[a]) for a in range(na)]
        for cp in mine:
            cp.start()
        first = []
        for a in range(na):
            first.append(copy(a, 0, me, sibling, src=ins[a]))
            first += [copy(a, 1 + j, me, (*chip, c), src=ins[a]) for j, chip in enumerate(chips)]
        for cp in first:
            cp.start()
        passed = []
        for j, chip in enumerate(chips):
            for a in range(na):
                copy(a, 1 + j, (*chip, c), me).wait_recv()
                fwd = copy(a, 4 + j, (*chip, c), sibling)
                fwd.start()
                passed.append(fwd)
        for a in range(na):
            copy(a, 0, sibling, me).wait_recv()
            for j, chip in enumerate(chips):
                copy(a, 4 + j, (*chip, 1 - c), me).wait_recv()
        for cp in first + passed:
            cp.wait_send()
        for cp in mine:
            cp.wait()

    return pl.pallas_call(
        body, name=name,
        in_specs=[HBM_SPEC] * na, out_specs=[HBM_SPEC] * na,
        out_shape=[jax.ShapeDtypeStruct((N_DEV,) + s.shape, s.dtype) for s in shards],
        scratch_shapes=[pltpu.SemaphoreType.DMA((na, 7)), pltpu.SemaphoreType.DMA((na, 7)),
                        pltpu.SemaphoreType.DMA((na,))],
        compiler_params=pltpu.CompilerParams(has_side_effects=True),
    )(*shards)


N_CHIPS = N_DEV // 2
EXCHANGE_SLOTS = {"direct": N_DEV - 1, "sibling": N_CHIPS, "chips": N_CHIPS - 1}


def _exchange_copies(ins, outs, send_sems, recv_sems, pattern="direct"):
    x, y, c = lax.axis_index("x"), lax.axis_index("y"), lax.axis_index("c")
    copies = []

    def add(a, src_slot, dst_slot, sem, peer):
        copies.append(pltpu.make_async_remote_copy(
            src_ref=ins[a].at[src_slot], dst_ref=outs[a].at[dst_slot],
            send_sem=send_sems.at[a, sem], recv_sem=recv_sems.at[a, sem],
            device_id=peer, device_id_type=MESH))

    for a in range(len(ins)):
        if pattern == "direct":
            for k in range(1, N_DEV):
                px = 1 - x if k & 4 else x
                py = 1 - y if k & 2 else y
                pc = 1 - c if k & 1 else c
                add(a, 4 * px + 2 * py + pc, k - 1, k - 1, (px, py, pc))
        elif pattern == "sibling":
            for j in range(N_CHIPS):
                add(a, j, j, j, (x, y, 1 - c))
        else:
            for k in range(1, N_CHIPS):
                px = 1 - x if k & 2 else x
                py = 1 - y if k & 1 else y
                add(a, 2 * px + py, k - 1, k - 1, (px, py, c))
    return copies


def _exchange_shapes(parts, pattern="direct"):
    return [jax.ShapeDtypeStruct((EXCHANGE_SLOTS[pattern],) + p.shape[1:], p.dtype) for p in parts]


def _exchange_sems(na, pattern="direct"):
    if not na:
        return []
    return [pltpu.SemaphoreType.DMA((na, EXCHANGE_SLOTS[pattern]))] * 2


def _exchange(parts, name, pattern):
    na = len(parts)

    def body(*refs):
        copies = _exchange_copies(refs[:na], refs[na:2 * na], *refs[2 * na:], pattern=pattern)
        for cp in copies:
            cp.start()
        for cp in copies:
            cp.wait()

    return pl.pallas_call(
        body, name=name,
        in_specs=[HBM_SPEC] * na, out_specs=[HBM_SPEC] * na,
        out_shape=_exchange_shapes(parts, pattern),
        scratch_shapes=_exchange_sems(na, pattern),
        compiler_params=pltpu.CompilerParams(has_side_effects=True),
    )(*parts)


def _chip_sum(mine, theirs):
    nslot, r, c = mine.shape
    rows = nslot * r
    tr = _tile(rows, 256, SUBLANE)

    def body(a_ref, b_ref, f_ref, w_ref):
        s = a_ref[...] + b_ref[...].astype(F32)
        f_ref[...] = s
        w_ref[...] = s.astype(w_ref.dtype)

    blk = pl.BlockSpec((tr, c), lambda i: (i, 0))
    f, w = pl.pallas_call(
        body, name="chip_sum", grid=(rows // tr,),
        in_specs=[blk, blk], out_specs=[blk, blk],
        out_shape=[jax.ShapeDtypeStruct((rows, c), F32), jax.ShapeDtypeStruct((rows, c), theirs.dtype)],
        compiler_params=_cparams(("parallel",)),
    )(mine.reshape(rows, c), theirs.reshape(rows, c))
    return f.reshape(mine.shape), w.reshape(mine.shape)


def _cast(v, dtype, name):
    r, c = v.shape
    tr = _tile(r, 256, SUBLANE)

    def body(i_ref, o_ref):
        o_ref[...] = i_ref[...].astype(o_ref.dtype)

    return pl.pallas_call(
        body, name=name, grid=(r // tr,),
        in_specs=[pl.BlockSpec((tr, c), lambda i: (i, 0))],
        out_specs=pl.BlockSpec((tr, c), lambda i: (i, 0)),
        out_shape=jax.ShapeDtypeStruct((r, c), dtype),
        compiler_params=_cparams(("parallel",)),
    )(v)


def _adamw(w, m, v, recv, own, name):
    r, c = w.shape
    ns = recv.shape[0]
    tr = _tile(r, 128, SUBLANE)
    c1 = 1.0 - ADAM_B1 ** ADAM_STEP
    c2 = 1.0 - ADAM_B2 ** ADAM_STEP

    def body(*refs):
        if own is None:
            w_ref, m_ref, v_ref, rc_ref = refs[:4]
            g = rc_ref[0].astype(F32)
            start = 1
        else:
            w_ref, m_ref, v_ref, rc_ref, own_ref = refs[:5]
            g = own_ref[...]
            start = 0
        g_o, d_o, m_o, v_o = refs[-4:]
        for s in range(start, ns):
            g = g + rc_ref[s].astype(F32)
        mn = ADAM_B1 * m_ref[...] + (1.0 - ADAM_B1) * g
        vn = ADAM_B2 * v_ref[...] + (1.0 - ADAM_B2) * (g * g)
        m_hat = mn / c1
        v_hat = vn / c2
        g_o[...] = g
        d_o[...] = -ADAM_LR * (m_hat / (jnp.sqrt(v_hat) + ADAM_EPS) + ADAM_WD * w_ref[...])
        m_o[...] = mn
        v_o[...] = vn

    row = pl.BlockSpec((tr, c), lambda i: (i, 0))
    ins = [w, m, v, recv] + ([] if own is None else [own])
    in_specs = [row, row, row, pl.BlockSpec((ns, tr, c), lambda i: (0, i, 0))] + ([] if own is None else [row])
    return pl.pallas_call(
        body, name=name, grid=(r // tr,),
        in_specs=in_specs, out_specs=[row] * 4,
        out_shape=[jax.ShapeDtypeStruct((r, c), F32)] * 4,
        compiler_params=_cparams(("parallel",)),
    )(*ins)


def _to_t(v, nh, n, axis=-1):
    v = jnp.moveaxis(v, axis, -1)
    v = v.reshape(v.shape[:-1] + (nh, n)).swapaxes(-1, -2).reshape(v.shape)
    return jnp.moveaxis(v, -1, axis)


def _from_t(v, nh, n, axis=-1):
    v = jnp.moveaxis(v, axis, -1)
    v = v.reshape(v.shape[:-1] + (n, nh)).swapaxes(-1, -2).reshape(v.shape)
    return jnp.moveaxis(v, -1, axis)


def _pad_to(v, size, axis):
    pad = [(0, 0)] * v.ndim
    pad[axis] = (0, size - v.shape[axis])
    return jnp.pad(v, pad)


def kernel(x, norm_pre_g, w_in, mu_shift, w0, w_lora_up, a0, a_lora_up, k_k, k_a, r_k, lnx_g, lnx_b, conv_w, conv_b, cln_g, cln_b, w_pw2, b_pw2, w_out, norm_post_g, loss_target, m_norm_pre_g, m_w_in, m_mu_shift, m_w0, m_w_lora_up, m_a0, m_a_lora_up, m_k_k, m_k_a, m_r_k, m_lnx_g, m_lnx_b, m_conv_w, m_conv_b, m_cln_g, m_cln_b, m_w_pw2, m_b_pw2, m_w_out, m_norm_post_g, v_norm_pre_g, v_w_in, v_mu_shift, v_w0, v_w_lora_up, v_a0, v_a_lora_up, v_k_k, v_k_a, v_r_k, v_lnx_g, v_lnx_b, v_conv_w, v_conv_b, v_cln_g, v_cln_b, v_w_pw2, v_b_pw2, v_w_out, v_norm_post_g):
    args = dict(locals())
    t, d = x.shape[1], x.shape[2]
    hw = w0.shape[0]
    cw = conv_b.shape[0]
    nh, n = r_k.shape
    lw, la = w_lora_up.shape[0], a_lora_up.shape[0]
    taps = conv_w.shape[0]
    in_cols = w_in.shape[1] * N_DEV
    shift_cols = 3 * hw + lw + la
    assert in_cols == shift_cols + hw + 3 * cw and hw == cw and hw % LANE == 0 and LANE % nh == 0
    assert lw <= LORA_PAD and la <= LORA_PAD and taps - 1 <= CONV_HALO and hw % (2 * LORA_PAD) == 0
    q = hw // LANE
    gate_blk, gv_blk, gg_blk, gc_blk = 3, 4, 5, 6
    lo0 = 7 * hw
    lo_blk = lo0 // (2 * LORA_PAD)
    x2, tgt2 = x[0], loss_target[0]
    row = lambda v: v.reshape(1, -1)

    gathered = _all_gather(
        [_cast(w_in, MXU_DTYPE, "cast_w_in"), _cast(w_out, MXU_DTYPE, "cast_w_out"),
         _cast(w_pw2, MXU_DTYPE, "cast_w_pw2"), w_lora_up, a_lora_up, conv_w], "gather_weights")
    w_in_g, w_out_g, w_pw2_g, wup_w_g, wup_a_g, conv_w_g = gathered
    w_full = w_in_g.transpose(1, 0, 2).reshape(d, in_cols)
    c0 = shift_cols
    wp = jnp.concatenate([
        _to_t(w_full[:, 0:hw], nh, n), _to_t(w_full[:, hw:2 * hw], nh, n), _to_t(w_full[:, 2 * hw:3 * hw], nh, n),
        _to_t(w_full[:, c0:c0 + hw], nh, n), w_full[:, c0 + hw:],
        _pad_to(w_full[:, 3 * hw:3 * hw + lw], LORA_PAD, 1), _pad_to(w_full[:, 3 * hw + lw:c0], LORA_PAD, 1)], axis=1)
    w_out_f = w_out_g.reshape(N_DEV * w_out.shape[0], d)
    w_out_p = jnp.concatenate([_to_t(w_out_f[:hw], nh, n, axis=0), w_out_f[hw:]], axis=0)
    w_pw2_f = w_pw2_g.reshape(N_DEV * w_pw2.shape[0], cw)
    wup_w = _pad_to(_to_t(wup_w_g.transpose(1, 0, 2).reshape(lw, hw), nh, n), LORA_PAD, 0)
    wup_a = _pad_to(_to_t(wup_a_g.transpose(1, 0, 2).reshape(la, hw), nh, n), LORA_PAD, 0)
    conv_w_f = conv_w_g.transpose(1, 0, 2).reshape(taps, cw)
    mu_p = row(jnp.concatenate([
        _to_t(mu_shift[0:hw], nh, n), _to_t(mu_shift[hw:2 * hw], nh, n), _to_t(mu_shift[2 * hw:3 * hw], nh, n)]))
    mu_lo = row(jnp.concatenate([
        _pad_to(mu_shift[3 * hw:3 * hw + lw], LORA_PAD, 0), _pad_to(mu_shift[3 * hw + lw:], LORA_PAD, 0)]))
    tvec = lambda v: row(_to_t(v, nh, n))
    w0_t, a0_t, kk_t, ka_t, lg_t, lb_t = tvec(w0), tvec(a0), tvec(k_k), tvec(k_a), tvec(lnx_g), tvec(lnx_b)
    rk_t = row(r_k.T)
    head = jnp.arange(hw, dtype=jnp.int32) % nh
    e = (head[:, None] == jnp.arange(LANE, dtype=jnp.int32)[None, :]).astype(F32)
    et = e.T

    c3 = lambda v: v.reshape(t, q, LANE)
    c2d = lambda v: v.reshape(t * q, LANE)
    me = 4 * lax.axis_index("x") + 2 * lax.axis_index("y") + lax.axis_index("c")
    own = lambda parts: lax.dynamic_index_in_dim(parts, me, 0, keepdims=False)
    wire = lambda parts, nm: _cast(parts.reshape(-1, parts.shape[-1]), WIRE_DTYPE, nm).reshape(parts.shape)

    h = _prenorm(x2, row(norm_pre_g))
    proj = _matmul(h, wp, "nn", F32, "mm_proj")
    r_a, w_a, kh_a, v_a, kn_a, b_a = _rwkv_pre(
        proj, mu_p, mu_lo, w0_t, a0_t, kk_t, ka_t, wup_w, wup_a, e, et, hw, lo_blk)
    y_c, sa_c, ck = _wkv_fwd(c3(w_a), c3(b_a), c3(kh_a), c3(r_a), c3(kn_a), c3(v_a), nh)
    y_a = c2d(y_c)
    y_rwkv = _rwkv_post(y_a, r_a, kh_a, v_a, proj, lg_t, lb_t, rk_t, e, et, hw, n, gate_blk)
    cs = _conv_fwd(proj, conv_w_f, row(conv_b), row(cln_g), row(cln_b), cw, gv_blk, gg_blk)
    c2 = _matmul(cs, w_pw2_f, "nn", F32, "mm_pw2")
    y_conv = _conv_gate(c2, proj, row(b_pw2), cw, gc_blk)
    mix = jnp.concatenate([y_rwkv, y_conv], axis=1)
    out = _matmul(mix, w_out_p, "nn", F32, "mm_out")
    dout, dy, loss_part, d_post_g = _post_loss(out, x2, tgt2, row(norm_post_g))

    dmix = _matmul(dout, w_out_p, "nt", F32, "mm_dmix")
    d_w_out_p = _matmul(mix, dout, "tn", F32, "mm_dw_out")
    dc2, dgc, d_b_pw2 = _conv_gate_bwd(dmix, c2, proj, row(b_pw2), cw, 1, gc_blk)
    dcs = _matmul(dc2, w_pw2_f, "nt", F32, "mm_dcs")
    d_w_pw2 = _matmul(cs, dc2, "tn", F32, "mm_dw_pw2")
    dc, d_conv_w, d_conv_b, d_cln_g, d_cln_b = _conv_bwd_norm(
        proj, dcs, conv_w_f, row(conv_b), row(cln_g), row(cln_b), cw, gv_blk, gg_blk)
    dgv, dgg = _conv_bwd_glu(dc, proj, conv_w_f, cw, gv_blk, gg_blk)
    dgr, dy_rec, dr_bon, dkh_bon, dv_bon, d_lg_t, d_lb_t, d_rk_t = _rwkv_post_bwd(
        dmix, y_a, r_a, kh_a, v_a, proj, lg_t, lb_t, rk_t, e, et, hw, n, gate_blk)
    ck_i = ck.reshape(ck.shape[0], n, n, nh).transpose(0, 2, 1, 3).reshape(ck.shape)
    d_w_out_f = jnp.concatenate([_from_t(d_w_out_p[:hw], nh, n, axis=0), d_w_out_p[hw:]], axis=0)
    d_w_out_parts = d_w_out_f.reshape((N_DEV,) + w_out.shape)
    d_w_pw2_parts = d_w_pw2.reshape((N_DEV,) + w_pw2.shape)
    dv_c, dr_c, dw_c, db_c, dk_c, dkn_c, recv_w_out, recv_w_pw2 = _wkv_bwd(
        c3(r_a), c3(w_a), c3(b_a), c3(kh_a), c3(kn_a), c3(dy_rec), sa_c, c3(v_a), ck_i, nh,
        exchange=[wire(d_w_out_parts, "wire_w_out"), wire(d_w_pw2_parts, "wire_w_pw2")])
    dxs, dxs_lo, d_mu_p, d_mu_lo, d_w0_t, d_a0_t, d_kk_t, d_ka_t, d_wup_w, d_wup_a = _rwkv_pre_bwd(
        proj, mu_p, mu_lo, w0_t, a0_t, kk_t, ka_t, wup_w, wup_a, e, et, hw, lo_blk,
        c2d(dr_c), c2d(dw_c), c2d(dk_c), c2d(dv_c), c2d(dkn_c), c2d(db_c), dr_bon, dkh_bon, dv_bon)
    dshift = _shift_bwd(dxs, mu_p, "shift_bwd")
    dshift_lo = _shift_bwd(dxs_lo, mu_lo, "shift_bwd_lora")
    dproj = jnp.concatenate([dshift, dgr, dgv, dgg, dgc, dshift_lo], axis=1)
    colparts = lambda v: v.reshape(v.shape[0], N_DEV, v.shape[1] // N_DEV).transpose(1, 0, 2)
    d_wup_w_parts = colparts(_from_t(d_wup_w[:lw], nh, n))
    d_wup_a_parts = colparts(_from_t(d_wup_a[:la], nh, n))
    d_conv_w_parts = colparts(d_conv_w)
    d_wp, recv_wup_w, recv_wup_a, recv_conv_w = _matmul(
        h, dproj, "tn", F32, "mm_dw_in", exchange=[d_wup_w_parts, d_wup_a_parts, d_conv_w_parts])

    d_w_full = jnp.concatenate([
        _from_t(d_wp[:, 0:hw], nh, n), _from_t(d_wp[:, hw:2 * hw], nh, n), _from_t(d_wp[:, 2 * hw:3 * hw], nh, n),
        d_wp[:, lo0:lo0 + lw], d_wp[:, lo0 + LORA_PAD:lo0 + LORA_PAD + la],
        _from_t(d_wp[:, 3 * hw:4 * hw], nh, n), d_wp[:, 4 * hw:lo0]], axis=1)
    sc = w_in.shape[1]
    by_core = d_w_full.reshape(d, N_CHIPS, 2, sc).transpose(2, 1, 0, 3)
    core = lax.axis_index("c")
    for_mine = lax.dynamic_index_in_dim(by_core, core, 0, keepdims=False)
    for_sibling = lax.dynamic_index_in_dim(by_core, 1 - core, 0, keepdims=False)
    (from_sibling,) = _exchange([wire(for_sibling, "wire_w_in")], "exchange_sibling", "sibling")
    chip_f32, chip_wire = _chip_sum(for_mine, from_sibling)
    dh, recv_w_in = _matmul(dproj, wp, "nt", F32, "mm_dh", exchange=[chip_wire], pattern="chips")
    own_w_in = lax.dynamic_index_in_dim(chip_f32, 2 * lax.axis_index("x") + lax.axis_index("y"), 0, keepdims=False)
    grad_x, d_pre_g = _prenorm_bwd(dh, x2, dy, row(norm_pre_g))
    d_mu = jnp.concatenate([
        _from_t(d_mu_p[0, 0:hw], nh, n), _from_t(d_mu_p[0, hw:2 * hw], nh, n), _from_t(d_mu_p[0, 2 * hw:3 * hw], nh, n),
        d_mu_lo[0, 0:lw], d_mu_lo[0, LORA_PAD:LORA_PAD + la]])
    ft = lambda v: _from_t(v[0], nh, n)
    small = {
        "norm_pre_g": d_pre_g[0], "mu_shift": d_mu, "w0": ft(d_w0_t), "a0": ft(d_a0_t), "k_k": ft(d_kk_t),
        "k_a": ft(d_ka_t), "r_k": d_rk_t[0].reshape(n, nh).T.reshape(-1), "lnx_g": ft(d_lg_t), "lnx_b": ft(d_lb_t),
        "conv_b": d_conv_b[0], "cln_g": d_cln_g[0], "cln_b": d_cln_b[0], "b_pw2": d_b_pw2[0],
        "norm_post_g": d_post_g[0]}
    small_names = list(small)
    packed = jnp.concatenate([small[k] for k in small_names] + [loss_part[0, 0:1]])
    plen = packed.shape[0]
    ppad = -(-plen // LANE) * LANE
    packed = _pad_to(packed, ppad, 0).reshape(1, ppad)

    (packed_all,) = _all_gather([packed], "gather_small")

    res = {}
    sharded = [("w_in", own_w_in, recv_w_in), ("w_out", own(d_w_out_parts), recv_w_out),
               ("w_pw2", own(d_w_pw2_parts), recv_w_pw2), ("w_lora_up", own(d_wup_w_parts), recv_wup_w),
               ("a_lora_up", own(d_wup_a_parts), recv_wup_a), ("conv_w", own(d_conv_w_parts), recv_conv_w)]
    for nm, mine, rc in sharded:
        res[nm] = _adamw(args[nm], args["m_" + nm], args["v_" + nm], rc, mine, "adamw_" + nm)
    w_small = _pad_to(jnp.concatenate([args[k].reshape(-1) for k in small_names]), ppad, 0).reshape(1, ppad)
    m_small = _pad_to(jnp.concatenate([args["m_" + k].reshape(-1) for k in small_names]), ppad, 0).reshape(1, ppad)
    v_small = _pad_to(jnp.concatenate([args["v_" + k].reshape(-1) for k in small_names]), ppad, 0).reshape(1, ppad)
    g_s, d_s, m_s, v_s = _adamw(w_small, m_small, v_small, packed_all, None, "adamw_small")
    off = 0
    for k in small_names:
        size = args[k].size
        res[k] = tuple(o[0, off:off + size].reshape(args[k].shape) for o in (g_s, d_s, m_s, v_s))
        off += size
    loss = g_s[0, plen - 1]

    order = ["norm_pre_g", "w_in", "mu_shift", "w0", "w_lora_up", "a0", "a_lora_up", "k_k", "k_a", "r_k",
             "lnx_g", "lnx_b", "conv_w", "conv_b", "cln_g", "cln_b", "w_pw2", "b_pw2", "w_out", "norm_post_g"]
    outs = [loss, grad_x[None]]
    for slot in range(4):
        outs += [res[k][slot] for k in order]
    return tuple(outs)
```

```python
import functools

import jax
import jax.numpy as jnp
from jax import lax
from jax.experimental import pallas as pl
from jax.experimental.pallas import tpu as pltpu

F32 = jnp.float32
MXU_DTYPE = jnp.bfloat16
WIRE_DTYPE = jnp.bfloat16
HI = lax.Precision.HIGHEST

NORM_EPS = 1e-6
LN_EPS = 1e-5
GN_EPS_PER_CHANNEL = 1e-5
KK_EPS = 1e-12
ADAM_LR = 0.001
ADAM_B1 = 0.9
ADAM_B2 = 0.999
ADAM_EPS = 1e-08
ADAM_WD = 0.01
ADAM_STEP = 10

LANE = 128
SUBLANE = 8
LORA_PAD = 128
CONV_HALO = 32
N_DEV = 8
VMEM_LIMIT = 56 * 1024 * 1024
WKV_CHUNK = 16
RWKV_ROWS = 64
MESH = pl.DeviceIdType.MESH


def _tile(n, target, mult):
    if n <= target:
        return n
    best = None
    for d in range(mult, target + 1, mult):
        if n % d == 0:
            best = d
    assert best is not None, (n, target, mult)
    return best


def _cparams(sem=None):
    return pltpu.CompilerParams(dimension_semantics=sem, vmem_limit_bytes=VMEM_LIMIT)


def _sigmoid(x):
    return 1.0 / (1.0 + jnp.exp(-x))


def _full(shape):
    nd = len(shape)
    return pl.BlockSpec(shape, lambda *_: (0,) * nd)


def _compact_spec(tt, q):
    return pl.BlockSpec((tt * q, LANE), lambda i: (i, 0))


def _load_compact(ref, tt, q):
    return jnp.concatenate([ref[pl.ds(p, tt, stride=q), :] for p in range(q)], axis=1)


def _store_compact(ref, val, tt, q):
    for p in range(q):
        ref[pl.ds(p, tt, stride=q), :] = val[:, p * LANE:(p + 1) * LANE]


def _matmul(a, b, mode, out_dtype, name, exchange=(), pattern="direct"):
    if mode == "nn":
        (m, k), (k2, n) = a.shape, b.shape
    elif mode == "nt":
        (m, k), (n, k2) = a.shape, b.shape
    else:
        (k, m), (k2, n) = a.shape, b.shape
    assert k == k2, (a.shape, b.shape, mode)
    tm, tn, tk = _tile(m, 1024, LANE), _tile(n, 768, LANE), _tile(k, 2048, LANE)
    nk = k // tk
    ne = len(exchange)
    grid = (m // tm, n // tn, nk)
    if mode == "nn":
        a_spec = pl.BlockSpec((tm, tk), lambda i, j, kk: (i, kk))
        b_spec = pl.BlockSpec((tk, tn), lambda i, j, kk: (kk, j))
        dims = (((1,), (0,)), ((), ()))
    elif mode == "nt":
        a_spec = pl.BlockSpec((tm, tk), lambda i, j, kk: (i, kk))
        b_spec = pl.BlockSpec((tn, tk), lambda i, j, kk: (j, kk))
        dims = (((1,), (1,)), ((), ()))
    else:
        a_spec = pl.BlockSpec((tk, tm), lambda i, j, kk: (kk, i))
        b_spec = pl.BlockSpec((tk, tn), lambda i, j, kk: (kk, j))
        dims = (((0,), (0,)), ((), ()))

    def body(*refs):
        a_ref, b_ref = refs[:2]
        parts = refs[2:2 + ne]
        o_ref = refs[2 + ne]
        recvs = refs[3 + ne:3 + 2 * ne]
        acc_ref = refs[3 + 2 * ne]
        i, j, kk = pl.program_id(0), pl.program_id(1), pl.program_id(2)
        if ne:
            copies = _exchange_copies(parts, recvs, *refs[4 + 2 * ne:], pattern=pattern)

            @pl.when((i == 0) & (j == 0) & (kk == 0))
            def _():
                for cp in copies:
                    cp.start()

        @pl.when(kk == 0)
        def _():
            acc_ref[...] = jnp.zeros_like(acc_ref)

        acc_ref[...] += lax.dot_general(a_ref[...], b_ref[...], dims, preferred_element_type=F32)

        @pl.when(kk == nk - 1)
        def _():
            o_ref[...] = acc_ref[...].astype(o_ref.dtype)

        if ne:
            @pl.when((i == grid[0] - 1) & (j == grid[1] - 1) & (kk == nk - 1))
            def _():
                for cp in copies:
                    cp.wait()

    res = pl.pallas_call(
        body, name=name,
        grid=grid,
        in_specs=[a_spec, b_spec] + [HBM_SPEC] * ne,
        out_specs=[pl.BlockSpec((tm, tn), lambda i, j, kk: (i, j))] + [HBM_SPEC] * ne,
        out_shape=[jax.ShapeDtypeStruct((m, n), out_dtype)] + _exchange_shapes(exchange, pattern),
        scratch_shapes=[pltpu.VMEM((tm, tn), F32)] + _exchange_sems(ne, pattern),
        compiler_params=_cparams(("arbitrary",) * 3 if ne else ("parallel", "parallel", "arbitrary")),
    )(a, b, *exchange)
    return res if ne else res[0]


def _prenorm(x, g):
    t, d = x.shape
    tt = _tile(t, 256, SUBLANE)

    def body(x_ref, g_ref, h_ref):
        xv = x_ref[...]
        rinv = lax.rsqrt(jnp.mean(xv * xv, axis=-1, keepdims=True) + NORM_EPS)
        h_ref[...] = (xv * rinv * g_ref[...]).astype(h_ref.dtype)

    return pl.pallas_call(
        body, name="prenorm", grid=(t // tt,),
        in_specs=[pl.BlockSpec((tt, d), lambda i: (i, 0)), _full((1, d))],
        out_specs=pl.BlockSpec((tt, d), lambda i: (i, 0)),
        out_shape=jax.ShapeDtypeStruct((t, d), MXU_DTYPE),
        compiler_params=_cparams(("parallel",)),
    )(x, g)


def _post_loss(out, x, target, g):
    t, d = out.shape
    tt = _tile(t, 128, SUBLANE)

    def body(o_ref, x_ref, t_ref, g_ref, dout_ref, dy_ref, loss_ref, dg_ref):
        i = pl.program_id(0)
        ov = o_ref[...]
        rinv = lax.rsqrt(jnp.mean(ov * ov, axis=-1, keepdims=True) + NORM_EPS)
        nv = ov * rinv
        gv = g_ref[...]
        err = x_ref[...] + nv * gv - t_ref[...]
        part = 0.5 * jnp.sum(jnp.mean(err * err, axis=-1, keepdims=True), axis=0, keepdims=True)
        dy = err * (1.0 / d)
        dy_ref[...] = dy
        dn = dy * gv
        dout = rinv * (dn - nv * jnp.mean(dn * nv, axis=-1, keepdims=True))
        dout_ref[...] = dout.astype(dout_ref.dtype)
        dg = jnp.sum(dy * nv, axis=0, keepdims=True)

        @pl.when(i == 0)
        def _():
            loss_ref[...] = jnp.zeros_like(loss_ref)
            dg_ref[...] = jnp.zeros_like(dg_ref)

        loss_ref[...] += jnp.broadcast_to(part, loss_ref.shape)
        dg_ref[...] += dg

    row = pl.BlockSpec((tt, d), lambda i: (i, 0))
    return pl.pallas_call(
        body, name="post_loss", grid=(t // tt,),
        in_specs=[row, row, row, _full((1, d))],
        out_specs=[row, row, _full((1, LANE)), _full((1, d))],
        out_shape=[jax.ShapeDtypeStruct((t, d), MXU_DTYPE), jax.ShapeDtypeStruct((t, d), F32),
                   jax.ShapeDtypeStruct((1, LANE), F32), jax.ShapeDtypeStruct((1, d), F32)],
        compiler_params=_cparams(("arbitrary",)),
    )(out, x, target, g)


def _prenorm_bwd(dh, x, dy, g):
    t, d = x.shape
    tt = _tile(t, 128, SUBLANE)

    def body(dh_ref, x_ref, dy_ref, g_ref, gx_ref, dg_ref):
        i = pl.program_id(0)
        xv = x_ref[...]
        rinv = lax.rsqrt(jnp.mean(xv * xv, axis=-1, keepdims=True) + NORM_EPS)
        nx = xv * rinv
        dhv = dh_ref[...]
        dnx = dhv * g_ref[...]
        dx = rinv * (dnx - nx * jnp.mean(dnx * nx, axis=-1, keepdims=True))
        gx_ref[...] = dy_ref[...] + dx

        @pl.when(i == 0)
        def _():
            dg_ref[...] = jnp.zeros_like(dg_ref)

        dg_ref[...] += jnp.sum(dhv * nx, axis=0, keepdims=True)

    row = pl.BlockSpec((tt, d), lambda i: (i, 0))
    return pl.pallas_call(
        body, name="prenorm_bwd", grid=(t // tt,),
        in_specs=[row, row, row, _full((1, d))],
        out_specs=[row, _full((1, d))],
        out_shape=[jax.ShapeDtypeStruct((t, d), F32), jax.ShapeDtypeStruct((1, d), F32)],
        compiler_params=_cparams(("arbitrary",)),
    )(dh, x, dy, g)


def _segsum(v, nh):
    q = v.shape[1] // LANE
    s = v[:, 0:LANE]
    for p in range(1, q):
        s = s + v[:, p * LANE:(p + 1) * LANE]
    shift = nh
    while shift < LANE:
        s = s + pltpu.roll(s, shift, 1)
        shift *= 2
    return jnp.concatenate([s] * q, axis=1)


def _shifted(cur_ref, prev_ref, lo, hi, first):
    cur = cur_ref[:, lo:hi]
    last = jnp.where(first, 0.0, prev_ref[SUBLANE - 1:SUBLANE, lo:hi])
    prev = pltpu.roll(cur, 1, 0)
    rows = lax.broadcasted_iota(jnp.int32, cur.shape, 0)
    return cur, jnp.where(rows == 0, last, prev)


def _rwkv_mix(main_ref, mainp_ref, lo_ref, lop_ref, mu_ref, mulo_ref, w0_ref, a0_ref, kk_ref, ka_ref,
              wupw_ref, wupa_ref, hw, nh, first):
    def xs(cur_ref, prev_ref, m_ref, lo, hi):
        cur, prev = _shifted(cur_ref, prev_ref, lo, hi, first)
        return cur + (prev - cur) * m_ref[:, lo:hi], prev - cur

    out = {}
    out["r"], out["r_d"] = xs(main_ref, mainp_ref, mu_ref, 0, hw)
    out["k"], out["k_d"] = xs(main_ref, mainp_ref, mu_ref, hw, 2 * hw)
    out["v"], out["v_d"] = xs(main_ref, mainp_ref, mu_ref, 2 * hw, 3 * hw)
    out["wl"], out["wl_d"] = xs(lo_ref, lop_ref, mulo_ref, 0, LORA_PAD)
    out["al"], out["al_d"] = xs(lo_ref, lop_ref, mulo_ref, LORA_PAD, 2 * LORA_PAD)
    th = jnp.tanh(out["wl"])
    zw = w0_ref[...] + jnp.dot(th, wupw_ref[...], preferred_element_type=F32, precision=HI)
    u = -zw
    softplus = jnp.maximum(u, 0.0) + jnp.log(1.0 + jnp.exp(-jnp.abs(u)))
    wlog = -softplus - 0.5
    ew = jnp.exp(wlog)
    za = a0_ref[...] + jnp.dot(out["al"], wupa_ref[...], preferred_element_type=F32, precision=HI)
    a = _sigmoid(za)
    kkr = out["k"] * kk_ref[...]
    nr = jnp.sqrt(_segsum(kkr * kkr, nh))
    nrm = jnp.maximum(nr, KK_EPS)
    out.update(th=th, zw=zw, ew=ew, decay=jnp.exp(-ew), a=a, kkr=kkr, nr=nr, nrm=nrm, kk=kkr / nrm)
    out["kh"] = out["k"] * (1.0 + (a - 1.0) * ka_ref[...])
    return out


def _mix_specs(tt, hw, lo_blk):
    mw, lw2 = 3 * hw, 2 * LORA_PAD
    before = lambda i: jnp.maximum(i * (tt // SUBLANE) - 1, 0)
    vec = _full((1, hw))
    return [pl.BlockSpec((tt, mw), lambda i: (i, 0)), pl.BlockSpec((SUBLANE, mw), lambda i: (before(i), 0)),
            pl.BlockSpec((tt, lw2), lambda i: (i, lo_blk)), pl.BlockSpec((SUBLANE, lw2), lambda i: (before(i), lo_blk)),
            _full((1, mw)), _full((1, lw2)), vec, vec, vec, vec, _full((LORA_PAD, hw)), _full((LORA_PAD, hw))]


def _rwkv_pre(proj, mu, mu_lo, w0, a0, k_k, k_a, wup_w, wup_a, hw, nh, lo_blk):
    t = proj.shape[0]
    tt = _tile(t, RWKV_ROWS, SUBLANE)

    def body(*refs):
        r_o, w_o, kh_o, v_o, kn_o, b_o = refs[-6:]
        f = _rwkv_mix(*refs[:-6], hw, nh, pl.program_id(0) == 0)
        for ref, val in ((r_o, f["r"]), (w_o, f["decay"]), (kh_o, f["kh"]), (v_o, f["v"]), (kn_o, -f["kk"]),
                         (b_o, f["kk"] * f["a"])):
            _store_compact(ref, val, tt, q)

    q = hw // LANE
    return pl.pallas_call(
        body, name="rwkv_pre", grid=(t // tt,),
        in_specs=_mix_specs(tt, hw, lo_blk),
        out_specs=[_compact_spec(tt, q)] * 6,
        out_shape=[jax.ShapeDtypeStruct((t * q, LANE), F32)] * 6,
        compiler_params=_cparams(("parallel",)),
    )(proj, proj, proj, proj, mu, mu_lo, w0, a0, k_k, k_a, wup_w, wup_a)


def _rwkv_post_math(y, r, kh, v, g, lnx_g, lnx_b, r_k, nh, n):
    mean = _segsum(y, nh) * (1.0 / n)
    yc = y - mean
    var = _segsum(yc * yc, nh) * (1.0 / n)
    rstd = lax.rsqrt(var + GN_EPS_PER_CHANNEL * n)
    yn = yc * rstd
    s = _segsum(r * kh * r_k, nh)
    y3 = yn * lnx_g + lnx_b + s * v
    sg = _sigmoid(g)
    return yn, rstd, s, y3, sg


def _rwkv_post(y, r, kh, v, proj, lnx_g, lnx_b, r_k, hw, n, gate_blk):
    t = proj.shape[0]
    tt = _tile(t, RWKV_ROWS, SUBLANE)
    q = hw // LANE
    nh = hw // n

    def body(y_ref, r_ref, kh_ref, v_ref, g_ref, lg_ref, lb_ref, rk_ref, o_ref):
        g = g_ref[...]
        y, r, kh, v = (_load_compact(ref, tt, q) for ref in (y_ref, r_ref, kh_ref, v_ref))
        _, _, _, y3, sg = _rwkv_post_math(y, r, kh, v, g, lg_ref[...], lb_ref[...], rk_ref[...], nh, n)
        o_ref[...] = (y3 * (g * sg)).astype(o_ref.dtype)

    row = pl.BlockSpec((tt, hw), lambda i: (i, 0))
    comp = _compact_spec(tt, q)
    vec = _full((1, hw))
    return pl.pallas_call(
        body, name="rwkv_post", grid=(t // tt,),
        in_specs=[comp, comp, comp, comp, pl.BlockSpec((tt, hw), lambda i: (i, gate_blk)), vec, vec, vec],
        out_specs=row,
        out_shape=jax.ShapeDtypeStruct((t, hw), MXU_DTYPE),
        compiler_params=_cparams(("parallel",)),
    )(y, r, kh, v, proj, lnx_g, lnx_b, r_k)


def _rwkv_post_bwd(dmix, y, r, kh, v, proj, lnx_g, lnx_b, r_k, hw, n, gate_blk):
    t = proj.shape[0]
    tt = _tile(t, RWKV_ROWS, SUBLANE)
    q = hw // LANE
    nh = hw // n

    def body(dm_ref, y_ref, r_ref, kh_ref, v_ref, g_ref, lg_ref, lb_ref, rk_ref,
             dg_o, dy_o, dr_o, dkh_o, dv_o, dlg_o, dlb_o, drk_o):
        i = pl.program_id(0)
        g, rk, lg = g_ref[...], rk_ref[...], lg_ref[...]
        y, r, kh, v = (_load_compact(ref, tt, q) for ref in (y_ref, r_ref, kh_ref, v_ref))
        yn, rstd, s, y3, sg = _rwkv_post_math(y, r, kh, v, g, lg, lb_ref[...], rk, nh, n)
        dyr = dm_ref[...]
        dy3 = dyr * (g * sg)
        dg_o[...] = (dyr * y3 * (sg * (1.0 + g * (1.0 - sg)))).astype(dg_o.dtype)
        ds = _segsum(dy3 * v, nh)
        _store_compact(dv_o, dy3 * s, tt, q)
        _store_compact(dr_o, ds * kh * rk, tt, q)
        _store_compact(dkh_o, ds * r * rk, tt, q)
        dyn = dy3 * lg
        m1 = _segsum(dyn, nh) * (1.0 / n)
        m2 = _segsum(dyn * yn, nh) * (1.0 / n)
        _store_compact(dy_o, rstd * (dyn - m1 - yn * m2), tt, q)

        @pl.when(i == 0)
        def _():
            dlg_o[...] = jnp.zeros_like(dlg_o)
            dlb_o[...] = jnp.zeros_like(dlb_o)
            drk_o[...] = jnp.zeros_like(drk_o)

        dlg_o[...] += jnp.sum(dy3 * yn, axis=0, keepdims=True)
        dlb_o[...] += jnp.sum(dy3, axis=0, keepdims=True)
        drk_o[...] += jnp.sum(ds * r * kh, axis=0, keepdims=True)

    row = pl.BlockSpec((tt, hw), lambda i: (i, 0))
    comp = _compact_spec(tt, q)
    vec = _full((1, hw))
    rowf = jax.ShapeDtypeStruct((t * q, LANE), F32)
    vecf = jax.ShapeDtypeStruct((1, hw), F32)
    return pl.pallas_call(
        body, name="rwkv_post_bwd", grid=(t // tt,),
        in_specs=[row, comp, comp, comp, comp, pl.BlockSpec((tt, hw), lambda i: (i, gate_blk)), vec, vec, vec],
        out_specs=[row, comp, comp, comp, comp, vec, vec, vec],
        out_shape=[jax.ShapeDtypeStruct((t, hw), MXU_DTYPE), rowf, rowf, rowf, rowf, vecf, vecf, vecf],
        compiler_params=_cparams(("arbitrary",)),
    )(dmix, y, r, kh, v, proj, lnx_g, lnx_b, r_k)


def _rwkv_pre_bwd(proj, mu, mu_lo, w0, a0, k_k, k_a, wup_w, wup_a, hw, nh, lo_blk,
                  dr_rec, dw_rec, dkh_rec, dv_rec, dkn_rec, db_rec, dr_bon, dkh_bon, dv_bon):
    t = proj.shape[0]
    mw, lw2 = 3 * hw, 2 * LORA_PAD
    tt = _tile(t, RWKV_ROWS, SUBLANE)
    q = hw // LANE
    n_in = 12

    def body(*refs):
        mix_refs = refs[:n_in]
        drr, dwr, dkhr, dvr, dknr, db, drb, dkhb, dvb = (
            _load_compact(ref, tt, q) for ref in refs[n_in:n_in + 9])
        dxs_o, dxl_o, dmu_o, dmul_o, dw0_o, da0_o, dkk_o, dka_o, dwupw_o, dwupa_o = refs[n_in + 9:]
        kk_ref, ka_ref, wupw_ref, wupa_ref = mix_refs[8:12]
        i = pl.program_id(0)
        f = _rwkv_mix(*mix_refs, hw, nh, i == 0)
        k, a, kk, nrm = f["k"], f["a"], f["kk"], f["nrm"]
        k_a, k_k = ka_ref[...], kk_ref[...]
        dr = drr + drb
        dkh = dkhr + dkhb
        dv = dvr + dvb
        da = db * kk + dkh * k * k_a
        dkk = db * a - dknr
        dk = dkh * (1.0 + (a - 1.0) * k_a)
        dka = jnp.sum(dkh * k * (a - 1.0), axis=0, keepdims=True)
        proj_kk = _segsum(dkk * kk, nh)
        dkkr = jnp.where(f["nr"] > KK_EPS, (dkk - kk * proj_kk) / nrm, dkk * (1.0 / KK_EPS))
        dk = dk + dkkr * k_k
        dkk_w = jnp.sum(dkkr * k, axis=0, keepdims=True)
        dza = da * a * (1.0 - a)
        dzw = dwr * f["decay"] * (-f["ew"]) * _sigmoid(-f["zw"])
        nt_dims = (((1,), (1,)), ((), ()))
        tn_dims = (((0,), (0,)), ((), ()))
        dal = lax.dot_general(dza, wupa_ref[...], nt_dims, preferred_element_type=F32, precision=HI)
        dth = lax.dot_general(dzw, wupw_ref[...], nt_dims, preferred_element_type=F32, precision=HI)
        dwl = dth * (1.0 - f["th"] * f["th"])
        dxs_o[:, 0:hw] = dr
        dxs_o[:, hw:2 * hw] = dk
        dxs_o[:, 2 * hw:3 * hw] = dv
        dxl_o[:, 0:LORA_PAD] = dwl
        dxl_o[:, LORA_PAD:lw2] = dal

        @pl.when(i == 0)
        def _():
            for ref in (dmu_o, dmul_o, dw0_o, da0_o, dkk_o, dka_o, dwupw_o, dwupa_o):
                ref[...] = jnp.zeros_like(ref)

        def colsum(v):
            return jnp.sum(v, axis=0, keepdims=True)

        dmu_o[:, 0:hw] += colsum(dr * f["r_d"])
        dmu_o[:, hw:2 * hw] += colsum(dk * f["k_d"])
        dmu_o[:, 2 * hw:3 * hw] += colsum(dv * f["v_d"])
        dmul_o[:, 0:LORA_PAD] += colsum(dwl * f["wl_d"])
        dmul_o[:, LORA_PAD:lw2] += colsum(dal * f["al_d"])
        dw0_o[...] += colsum(dzw)
        da0_o[...] += colsum(dza)
        dkk_o[...] += dkk_w
        dka_o[...] += dka
        dwupw_o[...] += lax.dot_general(f["th"], dzw, tn_dims, preferred_element_type=F32, precision=HI)
        dwupa_o[...] += lax.dot_general(f["al"], dza, tn_dims, preferred_element_type=F32, precision=HI)

    vec = _full((1, hw))
    vecf = jax.ShapeDtypeStruct((1, hw), F32)
    return pl.pallas_call(
        body, name="rwkv_pre_bwd", grid=(t // tt,),
        in_specs=_mix_specs(tt, hw, lo_blk) + [_compact_spec(tt, q)] * 9,
        out_specs=[pl.BlockSpec((tt, mw), lambda i: (i, 0)), pl.BlockSpec((tt, lw2), lambda i: (i, 0)),
                   _full((1, mw)), _full((1, lw2)), vec, vec, vec, vec,
                   _full((LORA_PAD, hw)), _full((LORA_PAD, hw))],
        out_shape=[jax.ShapeDtypeStruct((t, mw), F32), jax.ShapeDtypeStruct((t, lw2), F32),
                   jax.ShapeDtypeStruct((1, mw), F32), jax.ShapeDtypeStruct((1, lw2), F32),
                   vecf, vecf, vecf, vecf,
                   jax.ShapeDtypeStruct((LORA_PAD, hw), F32), jax.ShapeDtypeStruct((LORA_PAD, hw), F32)],
        compiler_params=_cparams(("arbitrary",)),
    )(proj, proj, proj, proj, mu, mu_lo, w0, a0, k_k, k_a, wup_w, wup_a,
      dr_rec, dw_rec, dkh_rec, dv_rec, dkn_rec, db_rec, dr_bon, dkh_bon, dv_bon)


def _shift_bwd(dxs, mu, name):
    t, sw = dxs.shape
    tt = _tile(t, 256, SUBLANE)
    nblk = t // SUBLANE

    def body(d_ref, nxt_ref, mu_ref, o_ref):
        last = pl.program_id(0) == pl.num_programs(0) - 1
        cur = d_ref[...]
        first_next = jnp.where(last, 0.0, nxt_ref[0:1, :])
        nxt = pltpu.roll(cur, tt - 1, 0)
        rows = lax.broadcasted_iota(jnp.int32, cur.shape, 0)
        nxt = jnp.where(rows == tt - 1, first_next, nxt)
        m = mu_ref[...]
        o_ref[...] = (cur * (1.0 - m) + nxt * m).astype(o_ref.dtype)

    return pl.pallas_call(
        body, name=name, grid=(t // tt,),
        in_specs=[pl.BlockSpec((tt, sw), lambda i: (i, 0)),
                  pl.BlockSpec((SUBLANE, sw), lambda i: (jnp.minimum((i + 1) * (tt // SUBLANE), nblk - 1), 0)),
                  _full((1, sw))],
        out_specs=pl.BlockSpec((tt, sw), lambda i: (i, 0)),
        out_shape=jax.ShapeDtypeStruct((t, sw), MXU_DTYPE),
        compiler_params=_cparams(("parallel",)),
    )(dxs, dxs, mu)


def _tree_sum(parts):
    while len(parts) > 1:
        parts = [parts[p] + parts[p + 1] for p in range(0, len(parts) - 1, 2)] + ([parts[-1]] if len(parts) % 2 else [])
    return parts[0]


def _tile_rows(src_ref, dst_ref, tc, nh):
    def convert(ts, carry):
        _tile_step(src_ref, dst_ref, ts, nh)
        return carry

    lax.fori_loop(0, tc, convert, 0)


def _tile_step(src_ref, dst_ref, ts, nh):
    for grp, m in enumerate(_tiled(src_ref[ts], nh)):
        dst_ref[ts, grp] = m


def _tiled(v, nh):
    rep = LANE // nh
    lane_group = lax.broadcasted_iota(jnp.int32, v.shape, 1) // nh
    rolled = [v] + [pltpu.roll(v, k * nh, 1) for k in range(1, rep)]
    out = []
    for grp in range(rep):
        m = rolled[(0 - grp) % rep]
        for g in range(1, rep):
            m = jnp.where(lane_group == g, rolled[(g - grp) % rep], m)
        out.append(m)
    return out


def _wkv_fwd(w_c, b_c, k_c, r_c, kn_c, v_c, nh):
    t, q, _ = v_c.shape
    rep = LANE // nh
    n = q * rep
    tc = _tile(t, WKV_CHUNK, 1)
    nc = t // tc
    nacc = 4
    nv = 5

    def body(*refs):
        cur, v_ref, nxt = refs[:nv], refs[nv], refs[nv + 1:2 * nv + 1]
        y_ref, sa_ref, ck_ref, s_ref = refs[2 * nv + 1:2 * nv + 5]
        tiles_even, tiles_odd = refs[2 * nv + 5:3 * nv + 5], refs[3 * nv + 5:]
        c = pl.program_id(0)

        @pl.when(c == 0)
        def _():
            s_ref[...] = jnp.zeros_like(s_ref)
            for src, dst in zip(cur, tiles_even):
                _tile_rows(src, dst, tc, nh)

        ck_ref[0] = s_ref[...]

        def row(ref, ts, j):
            return ref[ts, j % rep, pl.ds(j // rep, 1), :]

        def run(mine, ahead_tiles):
            wt, bt, kt, rt, knt = mine

            def step(ts, carry):
                for src, dst in zip(nxt, ahead_tiles):
                    _tile_step(src, dst, ts, nh)
                vt = v_ref[ts]
                acc = [None] * nacc
                for j in range(n):
                    term = s_ref[j] * row(knt, ts, j)
                    acc[j % nacc] = term if acc[j % nacc] is None else acc[j % nacc] + term
                sa = _tree_sum(acc)
                sa_ref[ts] = sa
                acc = [None] * nacc
                for j in range(n):
                    sj = s_ref[j] * row(wt, ts, j) + sa * row(bt, ts, j) + vt * row(kt, ts, j)
                    s_ref[j] = sj
                    term = sj * row(rt, ts, j)
                    acc[j % nacc] = term if acc[j % nacc] is None else acc[j % nacc] + term
                y_ref[ts] = _tree_sum(acc)
                return carry

            lax.fori_loop(0, tc, step, 0)

        @pl.when(c % 2 == 0)
        def _():
            run(tiles_even, tiles_odd)

        @pl.when(c % 2 == 1)
        def _():
            run(tiles_odd, tiles_even)

    comp = pl.BlockSpec((tc, q, LANE), lambda c: (c, 0, 0))
    ahead = pl.BlockSpec((tc, q, LANE), lambda c: (jnp.minimum(c + 1, nc - 1), 0, 0))
    return pl.pallas_call(
        body, name="wkv_fwd", grid=(nc,),
        in_specs=[comp] * (nv + 1) + [ahead] * nv,
        out_specs=[comp, comp, pl.BlockSpec((1, n, q, LANE), lambda c: (c, 0, 0, 0))],
        out_shape=[jax.ShapeDtypeStruct((t, q, LANE), F32), jax.ShapeDtypeStruct((t, q, LANE), F32),
                   jax.ShapeDtypeStruct((nc, n, q, LANE), F32)],
        scratch_shapes=[pltpu.VMEM((n, q, LANE), F32)] + [pltpu.VMEM((tc, rep, q, LANE), F32)] * (2 * nv),
        compiler_params=_cparams(("arbitrary",)),
    )(w_c, b_c, k_c, r_c, kn_c, v_c, w_c, b_c, k_c, r_c, kn_c)


def _wkv_bwd(r_c, w_c, b_c, k_c, kn_c, dy_c, sa_c, v_c, ck_i, nh, exchange=()):
    t, q, _ = dy_c.shape
    rep = LANE // nh
    n = q * rep
    tc = _tile(t, WKV_CHUNK, 1)
    nc = t // tc
    nacc = 2
    ne = len(exchange)
    nv = 8
    n_in = 2 * nv + 1 + ne

    def body(*refs):
        cur, ck_ref, nxt = refs[:nv], refs[nv], refs[nv + 1:2 * nv + 1]
        rc_ref, wc_ref, bc_ref, kc_ref, knc_ref, dyc_ref = cur[:6]
        parts = refs[2 * nv + 1:n_in]
        dv_o, dr_o, dw_o, db_o, dk_o, dkn_o = refs[n_in:n_in + 6]
        recvs = refs[n_in + 6:n_in + 6 + ne]
        hist, g_ref, gp_ref, dsat_ref = refs[n_in + 6 + ne:n_in + 10 + ne]
        tiles_even = refs[n_in + 10 + ne:n_in + 10 + ne + nv]
        tiles_odd = refs[n_in + 10 + ne + nv:n_in + 10 + ne + 2 * nv]
        c = pl.program_id(0)
        if ne:
            copies = _exchange_copies(parts, recvs, *refs[n_in + 10 + ne + 2 * nv:])

            @pl.when(c == 0)
            def _():
                for cp in copies:
                    cp.start()

        @pl.when(c == 0)
        def _():
            g_ref[...] = jnp.zeros_like(g_ref)
            gp_ref[...] = jnp.zeros_like(gp_ref)
            for src, dst in zip(cur, tiles_even):
                _tile_rows(src, dst, tc, nh)

        def row(ref, ts, idx):
            return ref[ts, idx % rep, pl.ds(idx // rep, 1), :]

        hist[0] = ck_ref[0]

        def run(mine, ahead_tiles):
            rt, wt, bt, kt, knt, dyt, sat, vt = mine

            def fstep(ts, carry):
                wv, bv, kv = wc_ref[ts], bc_ref[ts], kc_ref[ts]
                for i in range(n):
                    hist[ts + 1, i] = hist[ts, i] * wv + row(sat, ts, i) * bv + row(vt, ts, i) * kv
                return carry

            lax.fori_loop(0, tc, fstep, 0)

            def bstep(s, carry):
                ts = tc - 1 - s
                for src, dst in zip(nxt, ahead_tiles):
                    _tile_step(src, dst, ts, nh)
                dy = dyc_ref[ts]
                acc_sa, acc_v = [None] * nacc, [None] * nacc
                for j in range(n):
                    gj = g_ref[j] + dy * row(rt, ts, j)
                    g_ref[j] = gj
                    t1 = gj * row(bt, ts, j)
                    t2 = gj * row(kt, ts, j)
                    a = j % nacc
                    acc_sa[a] = t1 if acc_sa[a] is None else acc_sa[a] + t1
                    acc_v[a] = t2 if acc_v[a] is None else acc_v[a] + t2
                dsa = _tree_sum(acc_sa)
                dv_o[ts] = _tree_sum(acc_v)
                for j in range(n):
                    g_ref[j] = g_ref[j] * row(wt, ts, j) + dsa * row(knt, ts, j)
                for grp, m in enumerate(_tiled(dsa, nh)):
                    dsat_ref[grp] = m
                rv, wv, knv = rc_ref[ts], wc_ref[ts], knc_ref[ts]
                names = ("dr", "dw", "db", "dk", "dkn")
                accs = {nm: [None] * nacc for nm in names}
                for i in range(n):
                    dsai = dsat_ref[i % rep, pl.ds(i // rep, 1), :]
                    dyi = row(dyt, ts, i)
                    s_prev = hist[ts, i]
                    gi = gp_ref[i] + dyi * rv
                    terms = {"dr": hist[ts + 1, i] * dyi, "dw": gi * s_prev, "db": gi * row(sat, ts, i),
                             "dk": gi * row(vt, ts, i), "dkn": dsai * s_prev}
                    a = i % nacc
                    for nm in names:
                        accs[nm][a] = terms[nm] if accs[nm][a] is None else accs[nm][a] + terms[nm]
                    gp_ref[i] = gi * wv + dsai * knv
                dr_o[ts] = _tree_sum(accs["dr"])
                dw_o[ts] = _tree_sum(accs["dw"])
                db_o[ts] = _tree_sum(accs["db"])
                dk_o[ts] = _tree_sum(accs["dk"])
                dkn_o[ts] = _tree_sum(accs["dkn"])
                return carry

            lax.fori_loop(0, tc, bstep, 0)

        @pl.when(c % 2 == 0)
        def _():
            run(tiles_even, tiles_odd)

        @pl.when(c % 2 == 1)
        def _():
            run(tiles_odd, tiles_even)

        if ne:
            @pl.when(c == nc - 1)
            def _():
                for cp in copies:
                    cp.wait()

    comp = pl.BlockSpec((tc, q, LANE), lambda c: (nc - 1 - c, 0, 0))
    ahead = pl.BlockSpec((tc, q, LANE), lambda c: (jnp.maximum(nc - 2 - c, 0), 0, 0))
    outc = jax.ShapeDtypeStruct((t, q, LANE), F32)
    vectors = (r_c, w_c, b_c, k_c, kn_c, dy_c, sa_c, v_c)
    return pl.pallas_call(
        body, name="wkv_bwd", grid=(nc,),
        in_specs=[comp] * nv + [pl.BlockSpec((1, n, q, LANE), lambda c: (nc - 1 - c, 0, 0, 0))] + [ahead] * nv
        + [HBM_SPEC] * ne,
        out_specs=[comp] * 6 + [HBM_SPEC] * ne,
        out_shape=[outc] * 6 + _exchange_shapes(exchange),
        scratch_shapes=[pltpu.VMEM((tc + 1, n, q, LANE), F32), pltpu.VMEM((n, q, LANE), F32),
                        pltpu.VMEM((n, q, LANE), F32), pltpu.VMEM((rep, q, LANE), F32)]
        + [pltpu.VMEM((tc, rep, q, LANE), F32)] * (2 * nv) + _exchange_sems(ne),
        compiler_params=_cparams(("arbitrary",)),
    )(*vectors, ck_i, *vectors, *exchange)


def _shift_copies(ext_ref, sh_ref):
    rows = ext_ref.shape[0] - SUBLANE
    for p in range(1, SUBLANE):
        sh_ref[p - 1, 0:rows, :] = ext_ref[p:p + rows, :]


def _window(ext_ref, sh_ref, start, size):
    p = start % SUBLANE
    if p == 0:
        return ext_ref[start:start + size, :]
    assert start - p + size <= ext_ref.shape[0] - SUBLANE
    return sh_ref[p - 1, start - p:start - p + size, :]


def _conv_stage(gv_ref, gg_ref, gvh_ref, ggh_ref, cw_ref, cb_ref, lg_ref, lb_ref, ext_ref, sh_ref, first, tt, taps):
    u = gv_ref[...] * _sigmoid(gg_ref[...])
    uh = jnp.where(first, 0.0, gvh_ref[...] * _sigmoid(ggh_ref[...]))
    ext_ref[0:CONV_HALO, :] = uh
    ext_ref[CONV_HALO:CONV_HALO + tt, :] = u
    _shift_copies(ext_ref, sh_ref)
    off = CONV_HALO - (taps - 1)
    c = cb_ref[...] + _window(ext_ref, sh_ref, off, tt) * cw_ref[0:1, :]
    for j in range(1, taps):
        c = c + _window(ext_ref, sh_ref, off + j, tt) * cw_ref[j:j + 1, :]
    mean = jnp.mean(c, axis=-1, keepdims=True)
    cc = c - mean
    rstd = lax.rsqrt(jnp.mean(cc * cc, axis=-1, keepdims=True) + LN_EPS)
    chat = cc * rstd
    cn = chat * lg_ref[...] + lb_ref[...]
    return chat, rstd, cn


def _conv_specs(t, tt, cw, taps, gv_blk, gg_blk):
    hb = tt // CONV_HALO
    return [pl.BlockSpec((tt, cw), lambda i: (i, gv_blk)), pl.BlockSpec((tt, cw), lambda i: (i, gg_blk)),
            pl.BlockSpec((CONV_HALO, cw), lambda i: (jnp.maximum(i * hb - 1, 0), gv_blk)),
            pl.BlockSpec((CONV_HALO, cw), lambda i: (jnp.maximum(i * hb - 1, 0), gg_blk)),
            _full((taps, cw)), _full((1, cw)), _full((1, cw)), _full((1, cw))]


def _conv_fwd(proj, conv_w, conv_b, cln_g, cln_b, cw, gv_blk, gg_blk):
    t = proj.shape[0]
    taps = conv_w.shape[0]
    tt = _tile(t, 128, CONV_HALO)

    def body(gv_ref, gg_ref, gvh_ref, ggh_ref, cw_ref, cb_ref, lg_ref, lb_ref, o_ref, ext_ref, sh_ref):
        _, _, cn = _conv_stage(gv_ref, gg_ref, gvh_ref, ggh_ref, cw_ref, cb_ref, lg_ref, lb_ref, ext_ref, sh_ref,
                               pl.program_id(0) == 0, tt, taps)
        o_ref[...] = (cn * _sigmoid(cn)).astype(o_ref.dtype)

    return pl.pallas_call(
        body, name="conv_fwd", grid=(t // tt,),
        in_specs=_conv_specs(t, tt, cw, taps, gv_blk, gg_blk),
        out_specs=pl.BlockSpec((tt, cw), lambda i: (i, 0)),
        out_shape=jax.ShapeDtypeStruct((t, cw), MXU_DTYPE),
        scratch_shapes=[pltpu.VMEM((CONV_HALO + tt, cw), F32), pltpu.VMEM((SUBLANE - 1, CONV_HALO + tt, cw), F32)],
        compiler_params=_cparams(("parallel",)),
    )(proj, proj, proj, proj, conv_w, conv_b, cln_g, cln_b)


def _conv_gate(c2, proj, b_pw2, cw, gc_blk):
    t = c2.shape[0]
    tt = _tile(t, 256, SUBLANE)

    def body(c_ref, g_ref, b_ref, o_ref):
        g = g_ref[...]
        o_ref[...] = ((c_ref[...] + b_ref[...]) * (g * _sigmoid(g))).astype(o_ref.dtype)

    return pl.pallas_call(
        body, name="conv_gate", grid=(t // tt,),
        in_specs=[pl.BlockSpec((tt, cw), lambda i: (i, 0)), pl.BlockSpec((tt, cw), lambda i: (i, gc_blk)),
                  _full((1, cw))],
        out_specs=pl.BlockSpec((tt, cw), lambda i: (i, 0)),
        out_shape=jax.ShapeDtypeStruct((t, cw), MXU_DTYPE),
        compiler_params=_cparams(("parallel",)),
    )(c2, proj, b_pw2)


def _conv_gate_bwd(dmix, c2, proj, b_pw2, cw, dm_blk, gc_blk):
    t = c2.shape[0]
    tt = _tile(t, 256, SUBLANE)

    def body(dm_ref, c_ref, g_ref, b_ref, dc2_o, dg_o, db_o):
        g = g_ref[...]
        sg = _sigmoid(g)
        dyc = dm_ref[...]
        dc2 = dyc * (g * sg)
        dc2_o[...] = dc2.astype(dc2_o.dtype)
        dg_o[...] = (dyc * (c_ref[...] + b_ref[...]) * (sg * (1.0 + g * (1.0 - sg)))).astype(dg_o.dtype)

        @pl.when(pl.program_id(0) == 0)
        def _():
            db_o[...] = jnp.zeros_like(db_o)

        db_o[...] += jnp.sum(dc2, axis=0, keepdims=True)

    row = pl.BlockSpec((tt, cw), lambda i: (i, 0))
    return pl.pallas_call(
        body, name="conv_gate_bwd", grid=(t // tt,),
        in_specs=[pl.BlockSpec((tt, cw), lambda i: (i, dm_blk)), row,
                  pl.BlockSpec((tt, cw), lambda i: (i, gc_blk)), _full((1, cw))],
        out_specs=[row, row, _full((1, cw))],
        out_shape=[jax.ShapeDtypeStruct((t, cw), MXU_DTYPE), jax.ShapeDtypeStruct((t, cw), MXU_DTYPE),
                   jax.ShapeDtypeStruct((1, cw), F32)],
        compiler_params=_cparams(("arbitrary",)),
    )(dmix, c2, proj, b_pw2)


def _conv_bwd_norm(proj, dcs, conv_w, conv_b, cln_g, cln_b, cw, gv_blk, gg_blk):
    t = proj.shape[0]
    taps = conv_w.shape[0]
    tt = _tile(t, 128, CONV_HALO)

    def body(gv_ref, gg_ref, gvh_ref, ggh_ref, cw_ref, cb_ref, lg_ref, lb_ref, dcs_ref,
             dc_o, dcw_o, dcb_o, dlg_o, dlb_o, ext_ref, sh_ref):
        chat, rstd, cn = _conv_stage(gv_ref, gg_ref, gvh_ref, ggh_ref, cw_ref, cb_ref, lg_ref, lb_ref, ext_ref,
                                     sh_ref, pl.program_id(0) == 0, tt, taps)
        s = _sigmoid(cn)
        dcn = dcs_ref[...] * (s * (1.0 + cn * (1.0 - s)))
        dchat = dcn * lg_ref[...]
        dc = rstd * (dchat - jnp.mean(dchat, axis=-1, keepdims=True)
                     - chat * jnp.mean(dchat * chat, axis=-1, keepdims=True))
        dc_o[...] = dc

        @pl.when(pl.program_id(0) == 0)
        def _():
            for ref in (dcw_o, dcb_o, dlg_o, dlb_o):
                ref[...] = jnp.zeros_like(ref)

        dlg_o[...] += jnp.sum(dcn * chat, axis=0, keepdims=True)
        dlb_o[...] += jnp.sum(dcn, axis=0, keepdims=True)
        dcb_o[...] += jnp.sum(dc, axis=0, keepdims=True)
        off = CONV_HALO - (taps - 1)
        for j in range(taps):
            dcw_o[j:j + 1, :] += jnp.sum(_window(ext_ref, sh_ref, off + j, tt) * dc, axis=0, keepdims=True)

    vec = _full((1, cw))
    vecf = jax.ShapeDtypeStruct((1, cw), F32)
    return pl.pallas_call(
        body, name="conv_bwd_norm", grid=(t // tt,),
        in_specs=_conv_specs(t, tt, cw, taps, gv_blk, gg_blk) + [pl.BlockSpec((tt, cw), lambda i: (i, 0))],
        out_specs=[pl.BlockSpec((tt, cw), lambda i: (i, 0)), _full((taps, cw)), vec, vec, vec],
        out_shape=[jax.ShapeDtypeStruct((t, cw), F32), jax.ShapeDtypeStruct((taps, cw), F32), vecf, vecf, vecf],
        scratch_shapes=[pltpu.VMEM((CONV_HALO + tt, cw), F32), pltpu.VMEM((SUBLANE - 1, CONV_HALO + tt, cw), F32)],
        compiler_params=_cparams(("arbitrary",)),
    )(proj, proj, proj, proj, conv_w, conv_b, cln_g, cln_b, dcs)


def _conv_bwd_glu(dc, proj, conv_w, cw, gv_blk, gg_blk):
    t = dc.shape[0]
    taps = conv_w.shape[0]
    tt = _tile(t, 128, CONV_HALO)
    hb = tt // CONV_HALO
    nhalo = t // CONV_HALO

    def body(dc_ref, dch_ref, gv_ref, gg_ref, cw_ref, dgv_o, dgg_o, ext_ref, sh_ref):
        last = pl.program_id(0) == pl.num_programs(0) - 1
        ext_ref[0:tt, :] = dc_ref[...]
        ext_ref[tt:tt + CONV_HALO, :] = jnp.where(last, 0.0, dch_ref[...])
        _shift_copies(ext_ref, sh_ref)
        du = _window(ext_ref, sh_ref, taps - 1, tt) * cw_ref[0:1, :]
        for j in range(1, taps):
            du = du + _window(ext_ref, sh_ref, taps - 1 - j, tt) * cw_ref[j:j + 1, :]
        sg = _sigmoid(gg_ref[...])
        dgv_o[...] = (du * sg).astype(dgv_o.dtype)
        dgg_o[...] = (du * gv_ref[...] * sg * (1.0 - sg)).astype(dgg_o.dtype)

    row = pl.BlockSpec((tt, cw), lambda i: (i, 0))
    return pl.pallas_call(
        body, name="conv_bwd_glu", grid=(t // tt,),
        in_specs=[row, pl.BlockSpec((CONV_HALO, cw), lambda i: (jnp.minimum((i + 1) * hb, nhalo - 1), 0)),
                  pl.BlockSpec((tt, cw), lambda i: (i, gv_blk)), pl.BlockSpec((tt, cw), lambda i: (i, gg_blk)),
                  _full((taps, cw))],
        out_specs=[row, row],
        out_shape=[jax.ShapeDtypeStruct((t, cw), MXU_DTYPE)] * 2,
        scratch_shapes=[pltpu.VMEM((tt + CONV_HALO, cw), F32), pltpu.VMEM((SUBLANE - 1, tt + CONV_HALO, cw), F32)],
        compiler_params=_cparams(("parallel",)),
    )(dc, dc, proj, proj, conv_w)


HBM_SPEC = pl.BlockSpec(memory_space=pltpu.HBM)


def _all_gather(shards, name):
    na = len(shards)

    def body(*refs):
        ins, outs = refs[:na], refs[na:2 * na]
        send_sems, recv_sems, local_sems = refs[2 * na:]
        x, y, c = lax.axis_index("x"), lax.axis_index("y"), lax.axis_index("c")
        me, sibling = (x, y, c), (x, y, 1 - c)
        chips = [(1 - x, y), (x, 1 - y), (1 - x, 1 - y)]

        def slot(px, py, pc):
            return 4 * px + 2 * py + pc

        def copy(a, k, block, to, src=None):
            dst = outs[a].at[slot(*block)]
            return pltpu.make_async_remote_copy(
                src_ref=dst if src is None else src, dst_ref=dst,
                send_sem=send_sems.at[a, k], recv_sem=recv_sems.at[a, k],
                device_id=to, device_id_type=MESH)

        mine = [pltpu.make_async_copy(ins[a], outs[a].at[slot(*me)], local_sems.at[a]) for a in range(na)]
        for cp in mine:
            cp.start()
        first = []
        for a in range(na):
            first.append(copy(a, 0, me, sibling, src=ins[a]))
            first += [copy(a, 1 + j, me, (*chip, c), src=ins[a]) for j, chip in enumerate(chips)]
        for cp in first:
            cp.start()
        passed = []
        for j, chip in enumerate(chips):
            for a in range(na):
                copy(a, 1 + j, (*chip, c), me).wait_recv()
                fwd = copy(a, 4 + j, (*chip, c), sibling)
                fwd.start()
                passed.append(fwd)
        for a in range(na):
            copy(a, 0, sibling, me).wait_recv()
            for j, chip in enumerate(chips):
                copy(a, 4 + j, (*chip, 1 - c), me).wait_recv()
        for cp in first + passed:
            cp.wait_send()
        for cp in mine:
            cp.wait()

    return pl.pallas_call(
        body, name=name,
        in_specs=[HBM_SPEC] * na, out_specs=[HBM_SPEC] * na,
        out_shape=[jax.ShapeDtypeStruct((N_DEV,) + s.shape, s.dtype) for s in shards],
        scratch_shapes=[pltpu.SemaphoreType.DMA((na, 7)), pltpu.SemaphoreType.DMA((na, 7)),
                        pltpu.SemaphoreType.DMA((na,))],
        compiler_params=pltpu.CompilerParams(has_side_effects=True),
    )(*shards)


N_CHIPS = N_DEV // 2
EXCHANGE_SLOTS = {"direct": N_DEV - 1, "sibling": N_CHIPS, "chips": N_CHIPS - 1}


def _exchange_copies(ins, outs, send_sems, recv_sems, pattern="direct"):
    x, y, c = lax.axis_index("x"), lax.axis_index("y"), lax.axis_index("c")
    copies = []

    def add(a, src_slot, dst_slot, sem, peer):
        copies.append(pltpu.make_async_remote_copy(
            src_ref=ins[a].at[src_slot], dst_ref=outs[a].at[dst_slot],
            send_sem=send_sems.at[a, sem], recv_sem=recv_sems.at[a, sem],
            device_id=peer, device_id_type=MESH))

    for a in range(len(ins)):
        if pattern == "direct":
            for k in range(1, N_DEV):
                px = 1 - x if k & 4 else x
                py = 1 - y if k & 2 else y
                pc = 1 - c if k & 1 else c
                add(a, 4 * px + 2 * py + pc, k - 1, k - 1, (px, py, pc))
        elif pattern == "sibling":
            for j in range(N_CHIPS):
                add(a, j, j, j, (x, y, 1 - c))
        else:
            for k in range(1, N_CHIPS):
                px = 1 - x if k & 2 else x
                py = 1 - y if k & 1 else y
                add(a, 2 * px + py, k - 1, k - 1, (px, py, c))
    return copies


def _exchange_shapes(parts, pattern="direct"):
    return [jax.ShapeDtypeStruct((EXCHANGE_SLOTS[pattern],) + p.shape[1:], p.dtype) for p in parts]


def _exchange_sems(na, pattern="direct"):
    if not na:
        return []
    return [pltpu.SemaphoreType.DMA((na, EXCHANGE_SLOTS[pattern]))] * 2


def _exchange(parts, name, pattern):
    na = len(parts)

    def body(*refs):
        copies = _exchange_copies(refs[:na], refs[na:2 * na], *refs[2 * na:], pattern=pattern)
        for cp in copies:
            cp.start()
        for cp in copies:
            cp.wait()

    return pl.pallas_call(
        body, name=name,
        in_specs=[HBM_SPEC] * na, out_specs=[HBM_SPEC] * na,
        out_shape=_exchange_shapes(parts, pattern),
        scratch_shapes=_exchange_sems(na, pattern),
        compiler_params=pltpu.CompilerParams(has_side_effects=True),
    )(*parts)


def _chip_sum(mine, theirs):
    nslot, r, c = mine.shape
    rows = nslot * r
    tr = _tile(rows, 256, SUBLANE)

    def body(a_ref, b_ref, f_ref, w_ref):
        s = a_ref[...] + b_ref[...].astype(F32)
        f_ref[...] = s
        w_ref[...] = s.astype(w_ref.dtype)

    blk = pl.BlockSpec((tr, c), lambda i: (i, 0))
    f, w = pl.pallas_call(
        body, name="chip_sum", grid=(rows // tr,),
        in_specs=[blk, blk], out_specs=[blk, blk],
        out_shape=[jax.ShapeDtypeStruct((rows, c), F32), jax.ShapeDtypeStruct((rows, c), theirs.dtype)],
        compiler_params=_cparams(("parallel",)),
    )(mine.reshape(rows, c), theirs.reshape(rows, c))
    return f.reshape(mine.shape), w.reshape(mine.shape)


def _cast(v, dtype, name):
    r, c = v.shape
    tr = _tile(r, 256, SUBLANE)

    def body(i_ref, o_ref):
        o_ref[...] = i_ref[...].astype(o_ref.dtype)

    return pl.pallas_call(
        body, name=name, grid=(r // tr,),
        in_specs=[pl.BlockSpec((tr, c), lambda i: (i, 0))],
        out_specs=pl.BlockSpec((tr, c), lambda i: (i, 0)),
        out_shape=jax.ShapeDtypeStruct((r, c), dtype),
        compiler_params=_cparams(("parallel",)),
    )(v)


def _adamw(w, m, v, recv, own, name):
    r, c = w.shape
    ns = recv.shape[0]
    tr = _tile(r, 128, SUBLANE)
    c1 = 1.0 - ADAM_B1 ** ADAM_STEP
    c2 = 1.0 - ADAM_B2 ** ADAM_STEP

    def body(*refs):
        if own is None:
            w_ref, m_ref, v_ref, rc_ref = refs[:4]
            g = rc_ref[0].astype(F32)
            start = 1
        else:
            w_ref, m_ref, v_ref, rc_ref, own_ref = refs[:5]
            g = own_ref[...]
            start = 0
        g_o, d_o, m_o, v_o = refs[-4:]
        for s in range(start, ns):
            g = g + rc_ref[s].astype(F32)
        mn = ADAM_B1 * m_ref[...] + (1.0 - ADAM_B1) * g
        vn = ADAM_B2 * v_ref[...] + (1.0 - ADAM_B2) * (g * g)
        m_hat = mn / c1
        v_hat = vn / c2
        g_o[...] = g
        d_o[...] = -ADAM_LR * (m_hat / (jnp.sqrt(v_hat) + ADAM_EPS) + ADAM_WD * w_ref[...])
        m_o[...] = mn
        v_o[...] = vn

    row = pl.BlockSpec((tr, c), lambda i: (i, 0))
    ins = [w, m, v, recv] + ([] if own is None else [own])
    in_specs = [row, row, row, pl.BlockSpec((ns, tr, c), lambda i: (0, i, 0))] + ([] if own is None else [row])
    return pl.pallas_call(
        body, name=name, grid=(r // tr,),
        in_specs=in_specs, out_specs=[row] * 4,
        out_shape=[jax.ShapeDtypeStruct((r, c), F32)] * 4,
        compiler_params=_cparams(("parallel",)),
    )(*ins)


def _to_t(v, nh, n, axis=-1):
    v = jnp.moveaxis(v, axis, -1)
    v = v.reshape(v.shape[:-1] + (nh, n)).swapaxes(-1, -2).reshape(v.shape)
    return jnp.moveaxis(v, -1, axis)


def _from_t(v, nh, n, axis=-1):
    v = jnp.moveaxis(v, axis, -1)
    v = v.reshape(v.shape[:-1] + (n, nh)).swapaxes(-1, -2).reshape(v.shape)
    return jnp.moveaxis(v, -1, axis)


def _pad_to(v, size, axis):
    pad = [(0, 0)] * v.ndim
    pad[axis] = (0, size - v.shape[axis])
    return jnp.pad(v, pad)


def kernel(x, norm_pre_g, w_in, mu_shift, w0, w_lora_up, a0, a_lora_up, k_k, k_a, r_k, lnx_g, lnx_b, conv_w, conv_b, cln_g, cln_b, w_pw2, b_pw2, w_out, norm_post_g, loss_target, m_norm_pre_g, m_w_in, m_mu_shift, m_w0, m_w_lora_up, m_a0, m_a_lora_up, m_k_k, m_k_a, m_r_k, m_lnx_g, m_lnx_b, m_conv_w, m_conv_b, m_cln_g, m_cln_b, m_w_pw2, m_b_pw2, m_w_out, m_norm_post_g, v_norm_pre_g, v_w_in, v_mu_shift, v_w0, v_w_lora_up, v_a0, v_a_lora_up, v_k_k, v_k_a, v_r_k, v_lnx_g, v_lnx_b, v_conv_w, v_conv_b, v_cln_g, v_cln_b, v_w_pw2, v_b_pw2, v_w_out, v_norm_post_g):
    args = dict(locals())
    t, d = x.shape[1], x.shape[2]
    hw = w0.shape[0]
    cw = conv_b.shape[0]
    nh, n = r_k.shape
    lw, la = w_lora_up.shape[0], a_lora_up.shape[0]
    taps = conv_w.shape[0]
    in_cols = w_in.shape[1] * N_DEV
    shift_cols = 3 * hw + lw + la
    assert in_cols == shift_cols + hw + 3 * cw and hw == cw and hw % LANE == 0 and LANE % nh == 0
    assert lw <= LORA_PAD and la <= LORA_PAD and taps - 1 <= CONV_HALO and hw % (2 * LORA_PAD) == 0
    q = hw // LANE
    gate_blk, gv_blk, gg_blk, gc_blk = 3, 4, 5, 6
    lo0 = 7 * hw
    lo_blk = lo0 // (2 * LORA_PAD)
    x2, tgt2 = x[0], loss_target[0]
    row = lambda v: v.reshape(1, -1)

    gathered = _all_gather(
        [_cast(w_in, MXU_DTYPE, "cast_w_in"), _cast(w_out, MXU_DTYPE, "cast_w_out"),
         _cast(w_pw2, MXU_DTYPE, "cast_w_pw2"), w_lora_up, a_lora_up, conv_w], "gather_weights")
    w_in_g, w_out_g, w_pw2_g, wup_w_g, wup_a_g, conv_w_g = gathered
    w_full = w_in_g.transpose(1, 0, 2).reshape(d, in_cols)
    c0 = shift_cols
    wp = jnp.concatenate([
        _to_t(w_full[:, 0:hw], nh, n), _to_t(w_full[:, hw:2 * hw], nh, n), _to_t(w_full[:, 2 * hw:3 * hw], nh, n),
        _to_t(w_full[:, c0:c0 + hw], nh, n), w_full[:, c0 + hw:],
        _pad_to(w_full[:, 3 * hw:3 * hw + lw], LORA_PAD, 1), _pad_to(w_full[:, 3 * hw + lw:c0], LORA_PAD, 1)], axis=1)
    w_out_f = w_out_g.reshape(N_DEV * w_out.shape[0], d)
    w_out_p = jnp.concatenate([_to_t(w_out_f[:hw], nh, n, axis=0), w_out_f[hw:]], axis=0)
    w_pw2_f = w_pw2_g.reshape(N_DEV * w_pw2.shape[0], cw)
    wup_w = _pad_to(_to_t(wup_w_g.transpose(1, 0, 2).reshape(lw, hw), nh, n), LORA_PAD, 0)
    wup_a = _pad_to(_to_t(wup_a_g.transpose(1, 0, 2).reshape(la, hw), nh, n), LORA_PAD, 0)
    conv_w_f = conv_w_g.transpose(1, 0, 2).reshape(taps, cw)
    mu_p = row(jnp.concatenate([
        _to_t(mu_shift[0:hw], nh, n), _to_t(mu_shift[hw:2 * hw], nh, n), _to_t(mu_shift[2 * hw:3 * hw], nh, n)]))
    mu_lo = row(jnp.concatenate([
        _pad_to(mu_shift[3 * hw:3 * hw + lw], LORA_PAD, 0), _pad_to(mu_shift[3 * hw + lw:], LORA_PAD, 0)]))
    tvec = lambda v: row(_to_t(v, nh, n))
    w0_t, a0_t, kk_t, ka_t, lg_t, lb_t = tvec(w0), tvec(a0), tvec(k_k), tvec(k_a), tvec(lnx_g), tvec(lnx_b)
    rk_t = row(r_k.T)

    c3 = lambda v: v.reshape(t, q, LANE)
    c2d = lambda v: v.reshape(t * q, LANE)
    me = 4 * lax.axis_index("x") + 2 * lax.axis_index("y") + lax.axis_index("c")
    own = lambda parts: lax.dynamic_index_in_dim(parts, me, 0, keepdims=False)
    wire = lambda parts, nm: _cast(parts.reshape(-1, parts.shape[-1]), WIRE_DTYPE, nm).reshape(parts.shape)

    h = _prenorm(x2, row(norm_pre_g))
    proj = _matmul(h, wp, "nn", F32, "mm_proj")
    r_a, w_a, kh_a, v_a, kn_a, b_a = _rwkv_pre(
        proj, mu_p, mu_lo, w0_t, a0_t, kk_t, ka_t, wup_w, wup_a, hw, nh, lo_blk)
    y_c, sa_c, ck = _wkv_fwd(c3(w_a), c3(b_a), c3(kh_a), c3(r_a), c3(kn_a), c3(v_a), nh)
    y_a = c2d(y_c)
    y_rwkv = _rwkv_post(y_a, r_a, kh_a, v_a, proj, lg_t, lb_t, rk_t, hw, n, gate_blk)
    cs = _conv_fwd(proj, conv_w_f, row(conv_b), row(cln_g), row(cln_b), cw, gv_blk, gg_blk)
    c2 = _matmul(cs, w_pw2_f, "nn", F32, "mm_pw2")
    y_conv = _conv_gate(c2, proj, row(b_pw2), cw, gc_blk)
    mix = jnp.concatenate([y_rwkv, y_conv], axis=1)
    out = _matmul(mix, w_out_p, "nn", F32, "mm_out")
    dout, dy, loss_part, d_post_g = _post_loss(out, x2, tgt2, row(norm_post_g))

    dmix = _matmul(dout, w_out_p, "nt", F32, "mm_dmix")
    d_w_out_p = _matmul(mix, dout, "tn", F32, "mm_dw_out")
    dc2, dgc, d_b_pw2 = _conv_gate_bwd(dmix, c2, proj, row(b_pw2), cw, 1, gc_blk)
    dcs = _matmul(dc2, w_pw2_f, "nt", F32, "mm_dcs")
    d_w_pw2 = _matmul(cs, dc2, "tn", F32, "mm_dw_pw2")
    dc, d_conv_w, d_conv_b, d_cln_g, d_cln_b = _conv_bwd_norm(
        proj, dcs, conv_w_f, row(conv_b), row(cln_g), row(cln_b), cw, gv_blk, gg_blk)
    dgv, dgg = _conv_bwd_glu(dc, proj, conv_w_f, cw, gv_blk, gg_blk)
    dgr, dy_rec, dr_bon, dkh_bon, dv_bon, d_lg_t, d_lb_t, d_rk_t = _rwkv_post_bwd(
        dmix, y_a, r_a, kh_a, v_a, proj, lg_t, lb_t, rk_t, hw, n, gate_blk)
    ck_i = ck.reshape(ck.shape[0], n, n, nh).transpose(0, 2, 1, 3).reshape(ck.shape)
    d_w_out_f = jnp.concatenate([_from_t(d_w_out_p[:hw], nh, n, axis=0), d_w_out_p[hw:]], axis=0)
    d_w_out_parts = d_w_out_f.reshape((N_DEV,) + w_out.shape)
    d_w_pw2_parts = d_w_pw2.reshape((N_DEV,) + w_pw2.shape)
    dv_c, dr_c, dw_c, db_c, dk_c, dkn_c, recv_w_out, recv_w_pw2 = _wkv_bwd(
        c3(r_a), c3(w_a), c3(b_a), c3(kh_a), c3(kn_a), c3(dy_rec), sa_c, c3(v_a), ck_i, nh,
        exchange=[wire(d_w_out_parts, "wire_w_out"), wire(d_w_pw2_parts, "wire_w_pw2")])
    dxs, dxs_lo, d_mu_p, d_mu_lo, d_w0_t, d_a0_t, d_kk_t, d_ka_t, d_wup_w, d_wup_a = _rwkv_pre_bwd(
        proj, mu_p, mu_lo, w0_t, a0_t, kk_t, ka_t, wup_w, wup_a, hw, nh, lo_blk,
        c2d(dr_c), c2d(dw_c), c2d(dk_c), c2d(dv_c), c2d(dkn_c), c2d(db_c), dr_bon, dkh_bon, dv_bon)
    dshift = _shift_bwd(dxs, mu_p, "shift_bwd")
    dshift_lo = _shift_bwd(dxs_lo, mu_lo, "shift_bwd_lora")
    dproj = jnp.concatenate([dshift, dgr, dgv, dgg, dgc, dshift_lo], axis=1)
    colparts = lambda v: v.reshape(v.shape[0], N_DEV, v.shape[1] // N_DEV).transpose(1, 0, 2)
    d_wup_w_parts = colparts(_from_t(d_wup_w[:lw], nh, n))
    d_wup_a_parts = colparts(_from_t(d_wup_a[:la], nh, n))
    d_conv_w_parts = colparts(d_conv_w)
    d_wp, recv_wup_w, recv_wup_a, recv_conv_w = _matmul(
        h, dproj, "tn", F32, "mm_dw_in", exchange=[d_wup_w_parts, d_wup_a_parts, d_conv_w_parts])

    d_w_full = jnp.concatenate([
        _from_t(d_wp[:, 0:hw], nh, n), _from_t(d_wp[:, hw:2 * hw], nh, n), _from_t(d_wp[:, 2 * hw:3 * hw], nh, n),
        d_wp[:, lo0:lo0 + lw], d_wp[:, lo0 + LORA_PAD:lo0 + LORA_PAD + la],
        _from_t(d_wp[:, 3 * hw:4 * hw], nh, n), d_wp[:, 4 * hw:lo0]], axis=1)
    sc = w_in.shape[1]
    by_core = d_w_full.reshape(d, N_CHIPS, 2, sc).transpose(2, 1, 0, 3)
    core = lax.axis_index("c")
    for_mine = lax.dynamic_index_in_dim(by_core, core, 0, keepdims=False)
    for_sibling = lax.dynamic_index_in_dim(by_core, 1 - core, 0, keepdims=False)
    (from_sibling,) = _exchange([wire(for_sibling, "wire_w_in")], "exchange_sibling", "sibling")
    chip_f32, chip_wire = _chip_sum(for_mine, from_sibling)
    dh, recv_w_in = _matmul(dproj, wp, "nt", F32, "mm_dh", exchange=[chip_wire], pattern="chips")
    own_w_in = lax.dynamic_index_in_dim(chip_f32, 2 * lax.axis_index("x") + lax.axis_index("y"), 0, keepdims=False)
    grad_x, d_pre_g = _prenorm_bwd(dh, x2, dy, row(norm_pre_g))
    d_mu = jnp.concatenate([
        _from_t(d_mu_p[0, 0:hw], nh, n), _from_t(d_mu_p[0, hw:2 * hw], nh, n), _from_t(d_mu_p[0, 2 * hw:3 * hw], nh, n),
        d_mu_lo[0, 0:lw], d_mu_lo[0, LORA_PAD:LORA_PAD + la]])
    ft = lambda v: _from_t(v[0], nh, n)
    small = {
        "norm_pre_g": d_pre_g[0], "mu_shift": d_mu, "w0": ft(d_w0_t), "a0": ft(d_a0_t), "k_k": ft(d_kk_t),
        "k_a": ft(d_ka_t), "r_k": d_rk_t[0].reshape(n, nh).T.reshape(-1), "lnx_g": ft(d_lg_t), "lnx_b": ft(d_lb_t),
        "conv_b": d_conv_b[0], "cln_g": d_cln_g[0], "cln_b": d_cln_b[0], "b_pw2": d_b_pw2[0],
        "norm_post_g": d_post_g[0]}
    small_names = list(small)
    packed = jnp.concatenate([small[k] for k in small_names] + [loss_part[0, 0:1]])
    plen = packed.shape[0]
    ppad = -(-plen // LANE) * LANE
    packed = _pad_to(packed, ppad, 0).reshape(1, ppad)

    (packed_all,) = _all_gather([packed], "gather_small")

    res = {}
    sharded = [("w_in", own_w_in, recv_w_in), ("w_out", own(d_w_out_parts), recv_w_out),
               ("w_pw2", own(d_w_pw2_parts), recv_w_pw2), ("w_lora_up", own(d_wup_w_parts), recv_wup_w),
               ("a_lora_up", own(d_wup_a_parts), recv_wup_a), ("conv_w", own(d_conv_w_parts), recv_conv_w)]
    for nm, mine, rc in sharded:
        res[nm] = _adamw(args[nm], args["m_" + nm], args["v_" + nm], rc, mine, "adamw_" + nm)
    w_small = _pad_to(jnp.concatenate([args[k].reshape(-1) for k in small_names]), ppad, 0).reshape(1, ppad)
    m_small = _pad_to(jnp.concatenate([args["m_" + k].reshape(-1) for k in small_names]), ppad, 0).reshape(1, ppad)
    v_small = _pad_to(jnp.concatenate([args["v_" + k].reshape(-1) for k in small_names]), ppad, 0).reshape(1, ppad)
    g_s, d_s, m_s, v_s = _adamw(w_small, m_small, v_small, packed_all, None, "adamw_small")
    off = 0
    for k in small_names:
        size = args[k].size
        res[k] = tuple(o[0, off:off + size].reshape(args[k].shape) for o in (g_s, d_s, m_s, v_s))
        off += size
    loss = g_s[0, plen - 1]

    order = ["norm_pre_g", "w_in", "mu_shift", "w0", "w_lora_up", "a0", "a_lora_up", "k_k", "k_a", "r_k",
             "lnx_g", "lnx_b", "conv_w", "conv_b", "cln_g", "cln_b", "w_pw2", "b_pw2", "w_out", "norm_post_g"]
    outs = [loss, grad_x[None]]
    for slot in range(4):
        outs += [res[k][slot] for k in order]
    return tuple(outs)
```

```python
import functools

import jax
import jax.numpy as jnp
from jax import lax
from jax.experimental import pallas as pl
from jax.experimental.pallas import tpu as pltpu

F32 = jnp.float32
MXU_DTYPE = jnp.bfloat16
WIRE_DTYPE = jnp.bfloat16
HI = lax.Precision.HIGHEST

NORM_EPS = 1e-6
LN_EPS = 1e-5
GN_EPS_PER_CHANNEL = 1e-5
KK_EPS = 1e-12
ADAM_LR = 0.001
ADAM_B1 = 0.9
ADAM_B2 = 0.999
ADAM_EPS = 1e-08
ADAM_WD = 0.01
ADAM_STEP = 10

LANE = 128
SUBLANE = 8
LORA_PAD = 128
CONV_HALO = 32
N_DEV = 8
VMEM_LIMIT = 56 * 1024 * 1024
WKV_CHUNK = 16
RWKV_ROWS = 64
MESH = pl.DeviceIdType.MESH


def _tile(n, target, mult):
    if n <= target:
        return n
    best = None
    for d in range(mult, target + 1, mult):
        if n % d == 0:
            best = d
    assert best is not None, (n, target, mult)
    return best


def _cparams(sem=None):
    return pltpu.CompilerParams(dimension_semantics=sem, vmem_limit_bytes=VMEM_LIMIT)


def _sigmoid(x):
    return 1.0 / (1.0 + jnp.exp(-x))


def _full(shape):
    nd = len(shape)
    return pl.BlockSpec(shape, lambda *_: (0,) * nd)


def _compact_spec(tt, q):
    return pl.BlockSpec((tt * q, LANE), lambda i: (i, 0))


def _load_compact(ref, tt, q):
    return jnp.concatenate([ref[pl.ds(p, tt, stride=q), :] for p in range(q)], axis=1)


def _store_compact(ref, val, tt, q):
    for p in range(q):
        ref[pl.ds(p, tt, stride=q), :] = val[:, p * LANE:(p + 1) * LANE]


def _matmul(a, b, mode, out_dtype, name, exchange=(), pattern="direct"):
    if mode == "nn":
        (m, k), (k2, n) = a.shape, b.shape
    elif mode == "nt":
        (m, k), (n, k2) = a.shape, b.shape
    else:
        (k, m), (k2, n) = a.shape, b.shape
    assert k == k2, (a.shape, b.shape, mode)
    tm, tn, tk = _tile(m, 1024, LANE), _tile(n, 768, LANE), _tile(k, 2048, LANE)
    nk = k // tk
    ne = len(exchange)
    grid = (m // tm, n // tn, nk)
    if mode == "nn":
        a_spec = pl.BlockSpec((tm, tk), lambda i, j, kk: (i, kk))
        b_spec = pl.BlockSpec((tk, tn), lambda i, j, kk: (kk, j))
        dims = (((1,), (0,)), ((), ()))
    elif mode == "nt":
        a_spec = pl.BlockSpec((tm, tk), lambda i, j, kk: (i, kk))
        b_spec = pl.BlockSpec((tn, tk), lambda i, j, kk: (j, kk))
        dims = (((1,), (1,)), ((), ()))
    else:
        a_spec = pl.BlockSpec((tk, tm), lambda i, j, kk: (kk, i))
        b_spec = pl.BlockSpec((tk, tn), lambda i, j, kk: (kk, j))
        dims = (((0,), (0,)), ((), ()))

    def body(*refs):
        a_ref, b_ref = refs[:2]
        parts = refs[2:2 + ne]
        o_ref = refs[2 + ne]
        recvs = refs[3 + ne:3 + 2 * ne]
        acc_ref = refs[3 + 2 * ne]
        i, j, kk = pl.program_id(0), pl.program_id(1), pl.program_id(2)
        if ne:
            copies = _exchange_copies(parts, recvs, *refs[4 + 2 * ne:], pattern=pattern)

            @pl.when((i == 0) & (j == 0) & (kk == 0))
            def _():
                for cp in copies:
                    cp.start()

        @pl.when(kk == 0)
        def _():
            acc_ref[...] = jnp.zeros_like(acc_ref)

        acc_ref[...] += lax.dot_general(a_ref[...], b_ref[...], dims, preferred_element_type=F32)

        @pl.when(kk == nk - 1)
        def _():
            o_ref[...] = acc_ref[...].astype(o_ref.dtype)

        if ne:
            @pl.when((i == grid[0] - 1) & (j == grid[1] - 1) & (kk == nk - 1))
            def _():
                for cp in copies:
                    cp.wait()

    res = pl.pallas_call(
        body, name=name,
        grid=grid,
        in_specs=[a_spec, b_spec] + [HBM_SPEC] * ne,
        out_specs=[pl.BlockSpec((tm, tn), lambda i, j, kk: (i, j))] + [HBM_SPEC] * ne,
        out_shape=[jax.ShapeDtypeStruct((m, n), out_dtype)] + _exchange_shapes(exchange, pattern),
        scratch_shapes=[pltpu.VMEM((tm, tn), F32)] + _exchange_sems(ne, pattern),
        compiler_params=_cparams(("arbitrary",) * 3 if ne else ("parallel", "parallel", "arbitrary")),
    )(a, b, *exchange)
    return res if ne else res[0]


def _prenorm(x, g):
    t, d = x.shape
    tt = _tile(t, 256, SUBLANE)

    def body(x_ref, g_ref, h_ref):
        xv = x_ref[...]
        rinv = lax.rsqrt(jnp.mean(xv * xv, axis=-1, keepdims=True) + NORM_EPS)
        h_ref[...] = (xv * rinv * g_ref[...]).astype(h_ref.dtype)

    return pl.pallas_call(
        body, name="prenorm", grid=(t // tt,),
        in_specs=[pl.BlockSpec((tt, d), lambda i: (i, 0)), _full((1, d))],
        out_specs=pl.BlockSpec((tt, d), lambda i: (i, 0)),
        out_shape=jax.ShapeDtypeStruct((t, d), MXU_DTYPE),
        compiler_params=_cparams(("parallel",)),
    )(x, g)


def _post_loss(out, x, target, g):
    t, d = out.shape
    tt = _tile(t, 128, SUBLANE)

    def body(o_ref, x_ref, t_ref, g_ref, dout_ref, dy_ref, loss_ref, dg_ref):
        i = pl.program_id(0)
        ov = o_ref[...]
        rinv = lax.rsqrt(jnp.mean(ov * ov, axis=-1, keepdims=True) + NORM_EPS)
        nv = ov * rinv
        gv = g_ref[...]
        err = x_ref[...] + nv * gv - t_ref[...]
        part = 0.5 * jnp.sum(jnp.mean(err * err, axis=-1, keepdims=True), axis=0, keepdims=True)
        dy = err * (1.0 / d)
        dy_ref[...] = dy
        dn = dy * gv
        dout = rinv * (dn - nv * jnp.mean(dn * nv, axis=-1, keepdims=True))
        dout_ref[...] = dout.astype(dout_ref.dtype)
        dg = jnp.sum(dy * nv, axis=0, keepdims=True)

        @pl.when(i == 0)
        def _():
            loss_ref[...] = jnp.zeros_like(loss_ref)
            dg_ref[...] = jnp.zeros_like(dg_ref)

        loss_ref[...] += jnp.broadcast_to(part, loss_ref.shape)
        dg_ref[...] += dg

    row = pl.BlockSpec((tt, d), lambda i: (i, 0))
    return pl.pallas_call(
        body, name="post_loss", grid=(t // tt,),
        in_specs=[row, row, row, _full((1, d))],
        out_specs=[row, row, _full((1, LANE)), _full((1, d))],
        out_shape=[jax.ShapeDtypeStruct((t, d), MXU_DTYPE), jax.ShapeDtypeStruct((t, d), F32),
                   jax.ShapeDtypeStruct((1, LANE), F32), jax.ShapeDtypeStruct((1, d), F32)],
        compiler_params=_cparams(("arbitrary",)),
    )(out, x, target, g)


def _prenorm_bwd(dh, x, dy, g):
    t, d = x.shape
    tt = _tile(t, 128, SUBLANE)

    def body(dh_ref, x_ref, dy_ref, g_ref, gx_ref, dg_ref):
        i = pl.program_id(0)
        xv = x_ref[...]
        rinv = lax.rsqrt(jnp.mean(xv * xv, axis=-1, keepdims=True) + NORM_EPS)
        nx = xv * rinv
        dhv = dh_ref[...]
        dnx = dhv * g_ref[...]
        dx = rinv * (dnx - nx * jnp.mean(dnx * nx, axis=-1, keepdims=True))
        gx_ref[...] = dy_ref[...] + dx

        @pl.when(i == 0)
        def _():
            dg_ref[...] = jnp.zeros_like(dg_ref)

        dg_ref[...] += jnp.sum(dhv * nx, axis=0, keepdims=True)

    row = pl.BlockSpec((tt, d), lambda i: (i, 0))
    return pl.pallas_call(
        body, name="prenorm_bwd", grid=(t // tt,),
        in_specs=[row, row, row, _full((1, d))],
        out_specs=[row, _full((1, d))],
        out_shape=[jax.ShapeDtypeStruct((t, d), F32), jax.ShapeDtypeStruct((1, d), F32)],
        compiler_params=_cparams(("arbitrary",)),
    )(dh, x, dy, g)


def _segsum(v, nh):
    q = v.shape[1] // LANE
    s = v[:, 0:LANE]
    for p in range(1, q):
        s = s + v[:, p * LANE:(p + 1) * LANE]
    shift = nh
    while shift < LANE:
        s = s + pltpu.roll(s, shift, 1)
        shift *= 2
    return jnp.concatenate([s] * q, axis=1)


def _shifted(cur_ref, prev_ref, lo, hi, first):
    cur = cur_ref[:, lo:hi]
    last = jnp.where(first, 0.0, prev_ref[SUBLANE - 1:SUBLANE, lo:hi])
    prev = pltpu.roll(cur, 1, 0)
    rows = lax.broadcasted_iota(jnp.int32, cur.shape, 0)
    return cur, jnp.where(rows == 0, last, prev)


def _rwkv_mix(main_ref, mainp_ref, lo_ref, lop_ref, mu_ref, mulo_ref, w0_ref, a0_ref, kk_ref, ka_ref,
              wupw_ref, wupa_ref, hw, nh, first):
    def xs(cur_ref, prev_ref, m_ref, lo, hi):
        cur, prev = _shifted(cur_ref, prev_ref, lo, hi, first)
        return cur + (prev - cur) * m_ref[:, lo:hi], prev - cur

    out = {}
    out["r"], out["r_d"] = xs(main_ref, mainp_ref, mu_ref, 0, hw)
    out["k"], out["k_d"] = xs(main_ref, mainp_ref, mu_ref, hw, 2 * hw)
    out["v"], out["v_d"] = xs(main_ref, mainp_ref, mu_ref, 2 * hw, 3 * hw)
    out["wl"], out["wl_d"] = xs(lo_ref, lop_ref, mulo_ref, 0, LORA_PAD)
    out["al"], out["al_d"] = xs(lo_ref, lop_ref, mulo_ref, LORA_PAD, 2 * LORA_PAD)
    th = jnp.tanh(out["wl"])
    zw = w0_ref[...] + jnp.dot(th, wupw_ref[...], preferred_element_type=F32, precision=HI)
    u = -zw
    softplus = jnp.maximum(u, 0.0) + jnp.log(1.0 + jnp.exp(-jnp.abs(u)))
    wlog = -softplus - 0.5
    ew = jnp.exp(wlog)
    za = a0_ref[...] + jnp.dot(out["al"], wupa_ref[...], preferred_element_type=F32, precision=HI)
    a = _sigmoid(za)
    kkr = out["k"] * kk_ref[...]
    nr = jnp.sqrt(_segsum(kkr * kkr, nh))
    nrm = jnp.maximum(nr, KK_EPS)
    out.update(th=th, zw=zw, ew=ew, decay=jnp.exp(-ew), a=a, kkr=kkr, nr=nr, nrm=nrm, kk=kkr / nrm)
    out["kh"] = out["k"] * (1.0 + (a - 1.0) * ka_ref[...])
    return out


def _mix_specs(tt, hw, lo_blk):
    mw, lw2 = 3 * hw, 2 * LORA_PAD
    before = lambda i: jnp.maximum(i * (tt // SUBLANE) - 1, 0)
    vec = _full((1, hw))
    return [pl.BlockSpec((tt, mw), lambda i: (i, 0)), pl.BlockSpec((SUBLANE, mw), lambda i: (before(i), 0)),
            pl.BlockSpec((tt, lw2), lambda i: (i, lo_blk)), pl.BlockSpec((SUBLANE, lw2), lambda i: (before(i), lo_blk)),
            _full((1, mw)), _full((1, lw2)), vec, vec, vec, vec, _full((LORA_PAD, hw)), _full((LORA_PAD, hw))]


def _rwkv_pre(proj, mu, mu_lo, w0, a0, k_k, k_a, wup_w, wup_a, hw, nh, lo_blk):
    t = proj.shape[0]
    tt = _tile(t, RWKV_ROWS, SUBLANE)

    def body(*refs):
        r_o, w_o, kh_o, v_o, kn_o, b_o = refs[-6:]
        f = _rwkv_mix(*refs[:-6], hw, nh, pl.program_id(0) == 0)
        for ref, val in ((r_o, f["r"]), (w_o, f["decay"]), (kh_o, f["kh"]), (v_o, f["v"]), (kn_o, -f["kk"]),
                         (b_o, f["kk"] * f["a"])):
            _store_compact(ref, val, tt, q)

    q = hw // LANE
    return pl.pallas_call(
        body, name="rwkv_pre", grid=(t // tt,),
        in_specs=_mix_specs(tt, hw, lo_blk),
        out_specs=[_compact_spec(tt, q)] * 6,
        out_shape=[jax.ShapeDtypeStruct((t * q, LANE), F32)] * 6,
        compiler_params=_cparams(("parallel",)),
    )(proj, proj, proj, proj, mu, mu_lo, w0, a0, k_k, k_a, wup_w, wup_a)


def _rwkv_post_math(y, r, kh, v, g, lnx_g, lnx_b, r_k, nh, n):
    mean = _segsum(y, nh) * (1.0 / n)
    yc = y - mean
    var = _segsum(yc * yc, nh) * (1.0 / n)
    rstd = lax.rsqrt(var + GN_EPS_PER_CHANNEL * n)
    yn = yc * rstd
    s = _segsum(r * kh * r_k, nh)
    y3 = yn * lnx_g + lnx_b + s * v
    sg = _sigmoid(g)
    return yn, rstd, s, y3, sg


def _rwkv_post(y, r, kh, v, proj, lnx_g, lnx_b, r_k, hw, n, gate_blk):
    t = proj.shape[0]
    tt = _tile(t, RWKV_ROWS, SUBLANE)
    q = hw // LANE
    nh = hw // n

    def body(y_ref, r_ref, kh_ref, v_ref, g_ref, lg_ref, lb_ref, rk_ref, o_ref):
        g = g_ref[...]
        y, r, kh, v = (_load_compact(ref, tt, q) for ref in (y_ref, r_ref, kh_ref, v_ref))
        _, _, _, y3, sg = _rwkv_post_math(y, r, kh, v, g, lg_ref[...], lb_ref[...], rk_ref[...], nh, n)
        o_ref[...] = (y3 * (g * sg)).astype(o_ref.dtype)

    row = pl.BlockSpec((tt, hw), lambda i: (i, 0))
    comp = _compact_spec(tt, q)
    vec = _full((1, hw))
    return pl.pallas_call(
        body, name="rwkv_post", grid=(t // tt,),
        in_specs=[comp, comp, comp, comp, pl.BlockSpec((tt, hw), lambda i: (i, gate_blk)), vec, vec, vec],
        out_specs=row,
        out_shape=jax.ShapeDtypeStruct((t, hw), MXU_DTYPE),
        compiler_params=_cparams(("parallel",)),
    )(y, r, kh, v, proj, lnx_g, lnx_b, r_k)


def _rwkv_post_bwd(dmix, y, r, kh, v, proj, lnx_g, lnx_b, r_k, hw, n, gate_blk, dproj):
    t = proj.shape[0]
    tt = _tile(t, RWKV_ROWS, SUBLANE)
    q = hw // LANE
    nh = hw // n

    def body(dm_ref, y_ref, r_ref, kh_ref, v_ref, g_ref, lg_ref, lb_ref, rk_ref, dproj_ref,
             dg_o, dy_o, dr_o, dkh_o, dv_o, dlg_o, dlb_o, drk_o):
        i = pl.program_id(0)
        g, rk, lg = g_ref[...], rk_ref[...], lg_ref[...]
        y, r, kh, v = (_load_compact(ref, tt, q) for ref in (y_ref, r_ref, kh_ref, v_ref))
        yn, rstd, s, y3, sg = _rwkv_post_math(y, r, kh, v, g, lg, lb_ref[...], rk, nh, n)
        dyr = dm_ref[...]
        dy3 = dyr * (g * sg)
        dg_o[...] = (dyr * y3 * (sg * (1.0 + g * (1.0 - sg)))).astype(dg_o.dtype)
        ds = _segsum(dy3 * v, nh)
        _store_compact(dv_o, dy3 * s, tt, q)
        _store_compact(dr_o, ds * kh * rk, tt, q)
        _store_compact(dkh_o, ds * r * rk, tt, q)
        dyn = dy3 * lg
        m1 = _segsum(dyn, nh) * (1.0 / n)
        m2 = _segsum(dyn * yn, nh) * (1.0 / n)
        _store_compact(dy_o, rstd * (dyn - m1 - yn * m2), tt, q)

        @pl.when(i == 0)
        def _():
            dlg_o[...] = jnp.zeros_like(dlg_o)
            dlb_o[...] = jnp.zeros_like(dlb_o)
            drk_o[...] = jnp.zeros_like(drk_o)

        dlg_o[...] += jnp.sum(dy3 * yn, axis=0, keepdims=True)
        dlb_o[...] += jnp.sum(dy3, axis=0, keepdims=True)
        drk_o[...] += jnp.sum(ds * r * kh, axis=0, keepdims=True)

    row = pl.BlockSpec((tt, hw), lambda i: (i, 0))
    comp = _compact_spec(tt, q)
    vec = _full((1, hw))
    rowf = jax.ShapeDtypeStruct((t * q, LANE), F32)
    vecf = jax.ShapeDtypeStruct((1, hw), F32)
    return pl.pallas_call(
        body, name="rwkv_post_bwd", grid=(t // tt,),
        in_specs=[row, comp, comp, comp, comp, pl.BlockSpec((tt, hw), lambda i: (i, gate_blk)), vec, vec, vec,
                  ANY_SPEC],
        out_specs=[pl.BlockSpec((tt, hw), lambda i: (i, gate_blk)), comp, comp, comp, comp, vec, vec, vec],
        out_shape=[jax.ShapeDtypeStruct(dproj.shape, dproj.dtype), rowf, rowf, rowf, rowf, vecf, vecf, vecf],
        input_output_aliases={9: 0},
        compiler_params=_cparams(("arbitrary",)),
    )(dmix, y, r, kh, v, proj, lnx_g, lnx_b, r_k, dproj)


def _rwkv_pre_bwd(proj, mu, mu_lo, w0, a0, k_k, k_a, wup_w, wup_a, hw, nh, lo_blk,
                  dr_rec, dw_rec, dkh_rec, dv_rec, dkn_rec, db_rec, dr_bon, dkh_bon, dv_bon):
    t = proj.shape[0]
    mw, lw2 = 3 * hw, 2 * LORA_PAD
    tt = _tile(t, RWKV_ROWS, SUBLANE)
    q = hw // LANE
    n_in = 12

    def body(*refs):
        mix_refs = refs[:n_in]
        drr, dwr, dkhr, dvr, dknr, db, drb, dkhb, dvb = (
            _load_compact(ref, tt, q) for ref in refs[n_in:n_in + 9])
        dxs_o, dxl_o, dmu_o, dmul_o, dw0_o, da0_o, dkk_o, dka_o, dwupw_o, dwupa_o = refs[n_in + 9:]
        kk_ref, ka_ref, wupw_ref, wupa_ref = mix_refs[8:12]
        i = pl.program_id(0)
        f = _rwkv_mix(*mix_refs, hw, nh, i == 0)
        k, a, kk, nrm = f["k"], f["a"], f["kk"], f["nrm"]
        k_a, k_k = ka_ref[...], kk_ref[...]
        dr = drr + drb
        dkh = dkhr + dkhb
        dv = dvr + dvb
        da = db * kk + dkh * k * k_a
        dkk = db * a - dknr
        dk = dkh * (1.0 + (a - 1.0) * k_a)
        dka = jnp.sum(dkh * k * (a - 1.0), axis=0, keepdims=True)
        proj_kk = _segsum(dkk * kk, nh)
        dkkr = jnp.where(f["nr"] > KK_EPS, (dkk - kk * proj_kk) / nrm, dkk * (1.0 / KK_EPS))
        dk = dk + dkkr * k_k
        dkk_w = jnp.sum(dkkr * k, axis=0, keepdims=True)
        dza = da * a * (1.0 - a)
        dzw = dwr * f["decay"] * (-f["ew"]) * _sigmoid(-f["zw"])
        nt_dims = (((1,), (1,)), ((), ()))
        tn_dims = (((0,), (0,)), ((), ()))
        dal = lax.dot_general(dza, wupa_ref[...], nt_dims, preferred_element_type=F32, precision=HI)
        dth = lax.dot_general(dzw, wupw_ref[...], nt_dims, preferred_element_type=F32, precision=HI)
        dwl = dth * (1.0 - f["th"] * f["th"])
        dxs_o[:, 0:hw] = dr
        dxs_o[:, hw:2 * hw] = dk
        dxs_o[:, 2 * hw:3 * hw] = dv
        dxl_o[:, 0:LORA_PAD] = dwl
        dxl_o[:, LORA_PAD:lw2] = dal

        @pl.when(i == 0)
        def _():
            for ref in (dmu_o, dmul_o, dw0_o, da0_o, dkk_o, dka_o, dwupw_o, dwupa_o):
                ref[...] = jnp.zeros_like(ref)

        def colsum(v):
            return jnp.sum(v, axis=0, keepdims=True)

        dmu_o[:, 0:hw] += colsum(dr * f["r_d"])
        dmu_o[:, hw:2 * hw] += colsum(dk * f["k_d"])
        dmu_o[:, 2 * hw:3 * hw] += colsum(dv * f["v_d"])
        dmul_o[:, 0:LORA_PAD] += colsum(dwl * f["wl_d"])
        dmul_o[:, LORA_PAD:lw2] += colsum(dal * f["al_d"])
        dw0_o[...] += colsum(dzw)
        da0_o[...] += colsum(dza)
        dkk_o[...] += dkk_w
        dka_o[...] += dka
        dwupw_o[...] += lax.dot_general(f["th"], dzw, tn_dims, preferred_element_type=F32, precision=HI)
        dwupa_o[...] += lax.dot_general(f["al"], dza, tn_dims, preferred_element_type=F32, precision=HI)

    vec = _full((1, hw))
    vecf = jax.ShapeDtypeStruct((1, hw), F32)
    return pl.pallas_call(
        body, name="rwkv_pre_bwd", grid=(t // tt,),
        in_specs=_mix_specs(tt, hw, lo_blk) + [_compact_spec(tt, q)] * 9,
        out_specs=[pl.BlockSpec((tt, mw), lambda i: (i, 0)), pl.BlockSpec((tt, lw2), lambda i: (i, 0)),
                   _full((1, mw)), _full((1, lw2)), vec, vec, vec, vec,
                   _full((LORA_PAD, hw)), _full((LORA_PAD, hw))],
        out_shape=[jax.ShapeDtypeStruct((t, mw), F32), jax.ShapeDtypeStruct((t, lw2), F32),
                   jax.ShapeDtypeStruct((1, mw), F32), jax.ShapeDtypeStruct((1, lw2), F32),
                   vecf, vecf, vecf, vecf,
                   jax.ShapeDtypeStruct((LORA_PAD, hw), F32), jax.ShapeDtypeStruct((LORA_PAD, hw), F32)],
        compiler_params=_cparams(("arbitrary",)),
    )(proj, proj, proj, proj, mu, mu_lo, w0, a0, k_k, k_a, wup_w, wup_a,
      dr_rec, dw_rec, dkh_rec, dv_rec, dkn_rec, db_rec, dr_bon, dkh_bon, dv_bon)


def _shift_bwd(dxs, mu, name, dproj, col_blk):
    t, sw = dxs.shape
    tt = _tile(t, 256, SUBLANE)
    nblk = t // SUBLANE

    def body(d_ref, nxt_ref, mu_ref, dproj_ref, o_ref):
        last = pl.program_id(0) == pl.num_programs(0) - 1
        cur = d_ref[...]
        first_next = jnp.where(last, 0.0, nxt_ref[0:1, :])
        nxt = pltpu.roll(cur, tt - 1, 0)
        rows = lax.broadcasted_iota(jnp.int32, cur.shape, 0)
        nxt = jnp.where(rows == tt - 1, first_next, nxt)
        m = mu_ref[...]
        o_ref[...] = (cur * (1.0 - m) + nxt * m).astype(o_ref.dtype)

    return pl.pallas_call(
        body, name=name, grid=(t // tt,),
        in_specs=[pl.BlockSpec((tt, sw), lambda i: (i, 0)),
                  pl.BlockSpec((SUBLANE, sw), lambda i: (jnp.minimum((i + 1) * (tt // SUBLANE), nblk - 1), 0)),
                  _full((1, sw)), ANY_SPEC],
        out_specs=pl.BlockSpec((tt, sw), lambda i: (i, col_blk)),
        out_shape=jax.ShapeDtypeStruct(dproj.shape, dproj.dtype),
        input_output_aliases={3: 0},
        compiler_params=_cparams(("parallel",)),
    )(dxs, dxs, mu, dproj)


def _tree_sum(parts):
    while len(parts) > 1:
        parts = [parts[p] + parts[p + 1] for p in range(0, len(parts) - 1, 2)] + ([parts[-1]] if len(parts) % 2 else [])
    return parts[0]


def _tile_rows(src_ref, dst_ref, tc, nh):
    def convert(ts, carry):
        _tile_step(src_ref, dst_ref, ts, nh)
        return carry

    lax.fori_loop(0, tc, convert, 0)


def _tile_step(src_ref, dst_ref, ts, nh):
    for grp, m in enumerate(_tiled(src_ref[ts], nh)):
        dst_ref[ts, grp] = m


def _tiled(v, nh):
    rep = LANE // nh
    lane_group = lax.broadcasted_iota(jnp.int32, v.shape, 1) // nh
    rolled = [v] + [pltpu.roll(v, k * nh, 1) for k in range(1, rep)]
    out = []
    for grp in range(rep):
        m = rolled[(0 - grp) % rep]
        for g in range(1, rep):
            m = jnp.where(lane_group == g, rolled[(g - grp) % rep], m)
        out.append(m)
    return out


def _wkv_fwd(w_c, b_c, k_c, r_c, kn_c, v_c, nh):
    t, q, _ = v_c.shape
    rep = LANE // nh
    n = q * rep
    tc = _tile(t, WKV_CHUNK, 1)
    nc = t // tc
    nacc = 4
    nv = 5

    def body(*refs):
        cur, v_ref, nxt = refs[:nv], refs[nv], refs[nv + 1:2 * nv + 1]
        y_ref, sa_ref, ck_ref, s_ref = refs[2 * nv + 1:2 * nv + 5]
        tiles_even, tiles_odd = refs[2 * nv + 5:3 * nv + 5], refs[3 * nv + 5:]
        c = pl.program_id(0)

        @pl.when(c == 0)
        def _():
            s_ref[...] = jnp.zeros_like(s_ref)
            for src, dst in zip(cur, tiles_even):
                _tile_rows(src, dst, tc, nh)

        ck_ref[0] = s_ref[...]

        def row(ref, ts, j):
            return ref[ts, j % rep, pl.ds(j // rep, 1), :]

        def run(mine, ahead_tiles):
            wt, bt, kt, rt, knt = mine

            def step(ts, carry):
                for src, dst in zip(nxt, ahead_tiles):
                    _tile_step(src, dst, ts, nh)
                vt = v_ref[ts]
                acc = [None] * nacc
                for j in range(n):
                    term = s_ref[j] * row(knt, ts, j)
                    acc[j % nacc] = term if acc[j % nacc] is None else acc[j % nacc] + term
                sa = _tree_sum(acc)
                sa_ref[ts] = sa
                acc = [None] * nacc
                for j in range(n):
                    sj = s_ref[j] * row(wt, ts, j) + sa * row(bt, ts, j) + vt * row(kt, ts, j)
                    s_ref[j] = sj
                    term = sj * row(rt, ts, j)
                    acc[j % nacc] = term if acc[j % nacc] is None else acc[j % nacc] + term
                y_ref[ts] = _tree_sum(acc)
                return carry

            lax.fori_loop(0, tc, step, 0)

        @pl.when(c % 2 == 0)
        def _():
            run(tiles_even, tiles_odd)

        @pl.when(c % 2 == 1)
        def _():
            run(tiles_odd, tiles_even)

    comp = pl.BlockSpec((tc, q, LANE), lambda c: (c, 0, 0))
    ahead = pl.BlockSpec((tc, q, LANE), lambda c: (jnp.minimum(c + 1, nc - 1), 0, 0))
    return pl.pallas_call(
        body, name="wkv_fwd", grid=(nc,),
        in_specs=[comp] * (nv + 1) + [ahead] * nv,
        out_specs=[comp, comp, pl.BlockSpec((1, n, q, LANE), lambda c: (c, 0, 0, 0))],
        out_shape=[jax.ShapeDtypeStruct((t, q, LANE), F32), jax.ShapeDtypeStruct((t, q, LANE), F32),
                   jax.ShapeDtypeStruct((nc, n, q, LANE), F32)],
        scratch_shapes=[pltpu.VMEM((n, q, LANE), F32)] + [pltpu.VMEM((tc, rep, q, LANE), F32)] * (2 * nv),
        compiler_params=_cparams(("arbitrary",)),
    )(w_c, b_c, k_c, r_c, kn_c, v_c, w_c, b_c, k_c, r_c, kn_c)


def _wkv_bwd(r_c, w_c, b_c, k_c, kn_c, dy_c, sa_c, v_c, ck_i, nh, exchange=()):
    t, q, _ = dy_c.shape
    rep = LANE // nh
    n = q * rep
    tc = _tile(t, WKV_CHUNK, 1)
    nc = t // tc
    nacc = 2
    ne = len(exchange)
    nv = 8
    n_in = 2 * nv + 1 + ne

    def body(*refs):
        cur, ck_ref, nxt = refs[:nv], refs[nv], refs[nv + 1:2 * nv + 1]
        rc_ref, wc_ref, bc_ref, kc_ref, knc_ref, dyc_ref = cur[:6]
        parts = refs[2 * nv + 1:n_in]
        dv_o, dr_o, dw_o, db_o, dk_o, dkn_o = refs[n_in:n_in + 6]
        recvs = refs[n_in + 6:n_in + 6 + ne]
        hist, g_ref, gp_ref, dsat_ref = refs[n_in + 6 + ne:n_in + 10 + ne]
        tiles_even = refs[n_in + 10 + ne:n_in + 10 + ne + nv]
        tiles_odd = refs[n_in + 10 + ne + nv:n_in + 10 + ne + 2 * nv]
        c = pl.program_id(0)
        if ne:
            copies = _exchange_copies(parts, recvs, *refs[n_in + 10 + ne + 2 * nv:])

            @pl.when(c == 0)
            def _():
                for cp in copies:
                    cp.start()

        @pl.when(c == 0)
        def _():
            g_ref[...] = jnp.zeros_like(g_ref)
            gp_ref[...] = jnp.zeros_like(gp_ref)
            for src, dst in zip(cur, tiles_even):
                _tile_rows(src, dst, tc, nh)

        def row(ref, ts, idx):
            return ref[ts, idx % rep, pl.ds(idx // rep, 1), :]

        hist[0] = ck_ref[0]

        def run(mine, ahead_tiles):
            rt, wt, bt, kt, knt, dyt, sat, vt = mine

            def fstep(ts, carry):
                wv, bv, kv = wc_ref[ts], bc_ref[ts], kc_ref[ts]
                for i in range(n):
                    hist[ts + 1, i] = hist[ts, i] * wv + row(sat, ts, i) * bv + row(vt, ts, i) * kv
                return carry

            lax.fori_loop(0, tc, fstep, 0)

            def bstep(s, carry):
                ts = tc - 1 - s
                for src, dst in zip(nxt, ahead_tiles):
                    _tile_step(src, dst, ts, nh)
                dy = dyc_ref[ts]
                acc_sa, acc_v = [None] * nacc, [None] * nacc
                for j in range(n):
                    gj = g_ref[j] + dy * row(rt, ts, j)
                    g_ref[j] = gj
                    t1 = gj * row(bt, ts, j)
                    t2 = gj * row(kt, ts, j)
                    a = j % nacc
                    acc_sa[a] = t1 if acc_sa[a] is None else acc_sa[a] + t1
                    acc_v[a] = t2 if acc_v[a] is None else acc_v[a] + t2
                dsa = _tree_sum(acc_sa)
                dv_o[ts] = _tree_sum(acc_v)
                for j in range(n):
                    g_ref[j] = g_ref[j] * row(wt, ts, j) + dsa * row(knt, ts, j)
                for grp, m in enumerate(_tiled(dsa, nh)):
                    dsat_ref[grp] = m
                rv, wv, knv = rc_ref[ts], wc_ref[ts], knc_ref[ts]
                names = ("dr", "dw", "db", "dk", "dkn")
                accs = {nm: [None] * nacc for nm in names}
                for i in range(n):
                    dsai = dsat_ref[i % rep, pl.ds(i // rep, 1), :]
                    dyi = row(dyt, ts, i)
                    s_prev = hist[ts, i]
                    gi = gp_ref[i] + dyi * rv
                    terms = {"dr": hist[ts + 1, i] * dyi, "dw": gi * s_prev, "db": gi * row(sat, ts, i),
                             "dk": gi * row(vt, ts, i), "dkn": dsai * s_prev}
                    a = i % nacc
                    for nm in names:
                        accs[nm][a] = terms[nm] if accs[nm][a] is None else accs[nm][a] + terms[nm]
                    gp_ref[i] = gi * wv + dsai * knv
                dr_o[ts] = _tree_sum(accs["dr"])
                dw_o[ts] = _tree_sum(accs["dw"])
                db_o[ts] = _tree_sum(accs["db"])
                dk_o[ts] = _tree_sum(accs["dk"])
                dkn_o[ts] = _tree_sum(accs["dkn"])
                return carry

            lax.fori_loop(0, tc, bstep, 0)

        @pl.when(c % 2 == 0)
        def _():
            run(tiles_even, tiles_odd)

        @pl.when(c % 2 == 1)
        def _():
            run(tiles_odd, tiles_even)

        if ne:
            @pl.when(c == nc - 1)
            def _():
                for cp in copies:
                    cp.wait()

    comp = pl.BlockSpec((tc, q, LANE), lambda c: (nc - 1 - c, 0, 0))
    ahead = pl.BlockSpec((tc, q, LANE), lambda c: (jnp.maximum(nc - 2 - c, 0), 0, 0))
    outc = jax.ShapeDtypeStruct((t, q, LANE), F32)
    vectors = (r_c, w_c, b_c, k_c, kn_c, dy_c, sa_c, v_c)
    return pl.pallas_call(
        body, name="wkv_bwd", grid=(nc,),
        in_specs=[comp] * nv + [pl.BlockSpec((1, n, q, LANE), lambda c: (nc - 1 - c, 0, 0, 0))] + [ahead] * nv
        + [HBM_SPEC] * ne,
        out_specs=[comp] * 6 + [HBM_SPEC] * ne,
        out_shape=[outc] * 6 + _exchange_shapes(exchange),
        scratch_shapes=[pltpu.VMEM((tc + 1, n, q, LANE), F32), pltpu.VMEM((n, q, LANE), F32),
                        pltpu.VMEM((n, q, LANE), F32), pltpu.VMEM((rep, q, LANE), F32)]
        + [pltpu.VMEM((tc, rep, q, LANE), F32)] * (2 * nv) + _exchange_sems(ne),
        compiler_params=_cparams(("arbitrary",)),
    )(*vectors, ck_i, *vectors, *exchange)


def _shift_copies(ext_ref, sh_ref):
    rows = ext_ref.shape[0] - SUBLANE
    for p in range(1, SUBLANE):
        sh_ref[p - 1, 0:rows, :] = ext_ref[p:p + rows, :]


def _window(ext_ref, sh_ref, start, size):
    p = start % SUBLANE
    if p == 0:
        return ext_ref[start:start + size, :]
    assert start - p + size <= ext_ref.shape[0] - SUBLANE
    return sh_ref[p - 1, start - p:start - p + size, :]


def _conv_stage(gv_ref, gg_ref, gvh_ref, ggh_ref, cw_ref, cb_ref, lg_ref, lb_ref, ext_ref, sh_ref, first, tt, taps):
    u = gv_ref[...] * _sigmoid(gg_ref[...])
    uh = jnp.where(first, 0.0, gvh_ref[...] * _sigmoid(ggh_ref[...]))
    ext_ref[0:CONV_HALO, :] = uh
    ext_ref[CONV_HALO:CONV_HALO + tt, :] = u
    _shift_copies(ext_ref, sh_ref)
    off = CONV_HALO - (taps - 1)
    c = cb_ref[...] + _window(ext_ref, sh_ref, off, tt) * cw_ref[0:1, :]
    for j in range(1, taps):
        c = c + _window(ext_ref, sh_ref, off + j, tt) * cw_ref[j:j + 1, :]
    mean = jnp.mean(c, axis=-1, keepdims=True)
    cc = c - mean
    rstd = lax.rsqrt(jnp.mean(cc * cc, axis=-1, keepdims=True) + LN_EPS)
    chat = cc * rstd
    cn = chat * lg_ref[...] + lb_ref[...]
    return chat, rstd, cn


def _conv_specs(t, tt, cw, taps, gv_blk, gg_blk):
    hb = tt // CONV_HALO
    return [pl.BlockSpec((tt, cw), lambda i: (i, gv_blk)), pl.BlockSpec((tt, cw), lambda i: (i, gg_blk)),
            pl.BlockSpec((CONV_HALO, cw), lambda i: (jnp.maximum(i * hb - 1, 0), gv_blk)),
            pl.BlockSpec((CONV_HALO, cw), lambda i: (jnp.maximum(i * hb - 1, 0), gg_blk)),
            _full((taps, cw)), _full((1, cw)), _full((1, cw)), _full((1, cw))]


def _conv_fwd(proj, conv_w, conv_b, cln_g, cln_b, cw, gv_blk, gg_blk):
    t = proj.shape[0]
    taps = conv_w.shape[0]
    tt = _tile(t, 128, CONV_HALO)

    def body(gv_ref, gg_ref, gvh_ref, ggh_ref, cw_ref, cb_ref, lg_ref, lb_ref, o_ref, ext_ref, sh_ref):
        _, _, cn = _conv_stage(gv_ref, gg_ref, gvh_ref, ggh_ref, cw_ref, cb_ref, lg_ref, lb_ref, ext_ref, sh_ref,
                               pl.program_id(0) == 0, tt, taps)
        o_ref[...] = (cn * _sigmoid(cn)).astype(o_ref.dtype)

    return pl.pallas_call(
        body, name="conv_fwd", grid=(t // tt,),
        in_specs=_conv_specs(t, tt, cw, taps, gv_blk, gg_blk),
        out_specs=pl.BlockSpec((tt, cw), lambda i: (i, 0)),
        out_shape=jax.ShapeDtypeStruct((t, cw), MXU_DTYPE),
        scratch_shapes=[pltpu.VMEM((CONV_HALO + tt, cw), F32), pltpu.VMEM((SUBLANE - 1, CONV_HALO + tt, cw), F32)],
        compiler_params=_cparams(("parallel",)),
    )(proj, proj, proj, proj, conv_w, conv_b, cln_g, cln_b)


def _conv_gate(c2, proj, b_pw2, cw, gc_blk):
    t = c2.shape[0]
    tt = _tile(t, 256, SUBLANE)

    def body(c_ref, g_ref, b_ref, o_ref):
        g = g_ref[...]
        o_ref[...] = ((c_ref[...] + b_ref[...]) * (g * _sigmoid(g))).astype(o_ref.dtype)

    return pl.pallas_call(
        body, name="conv_gate", grid=(t // tt,),
        in_specs=[pl.BlockSpec((tt, cw), lambda i: (i, 0)), pl.BlockSpec((tt, cw), lambda i: (i, gc_blk)),
                  _full((1, cw))],
        out_specs=pl.BlockSpec((tt, cw), lambda i: (i, 0)),
        out_shape=jax.ShapeDtypeStruct((t, cw), MXU_DTYPE),
        compiler_params=_cparams(("parallel",)),
    )(c2, proj, b_pw2)


def _conv_gate_bwd(dmix, c2, proj, b_pw2, cw, dm_blk, gc_blk):
    t = c2.shape[0]
    tt = _tile(t, 256, SUBLANE)

    def body(dm_ref, c_ref, g_ref, b_ref, dc2_o, dg_o, db_o):
        g = g_ref[...]
        sg = _sigmoid(g)
        dyc = dm_ref[...]
        dc2 = dyc * (g * sg)
        dc2_o[...] = dc2.astype(dc2_o.dtype)
        dg_o[...] = (dyc * (c_ref[...] + b_ref[...]) * (sg * (1.0 + g * (1.0 - sg)))).astype(dg_o.dtype)

        @pl.when(pl.program_id(0) == 0)
        def _():
            db_o[...] = jnp.zeros_like(db_o)

        db_o[...] += jnp.sum(dc2, axis=0, keepdims=True)

    row = pl.BlockSpec((tt, cw), lambda i: (i, 0))
    return pl.pallas_call(
        body, name="conv_gate_bwd", grid=(t // tt,),
        in_specs=[pl.BlockSpec((tt, cw), lambda i: (i, dm_blk)), row,
                  pl.BlockSpec((tt, cw), lambda i: (i, gc_blk)), _full((1, cw))],
        out_specs=[row, pl.BlockSpec((tt, cw), lambda i: (i, gc_blk)), _full((1, cw))],
        out_shape=[jax.ShapeDtypeStruct((t, cw), MXU_DTYPE), jax.ShapeDtypeStruct((t, proj.shape[1]), MXU_DTYPE),
                   jax.ShapeDtypeStruct((1, cw), F32)],
        compiler_params=_cparams(("arbitrary",)),
    )(dmix, c2, proj, b_pw2)


def _conv_bwd_norm(proj, dcs, conv_w, conv_b, cln_g, cln_b, cw, gv_blk, gg_blk):
    t = proj.shape[0]
    taps = conv_w.shape[0]
    tt = _tile(t, 128, CONV_HALO)

    def body(gv_ref, gg_ref, gvh_ref, ggh_ref, cw_ref, cb_ref, lg_ref, lb_ref, dcs_ref,
             dc_o, dcw_o, dcb_o, dlg_o, dlb_o, ext_ref, sh_ref):
        chat, rstd, cn = _conv_stage(gv_ref, gg_ref, gvh_ref, ggh_ref, cw_ref, cb_ref, lg_ref, lb_ref, ext_ref,
                                     sh_ref, pl.program_id(0) == 0, tt, taps)
        s = _sigmoid(cn)
        dcn = dcs_ref[...] * (s * (1.0 + cn * (1.0 - s)))
        dchat = dcn * lg_ref[...]
        dc = rstd * (dchat - jnp.mean(dchat, axis=-1, keepdims=True)
                     - chat * jnp.mean(dchat * chat, axis=-1, keepdims=True))
        dc_o[...] = dc

        @pl.when(pl.program_id(0) == 0)
        def _():
            for ref in (dcw_o, dcb_o, dlg_o, dlb_o):
                ref[...] = jnp.zeros_like(ref)

        dlg_o[...] += jnp.sum(dcn * chat, axis=0, keepdims=True)
        dlb_o[...] += jnp.sum(dcn, axis=0, keepdims=True)
        dcb_o[...] += jnp.sum(dc, axis=0, keepdims=True)
        off = CONV_HALO - (taps - 1)
        for j in range(taps):
            dcw_o[j:j + 1, :] += jnp.sum(_window(ext_ref, sh_ref, off + j, tt) * dc, axis=0, keepdims=True)

    vec = _full((1, cw))
    vecf = jax.ShapeDtypeStruct((1, cw), F32)
    return pl.pallas_call(
        body, name="conv_bwd_norm", grid=(t // tt,),
        in_specs=_conv_specs(t, tt, cw, taps, gv_blk, gg_blk) + [pl.BlockSpec((tt, cw), lambda i: (i, 0))],
        out_specs=[pl.BlockSpec((tt, cw), lambda i: (i, 0)), _full((taps, cw)), vec, vec, vec],
        out_shape=[jax.ShapeDtypeStruct((t, cw), F32), jax.ShapeDtypeStruct((taps, cw), F32), vecf, vecf, vecf],
        scratch_shapes=[pltpu.VMEM((CONV_HALO + tt, cw), F32), pltpu.VMEM((SUBLANE - 1, CONV_HALO + tt, cw), F32)],
        compiler_params=_cparams(("arbitrary",)),
    )(proj, proj, proj, proj, conv_w, conv_b, cln_g, cln_b, dcs)


def _conv_bwd_glu(dc, proj, conv_w, cw, gv_blk, gg_blk, dproj):
    t = dc.shape[0]
    taps = conv_w.shape[0]
    tt = _tile(t, 128, CONV_HALO)
    hb = tt // CONV_HALO
    nhalo = t // CONV_HALO
    assert gg_blk == gv_blk + 1 and gv_blk % 2 == 0

    def body(dc_ref, dch_ref, gv_ref, gg_ref, cw_ref, dproj_ref, dg_o, ext_ref, sh_ref):
        last = pl.program_id(0) == pl.num_programs(0) - 1
        ext_ref[0:tt, :] = dc_ref[...]
        ext_ref[tt:tt + CONV_HALO, :] = jnp.where(last, 0.0, dch_ref[...])
        _shift_copies(ext_ref, sh_ref)
        du = _window(ext_ref, sh_ref, taps - 1, tt) * cw_ref[0:1, :]
        for j in range(1, taps):
            du = du + _window(ext_ref, sh_ref, taps - 1 - j, tt) * cw_ref[j:j + 1, :]
        sg = _sigmoid(gg_ref[...])
        dg_o[:, 0:cw] = (du * sg).astype(dg_o.dtype)
        dg_o[:, cw:2 * cw] = (du * gv_ref[...] * sg * (1.0 - sg)).astype(dg_o.dtype)

    row = pl.BlockSpec((tt, cw), lambda i: (i, 0))
    return pl.pallas_call(
        body, name="conv_bwd_glu", grid=(t // tt,),
        in_specs=[row, pl.BlockSpec((CONV_HALO, cw), lambda i: (jnp.minimum((i + 1) * hb, nhalo - 1), 0)),
                  pl.BlockSpec((tt, cw), lambda i: (i, gv_blk)), pl.BlockSpec((tt, cw), lambda i: (i, gg_blk)),
                  _full((taps, cw)), ANY_SPEC],
        out_specs=pl.BlockSpec((tt, 2 * cw), lambda i: (i, gv_blk // 2)),
        out_shape=jax.ShapeDtypeStruct(dproj.shape, dproj.dtype),
        input_output_aliases={5: 0},
        scratch_shapes=[pltpu.VMEM((tt + CONV_HALO, cw), F32), pltpu.VMEM((SUBLANE - 1, tt + CONV_HALO, cw), F32)],
        compiler_params=_cparams(("parallel",)),
    )(dc, dc, proj, proj, conv_w, dproj)


HBM_SPEC = pl.BlockSpec(memory_space=pltpu.HBM)
ANY_SPEC = pl.BlockSpec(memory_space=pl.ANY)


def _all_gather(shards, name):
    na = len(shards)

    def body(*refs):
        ins, outs = refs[:na], refs[na:2 * na]
        send_sems, recv_sems, local_sems = refs[2 * na:]
        x, y, c = lax.axis_index("x"), lax.axis_index("y"), lax.axis_index("c")
        me, sibling = (x, y, c), (x, y, 1 - c)
        chips = [(1 - x, y), (x, 1 - y), (1 - x, 1 - y)]

        def slot(px, py, pc):
            return 4 * px + 2 * py + pc

        def copy(a, k, block, to, src=None):
            dst = outs[a].at[slot(*block)]
            return pltpu.make_async_remote_copy(
                src_ref=dst if src is None else src, dst_ref=dst,
                send_sem=send_sems.at[a, k], recv_sem=recv_sems.at[a, k],
                device_id=to, device_id_type=MESH)

        mine = [pltpu.make_async_copy(ins[a], outs[a].at[slot(*me)], local_sems.at[a]) for a in range(na)]
        for cp in mine:
            cp.start()
        first = []
        for a in range(na):
            first.append(copy(a, 0, me, sibling, src=ins[a]))
            first += [copy(a, 1 + j, me, (*chip, c), src=ins[a]) for j, chip in enumerate(chips)]
        for cp in first:
            cp.start()
        passed = []
        for j, chip in enumerate(chips):
            for a in range(na):
                copy(a, 1 + j, (*chip, c), me).wait_recv()
                fwd = copy(a, 4 + j, (*chip, c), sibling)
                fwd.start()
                passed.append(fwd)
        for a in range(na):
            copy(a, 0, sibling, me).wait_recv()
            for j, chip in enumerate(chips):
                copy(a, 4 + j, (*chip, 1 - c), me).wait_recv()
        for cp in first + passed:
            cp.wait_send()
        for cp in mine:
            cp.wait()

    return pl.pallas_call(
        body, name=name,
        in_specs=[HBM_SPEC] * na, out_specs=[HBM_SPEC] * na,
        out_shape=[jax.ShapeDtypeStruct((N_DEV,) + s.shape, s.dtype) for s in shards],
        scratch_shapes=[pltpu.SemaphoreType.DMA((na, 7)), pltpu.SemaphoreType.DMA((na, 7)),
                        pltpu.SemaphoreType.DMA((na,))],
        compiler_params=pltpu.CompilerParams(has_side_effects=True),
    )(*shards)


N_CHIPS = N_DEV // 2
EXCHANGE_SLOTS = {"direct": N_DEV - 1, "sibling": N_CHIPS, "chips": N_CHIPS - 1}


def _exchange_copies(ins, outs, send_sems, recv_sems, pattern="direct"):
    x, y, c = lax.axis_index("x"), lax.axis_index("y"), lax.axis_index("c")
    copies = []

    def add(a, src_slot, dst_slot, sem, peer):
        copies.append(pltpu.make_async_remote_copy(
            src_ref=ins[a].at[src_slot], dst_ref=outs[a].at[dst_slot],
            send_sem=send_sems.at[a, sem], recv_sem=recv_sems.at[a, sem],
            device_id=peer, device_id_type=MESH))

    for a in range(len(ins)):
        if pattern == "direct":
            for k in range(1, N_DEV):
                px = 1 - x if k & 4 else x
                py = 1 - y if k & 2 else y
                pc = 1 - c if k & 1 else c
                add(a, 4 * px + 2 * py + pc, k - 1, k - 1, (px, py, pc))
        elif pattern == "sibling":
            for j in range(N_CHIPS):
                add(a, j, j, j, (x, y, 1 - c))
        else:
            for k in range(1, N_CHIPS):
                px = 1 - x if k & 2 else x
                py = 1 - y if k & 1 else y
                add(a, 2 * px + py, k - 1, k - 1, (px, py, c))
    return copies


def _exchange_shapes(parts, pattern="direct"):
    return [jax.ShapeDtypeStruct((EXCHANGE_SLOTS[pattern],) + p.shape[1:], p.dtype) for p in parts]


def _exchange_sems(na, pattern="direct"):
    if not na:
        return []
    return [pltpu.SemaphoreType.DMA((na, EXCHANGE_SLOTS[pattern]))] * 2


def _exchange(parts, name, pattern):
    na = len(parts)

    def body(*refs):
        copies = _exchange_copies(refs[:na], refs[na:2 * na], *refs[2 * na:], pattern=pattern)
        for cp in copies:
            cp.start()
        for cp in copies:
            cp.wait()

    return pl.pallas_call(
        body, name=name,
        in_specs=[HBM_SPEC] * na, out_specs=[HBM_SPEC] * na,
        out_shape=_exchange_shapes(parts, pattern),
        scratch_shapes=_exchange_sems(na, pattern),
        compiler_params=pltpu.CompilerParams(has_side_effects=True),
    )(*parts)


def _chip_sum(mine, theirs):
    nslot, r, c = mine.shape
    rows = nslot * r
    tr = _tile(rows, 256, SUBLANE)

    def body(a_ref, b_ref, f_ref, w_ref):
        s = a_ref[...] + b_ref[...].astype(F32)
        f_ref[...] = s
        w_ref[...] = s.astype(w_ref.dtype)

    blk = pl.BlockSpec((tr, c), lambda i: (i, 0))
    f, w = pl.pallas_call(
        body, name="chip_sum", grid=(rows // tr,),
        in_specs=[blk, blk], out_specs=[blk, blk],
        out_shape=[jax.ShapeDtypeStruct((rows, c), F32), jax.ShapeDtypeStruct((rows, c), theirs.dtype)],
        compiler_params=_cparams(("parallel",)),
    )(mine.reshape(rows, c), theirs.reshape(rows, c))
    return f.reshape(mine.shape), w.reshape(mine.shape)


def _cast(v, dtype, name):
    r, c = v.shape
    tr = _tile(r, 256, SUBLANE)

    def body(i_ref, o_ref):
        o_ref[...] = i_ref[...].astype(o_ref.dtype)

    return pl.pallas_call(
        body, name=name, grid=(r // tr,),
        in_specs=[pl.BlockSpec((tr, c), lambda i: (i, 0))],
        out_specs=pl.BlockSpec((tr, c), lambda i: (i, 0)),
        out_shape=jax.ShapeDtypeStruct((r, c), dtype),
        compiler_params=_cparams(("parallel",)),
    )(v)


def _adamw(w, m, v, recv, own, name):
    r, c = w.shape
    ns = recv.shape[0]
    tr = _tile(r, 128, SUBLANE)
    c1 = 1.0 - ADAM_B1 ** ADAM_STEP
    c2 = 1.0 - ADAM_B2 ** ADAM_STEP

    def body(*refs):
        if own is None:
            w_ref, m_ref, v_ref, rc_ref = refs[:4]
            g = rc_ref[0].astype(F32)
            start = 1
        else:
            w_ref, m_ref, v_ref, rc_ref, own_ref = refs[:5]
            g = own_ref[...]
            start = 0
        g_o, d_o, m_o, v_o = refs[-4:]
        for s in range(start, ns):
            g = g + rc_ref[s].astype(F32)
        mn = ADAM_B1 * m_ref[...] + (1.0 - ADAM_B1) * g
        vn = ADAM_B2 * v_ref[...] + (1.0 - ADAM_B2) * (g * g)
        m_hat = mn / c1
        v_hat = vn / c2
        g_o[...] = g
        d_o[...] = -ADAM_LR * (m_hat / (jnp.sqrt(v_hat) + ADAM_EPS) + ADAM_WD * w_ref[...])
        m_o[...] = mn
        v_o[...] = vn

    row = pl.BlockSpec((tr, c), lambda i: (i, 0))
    ins = [w, m, v, recv] + ([] if own is None else [own])
    in_specs = [row, row, row, pl.BlockSpec((ns, tr, c), lambda i: (0, i, 0))] + ([] if own is None else [row])
    return pl.pallas_call(
        body, name=name, grid=(r // tr,),
        in_specs=in_specs, out_specs=[row] * 4,
        out_shape=[jax.ShapeDtypeStruct((r, c), F32)] * 4,
        compiler_params=_cparams(("parallel",)),
    )(*ins)


def _to_t(v, nh, n, axis=-1):
    v = jnp.moveaxis(v, axis, -1)
    v = v.reshape(v.shape[:-1] + (nh, n)).swapaxes(-1, -2).reshape(v.shape)
    return jnp.moveaxis(v, -1, axis)


def _from_t(v, nh, n, axis=-1):
    v = jnp.moveaxis(v, axis, -1)
    v = v.reshape(v.shape[:-1] + (n, nh)).swapaxes(-1, -2).reshape(v.shape)
    return jnp.moveaxis(v, -1, axis)


def _pad_to(v, size, axis):
    pad = [(0, 0)] * v.ndim
    pad[axis] = (0, size - v.shape[axis])
    return jnp.pad(v, pad)


def kernel(x, norm_pre_g, w_in, mu_shift, w0, w_lora_up, a0, a_lora_up, k_k, k_a, r_k, lnx_g, lnx_b, conv_w, conv_b, cln_g, cln_b, w_pw2, b_pw2, w_out, norm_post_g, loss_target, m_norm_pre_g, m_w_in, m_mu_shift, m_w0, m_w_lora_up, m_a0, m_a_lora_up, m_k_k, m_k_a, m_r_k, m_lnx_g, m_lnx_b, m_conv_w, m_conv_b, m_cln_g, m_cln_b, m_w_pw2, m_b_pw2, m_w_out, m_norm_post_g, v_norm_pre_g, v_w_in, v_mu_shift, v_w0, v_w_lora_up, v_a0, v_a_lora_up, v_k_k, v_k_a, v_r_k, v_lnx_g, v_lnx_b, v_conv_w, v_conv_b, v_cln_g, v_cln_b, v_w_pw2, v_b_pw2, v_w_out, v_norm_post_g):
    args = dict(locals())
    t, d = x.shape[1], x.shape[2]
    hw = w0.shape[0]
    cw = conv_b.shape[0]
    nh, n = r_k.shape
    lw, la = w_lora_up.shape[0], a_lora_up.shape[0]
    taps = conv_w.shape[0]
    in_cols = w_in.shape[1] * N_DEV
    shift_cols = 3 * hw + lw + la
    assert in_cols == shift_cols + hw + 3 * cw and hw == cw and hw % LANE == 0 and LANE % nh == 0
    assert lw <= LORA_PAD and la <= LORA_PAD and taps - 1 <= CONV_HALO and hw % (2 * LORA_PAD) == 0
    q = hw // LANE
    gate_blk, gv_blk, gg_blk, gc_blk = 3, 4, 5, 6
    lo0 = 7 * hw
    lo_blk = lo0 // (2 * LORA_PAD)
    x2, tgt2 = x[0], loss_target[0]
    row = lambda v: v.reshape(1, -1)

    gathered = _all_gather(
        [_cast(w_in, MXU_DTYPE, "cast_w_in"), _cast(w_out, MXU_DTYPE, "cast_w_out"),
         _cast(w_pw2, MXU_DTYPE, "cast_w_pw2"), w_lora_up, a_lora_up, conv_w], "gather_weights")
    w_in_g, w_out_g, w_pw2_g, wup_w_g, wup_a_g, conv_w_g = gathered
    w_full = w_in_g.transpose(1, 0, 2).reshape(d, in_cols)
    c0 = shift_cols
    wp = jnp.concatenate([
        _to_t(w_full[:, 0:hw], nh, n), _to_t(w_full[:, hw:2 * hw], nh, n), _to_t(w_full[:, 2 * hw:3 * hw], nh, n),
        _to_t(w_full[:, c0:c0 + hw], nh, n), w_full[:, c0 + hw:],
        _pad_to(w_full[:, 3 * hw:3 * hw + lw], LORA_PAD, 1), _pad_to(w_full[:, 3 * hw + lw:c0], LORA_PAD, 1)], axis=1)
    w_out_f = w_out_g.reshape(N_DEV * w_out.shape[0], d)
    w_out_p = jnp.concatenate([_to_t(w_out_f[:hw], nh, n, axis=0), w_out_f[hw:]], axis=0)
    w_pw2_f = w_pw2_g.reshape(N_DEV * w_pw2.shape[0], cw)
    wup_w = _pad_to(_to_t(wup_w_g.transpose(1, 0, 2).reshape(lw, hw), nh, n), LORA_PAD, 0)
    wup_a = _pad_to(_to_t(wup_a_g.transpose(1, 0, 2).reshape(la, hw), nh, n), LORA_PAD, 0)
    conv_w_f = conv_w_g.transpose(1, 0, 2).reshape(taps, cw)
    mu_p = row(jnp.concatenate([
        _to_t(mu_shift[0:hw], nh, n), _to_t(mu_shift[hw:2 * hw], nh, n), _to_t(mu_shift[2 * hw:3 * hw], nh, n)]))
    mu_lo = row(jnp.concatenate([
        _pad_to(mu_shift[3 * hw:3 * hw + lw], LORA_PAD, 0), _pad_to(mu_shift[3 * hw + lw:], LORA_PAD, 0)]))
    tvec = lambda v: row(_to_t(v, nh, n))
    w0_t, a0_t, kk_t, ka_t, lg_t, lb_t = tvec(w0), tvec(a0), tvec(k_k), tvec(k_a), tvec(lnx_g), tvec(lnx_b)
    rk_t = row(r_k.T)

    c3 = lambda v: v.reshape(t, q, LANE)
    c2d = lambda v: v.reshape(t * q, LANE)
    me = 4 * lax.axis_index("x") + 2 * lax.axis_index("y") + lax.axis_index("c")
    own = lambda parts: lax.dynamic_index_in_dim(parts, me, 0, keepdims=False)
    wire = lambda parts, nm: _cast(parts.reshape(-1, parts.shape[-1]), WIRE_DTYPE, nm).reshape(parts.shape)

    h = _prenorm(x2, row(norm_pre_g))
    proj = _matmul(h, wp, "nn", F32, "mm_proj")
    r_a, w_a, kh_a, v_a, kn_a, b_a = _rwkv_pre(
        proj, mu_p, mu_lo, w0_t, a0_t, kk_t, ka_t, wup_w, wup_a, hw, nh, lo_blk)
    y_c, sa_c, ck = _wkv_fwd(c3(w_a), c3(b_a), c3(kh_a), c3(r_a), c3(kn_a), c3(v_a), nh)
    y_a = c2d(y_c)
    y_rwkv = _rwkv_post(y_a, r_a, kh_a, v_a, proj, lg_t, lb_t, rk_t, hw, n, gate_blk)
    cs = _conv_fwd(proj, conv_w_f, row(conv_b), row(cln_g), row(cln_b), cw, gv_blk, gg_blk)
    c2 = _matmul(cs, w_pw2_f, "nn", F32, "mm_pw2")
    y_conv = _conv_gate(c2, proj, row(b_pw2), cw, gc_blk)
    mix = jnp.concatenate([y_rwkv, y_conv], axis=1)
    out = _matmul(mix, w_out_p, "nn", F32, "mm_out")
    dout, dy, loss_part, d_post_g = _post_loss(out, x2, tgt2, row(norm_post_g))

    dmix = _matmul(dout, w_out_p, "nt", F32, "mm_dmix")
    d_w_out_p = _matmul(mix, dout, "tn", F32, "mm_dw_out")
    dc2, dproj, d_b_pw2 = _conv_gate_bwd(dmix, c2, proj, row(b_pw2), cw, 1, gc_blk)
    dcs = _matmul(dc2, w_pw2_f, "nt", F32, "mm_dcs")
    d_w_pw2 = _matmul(cs, dc2, "tn", F32, "mm_dw_pw2")
    dc, d_conv_w, d_conv_b, d_cln_g, d_cln_b = _conv_bwd_norm(
        proj, dcs, conv_w_f, row(conv_b), row(cln_g), row(cln_b), cw, gv_blk, gg_blk)
    dproj = _conv_bwd_glu(dc, proj, conv_w_f, cw, gv_blk, gg_blk, dproj)
    dproj, dy_rec, dr_bon, dkh_bon, dv_bon, d_lg_t, d_lb_t, d_rk_t = _rwkv_post_bwd(
        dmix, y_a, r_a, kh_a, v_a, proj, lg_t, lb_t, rk_t, hw, n, gate_blk, dproj)
    ck_i = ck.reshape(ck.shape[0], n, n, nh).transpose(0, 2, 1, 3).reshape(ck.shape)
    d_w_out_f = jnp.concatenate([_from_t(d_w_out_p[:hw], nh, n, axis=0), d_w_out_p[hw:]], axis=0)
    d_w_out_parts = d_w_out_f.reshape((N_DEV,) + w_out.shape)
    d_w_pw2_parts = d_w_pw2.reshape((N_DEV,) + w_pw2.shape)
    dv_c, dr_c, dw_c, db_c, dk_c, dkn_c, recv_w_out, recv_w_pw2 = _wkv_bwd(
        c3(r_a), c3(w_a), c3(b_a), c3(kh_a), c3(kn_a), c3(dy_rec), sa_c, c3(v_a), ck_i, nh,
        exchange=[wire(d_w_out_parts, "wire_w_out"), wire(d_w_pw2_parts, "wire_w_pw2")])
    dxs, dxs_lo, d_mu_p, d_mu_lo, d_w0_t, d_a0_t, d_kk_t, d_ka_t, d_wup_w, d_wup_a = _rwkv_pre_bwd(
        proj, mu_p, mu_lo, w0_t, a0_t, kk_t, ka_t, wup_w, wup_a, hw, nh, lo_blk,
        c2d(dr_c), c2d(dw_c), c2d(dk_c), c2d(dv_c), c2d(dkn_c), c2d(db_c), dr_bon, dkh_bon, dv_bon)
    dproj = _shift_bwd(dxs, mu_p, "shift_bwd", dproj, 0)
    dproj = _shift_bwd(dxs_lo, mu_lo, "shift_bwd_lora", dproj, lo_blk)
    colparts = lambda v: v.reshape(v.shape[0], N_DEV, v.shape[1] // N_DEV).transpose(1, 0, 2)
    d_wup_w_parts = colparts(_from_t(d_wup_w[:lw], nh, n))
    d_wup_a_parts = colparts(_from_t(d_wup_a[:la], nh, n))
    d_conv_w_parts = colparts(d_conv_w)
    d_wp, recv_wup_w, recv_wup_a, recv_conv_w = _matmul(
        h, dproj, "tn", F32, "mm_dw_in", exchange=[d_wup_w_parts, d_wup_a_parts, d_conv_w_parts])

    d_w_full = jnp.concatenate([
        _from_t(d_wp[:, 0:hw], nh, n), _from_t(d_wp[:, hw:2 * hw], nh, n), _from_t(d_wp[:, 2 * hw:3 * hw], nh, n),
        d_wp[:, lo0:lo0 + lw], d_wp[:, lo0 + LORA_PAD:lo0 + LORA_PAD + la],
        _from_t(d_wp[:, 3 * hw:4 * hw], nh, n), d_wp[:, 4 * hw:lo0]], axis=1)
    sc = w_in.shape[1]
    by_core = d_w_full.reshape(d, N_CHIPS, 2, sc).transpose(2, 1, 0, 3)
    core = lax.axis_index("c")
    for_mine = lax.dynamic_index_in_dim(by_core, core, 0, keepdims=False)
    for_sibling = lax.dynamic_index_in_dim(by_core, 1 - core, 0, keepdims=False)
    (from_sibling,) = _exchange([wire(for_sibling, "wire_w_in")], "exchange_sibling", "sibling")
    chip_f32, chip_wire = _chip_sum(for_mine, from_sibling)
    dh, recv_w_in = _matmul(dproj, wp, "nt", F32, "mm_dh", exchange=[chip_wire], pattern="chips")
    own_w_in = lax.dynamic_index_in_dim(chip_f32, 2 * lax.axis_index("x") + lax.axis_index("y"), 0, keepdims=False)
    grad_x, d_pre_g = _prenorm_bwd(dh, x2, dy, row(norm_pre_g))
    d_mu = jnp.concatenate([
        _from_t(d_mu_p[0, 0:hw], nh, n), _from_t(d_mu_p[0, hw:2 * hw], nh, n), _from_t(d_mu_p[0, 2 * hw:3 * hw], nh, n),
        d_mu_lo[0, 0:lw], d_mu_lo[0, LORA_PAD:LORA_PAD + la]])
    ft = lambda v: _from_t(v[0], nh, n)
    small = {
        "norm_pre_g": d_pre_g[0], "mu_shift": d_mu, "w0": ft(d_w0_t), "a0": ft(d_a0_t), "k_k": ft(d_kk_t),
        "k_a": ft(d_ka_t), "r_k": d_rk_t[0].reshape(n, nh).T.reshape(-1), "lnx_g": ft(d_lg_t), "lnx_b": ft(d_lb_t),
        "conv_b": d_conv_b[0], "cln_g": d_cln_g[0], "cln_b": d_cln_b[0], "b_pw2": d_b_pw2[0],
        "norm_post_g": d_post_g[0]}
    small_names = list(small)
    packed = jnp.concatenate([small[k] for k in small_names] + [loss_part[0, 0:1]])
    plen = packed.shape[0]
    ppad = -(-plen // LANE) * LANE
    packed = _pad_to(packed, ppad, 0).reshape(1, ppad)

    (packed_all,) = _all_gather([packed], "gather_small")

    res = {}
    sharded = [("w_in", own_w_in, recv_w_in), ("w_out", own(d_w_out_parts), recv_w_out),
               ("w_pw2", own(d_w_pw2_parts), recv_w_pw2), ("w_lora_up", own(d_wup_w_parts), recv_wup_w),
               ("a_lora_up", own(d_wup_a_parts), recv_wup_a), ("conv_w", own(d_conv_w_parts), recv_conv_w)]
    for nm, mine, rc in sharded:
        res[nm] = _adamw(args[nm], args["m_" + nm], args["v_" + nm], rc, mine, "adamw_" + nm)
    w_small = _pad_to(jnp.concatenate([args[k].reshape(-1) for k in small_names]), ppad, 0).reshape(1, ppad)
    m_small = _pad_to(jnp.concatenate([args["m_" + k].reshape(-1) for k in small_names]), ppad, 0).reshape(1, ppad)
    v_small = _pad_to(jnp.concatenate([args["v_" + k].reshape(-1) for k in small_names]), ppad, 0).reshape(1, ppad)
    g_s, d_s, m_s, v_s = _adamw(w_small, m_small, v_small, packed_all, None, "adamw_small")
    off = 0
    for k in small_names:
        size = args[k].size
        res[k] = tuple(o[0, off:off + size].reshape(args[k].shape) for o in (g_s, d_s, m_s, v_s))
        off += size
    loss = g_s[0, plen - 1]

    order = ["norm_pre_g", "w_in", "mu_shift", "w0", "w_lora_up", "a0", "a_lora_up", "k_k", "k_a", "r_k",
             "lnx_g", "lnx_b", "conv_w", "conv_b", "cln_g", "cln_b", "w_pw2", "b_pw2", "w_out", "norm_post_g"]
    outs = [loss, grad_x[None]]
    for slot in range(4):
        outs += [res[k][slot] for k in order]
    return tuple(outs)
```

```python
import functools

import jax
import jax.numpy as jnp
from jax import lax
from jax.experimental import pallas as pl
from jax.experimental.pallas import tpu as pltpu

F32 = jnp.float32
MXU_DTYPE = jnp.bfloat16
WIRE_DTYPE = jnp.bfloat16
HI = lax.Precision.HIGHEST

NORM_EPS = 1e-6
LN_EPS = 1e-5
GN_EPS_PER_CHANNEL = 1e-5
KK_EPS = 1e-12
ADAM_LR = 0.001
ADAM_B1 = 0.9
ADAM_B2 = 0.999
ADAM_EPS = 1e-08
ADAM_WD = 0.01
ADAM_STEP = 10

LANE = 128
SUBLANE = 8
LORA_PAD = 128
CONV_HALO = 32
N_DEV = 8
VMEM_LIMIT = 56 * 1024 * 1024
WKV_CHUNK = 16
RWKV_ROWS = 64
MESH = pl.DeviceIdType.MESH


def _tile(n, target, mult):
    if n <= target:
        return n
    best = None
    for d in range(mult, target + 1, mult):
        if n % d == 0:
            best = d
    assert best is not None, (n, target, mult)
    return best


def _cparams(sem=None):
    return pltpu.CompilerParams(dimension_semantics=sem, vmem_limit_bytes=VMEM_LIMIT)


def _sigmoid(x):
    return 1.0 / (1.0 + jnp.exp(-x))


def _full(shape):
    nd = len(shape)
    return pl.BlockSpec(shape, lambda *_: (0,) * nd)


def _compact_spec(tt, q):
    return pl.BlockSpec((tt * q, LANE), lambda i: (i, 0))


def _load_compact(ref, tt, q):
    return jnp.concatenate([ref[pl.ds(p, tt, stride=q), :] for p in range(q)], axis=1)


def _store_compact(ref, val, tt, q):
    for p in range(q):
        ref[pl.ds(p, tt, stride=q), :] = val[:, p * LANE:(p + 1) * LANE]


def _matmul(a, b, mode, out_dtype, name, exchange=(), pattern="direct", gather=()):
    if mode == "nn":
        (m, k), (k2, n) = a.shape, b.shape
    elif mode == "nt":
        (m, k), (n, k2) = a.shape, b.shape
    else:
        (k, m), (k2, n) = a.shape, b.shape
    assert k == k2, (a.shape, b.shape, mode)
    tm, tn, tk = _tile(m, 1024, LANE), _tile(n, 768, LANE), _tile(k, 2048, LANE)
    nk = k // tk
    ne = len(exchange)
    grid = (m // tm, n // tn, nk)
    if mode == "nn":
        a_spec = pl.BlockSpec((tm, tk), lambda i, j, kk: (i, kk))
        b_spec = pl.BlockSpec((tk, tn), lambda i, j, kk: (kk, j))
        dims = (((1,), (0,)), ((), ()))
    elif mode == "nt":
        a_spec = pl.BlockSpec((tm, tk), lambda i, j, kk: (i, kk))
        b_spec = pl.BlockSpec((tn, tk), lambda i, j, kk: (j, kk))
        dims = (((1,), (1,)), ((), ()))
    else:
        a_spec = pl.BlockSpec((tk, tm), lambda i, j, kk: (kk, i))
        b_spec = pl.BlockSpec((tk, tn), lambda i, j, kk: (kk, j))
        dims = (((0,), (0,)), ((), ()))

    ng = len(gather)
    nx = ne + ng

    def body(*refs):
        a_ref, b_ref = refs[:2]
        parts, shards = refs[2:2 + ne], refs[2 + ne:2 + nx]
        o_ref = refs[2 + nx]
        recvs, gathered = refs[3 + nx:3 + nx + ne], refs[3 + nx + ne:3 + 2 * nx]
        acc_ref = refs[3 + 2 * nx]
        sems = refs[4 + 2 * nx:]
        i, j, kk = pl.program_id(0), pl.program_id(1), pl.program_id(2)
        first_step = (i == 0) & (j == 0) & (kk == 0)
        last_step = (i == grid[0] - 1) & (j == grid[1] - 1) & (kk == nk - 1)
        if ne:
            copies = _exchange_copies(parts, recvs, *sems[:2], pattern=pattern)
        if ng:
            gather_starts, gather_finish = _gather_plan(shards, gathered, *sems[2 if ne else 0:])

        @pl.when(first_step)
        def _():
            for cp in (copies if ne else []) + (gather_starts if ng else []):
                cp.start()

        @pl.when(kk == 0)
        def _():
            acc_ref[...] = jnp.zeros_like(acc_ref)

        acc_ref[...] += lax.dot_general(a_ref[...], b_ref[...], dims, preferred_element_type=F32)

        @pl.when(kk == nk - 1)
        def _():
            o_ref[...] = acc_ref[...].astype(o_ref.dtype)

        @pl.when(last_step)
        def _():
            for cp in (copies if ne else []):
                cp.wait()
            if ng:
                gather_finish()

    res = pl.pallas_call(
        body, name=name,
        grid=grid,
        in_specs=[a_spec, b_spec] + [HBM_SPEC] * nx,
        out_specs=[pl.BlockSpec((tm, tn), lambda i, j, kk: (i, j))] + [HBM_SPEC] * nx,
        out_shape=[jax.ShapeDtypeStruct((m, n), out_dtype)] + _exchange_shapes(exchange, pattern)
        + _gather_shapes(gather),
        scratch_shapes=[pltpu.VMEM((tm, tn), F32)] + _exchange_sems(ne, pattern) + _gather_sems(ng),
        compiler_params=_cparams(("arbitrary",) * 3 if nx else ("parallel", "parallel", "arbitrary")),
    )(a, b, *exchange, *gather)
    return res if nx else res[0]


def _prenorm(x, g):
    t, d = x.shape
    tt = _tile(t, 256, SUBLANE)

    def body(x_ref, g_ref, h_ref):
        xv = x_ref[...]
        rinv = lax.rsqrt(jnp.mean(xv * xv, axis=-1, keepdims=True) + NORM_EPS)
        h_ref[...] = (xv * rinv * g_ref[...]).astype(h_ref.dtype)

    return pl.pallas_call(
        body, name="prenorm", grid=(t // tt,),
        in_specs=[pl.BlockSpec((tt, d), lambda i: (i, 0)), _full((1, d))],
        out_specs=pl.BlockSpec((tt, d), lambda i: (i, 0)),
        out_shape=jax.ShapeDtypeStruct((t, d), MXU_DTYPE),
        compiler_params=_cparams(("parallel",)),
    )(x, g)


def _post_loss(out, x, target, g):
    t, d = out.shape
    tt = _tile(t, 128, SUBLANE)

    def body(o_ref, x_ref, t_ref, g_ref, dout_ref, dy_ref, loss_ref, dg_ref):
        i = pl.program_id(0)
        ov = o_ref[...]
        rinv = lax.rsqrt(jnp.mean(ov * ov, axis=-1, keepdims=True) + NORM_EPS)
        nv = ov * rinv
        gv = g_ref[...]
        err = x_ref[...] + nv * gv - t_ref[...]
        part = 0.5 * jnp.sum(jnp.mean(err * err, axis=-1, keepdims=True), axis=0, keepdims=True)
        dy = err * (1.0 / d)
        dy_ref[...] = dy
        dn = dy * gv
        dout = rinv * (dn - nv * jnp.mean(dn * nv, axis=-1, keepdims=True))
        dout_ref[...] = dout.astype(dout_ref.dtype)
        dg = jnp.sum(dy * nv, axis=0, keepdims=True)

        @pl.when(i == 0)
        def _():
            loss_ref[...] = jnp.zeros_like(loss_ref)
            dg_ref[...] = jnp.zeros_like(dg_ref)

        loss_ref[...] += jnp.broadcast_to(part, loss_ref.shape)
        dg_ref[...] += dg

    row = pl.BlockSpec((tt, d), lambda i: (i, 0))
    return pl.pallas_call(
        body, name="post_loss", grid=(t // tt,),
        in_specs=[row, row, row, _full((1, d))],
        out_specs=[row, row, _full((1, LANE)), _full((1, d))],
        out_shape=[jax.ShapeDtypeStruct((t, d), MXU_DTYPE), jax.ShapeDtypeStruct((t, d), F32),
                   jax.ShapeDtypeStruct((1, LANE), F32), jax.ShapeDtypeStruct((1, d), F32)],
        compiler_params=_cparams(("arbitrary",)),
    )(out, x, target, g)


def _prenorm_bwd(dh, x, dy, g):
    t, d = x.shape
    tt = _tile(t, 128, SUBLANE)

    def body(dh_ref, x_ref, dy_ref, g_ref, gx_ref, dg_ref):
        i = pl.program_id(0)
        xv = x_ref[...]
        rinv = lax.rsqrt(jnp.mean(xv * xv, axis=-1, keepdims=True) + NORM_EPS)
        nx = xv * rinv
        dhv = dh_ref[...]
        dnx = dhv * g_ref[...]
        dx = rinv * (dnx - nx * jnp.mean(dnx * nx, axis=-1, keepdims=True))
        gx_ref[...] = dy_ref[...] + dx

        @pl.when(i == 0)
        def _():
            dg_ref[...] = jnp.zeros_like(dg_ref)

        dg_ref[...] += jnp.sum(dhv * nx, axis=0, keepdims=True)

    row = pl.BlockSpec((tt, d), lambda i: (i, 0))
    return pl.pallas_call(
        body, name="prenorm_bwd", grid=(t // tt,),
        in_specs=[row, row, row, _full((1, d))],
        out_specs=[row, _full((1, d))],
        out_shape=[jax.ShapeDtypeStruct((t, d), F32), jax.ShapeDtypeStruct((1, d), F32)],
        compiler_params=_cparams(("arbitrary",)),
    )(dh, x, dy, g)


def _segsum(v, nh):
    q = v.shape[1] // LANE
    s = v[:, 0:LANE]
    for p in range(1, q):
        s = s + v[:, p * LANE:(p + 1) * LANE]
    shift = nh
    while shift < LANE:
        s = s + pltpu.roll(s, shift, 1)
        shift *= 2
    return jnp.concatenate([s] * q, axis=1)


def _shifted(cur_ref, prev_ref, lo, hi, first):
    cur = cur_ref[:, lo:hi]
    last = jnp.where(first, 0.0, prev_ref[SUBLANE - 1:SUBLANE, lo:hi])
    prev = pltpu.roll(cur, 1, 0)
    rows = lax.broadcasted_iota(jnp.int32, cur.shape, 0)
    return cur, jnp.where(rows == 0, last, prev)


def _rwkv_mix(main_ref, mainp_ref, lo_ref, lop_ref, mu_ref, mulo_ref, w0_ref, a0_ref, kk_ref, ka_ref,
              wupw_ref, wupa_ref, hw, nh, first):
    def xs(cur_ref, prev_ref, m_ref, lo, hi):
        cur, prev = _shifted(cur_ref, prev_ref, lo, hi, first)
        return cur + (prev - cur) * m_ref[:, lo:hi], prev - cur

    out = {}
    out["r"], out["r_d"] = xs(main_ref, mainp_ref, mu_ref, 0, hw)
    out["k"], out["k_d"] = xs(main_ref, mainp_ref, mu_ref, hw, 2 * hw)
    out["v"], out["v_d"] = xs(main_ref, mainp_ref, mu_ref, 2 * hw, 3 * hw)
    out["wl"], out["wl_d"] = xs(lo_ref, lop_ref, mulo_ref, 0, LORA_PAD)
    out["al"], out["al_d"] = xs(lo_ref, lop_ref, mulo_ref, LORA_PAD, 2 * LORA_PAD)
    th = jnp.tanh(out["wl"])
    zw = w0_ref[...] + jnp.dot(th, wupw_ref[...], preferred_element_type=F32, precision=HI)
    u = -zw
    softplus = jnp.maximum(u, 0.0) + jnp.log(1.0 + jnp.exp(-jnp.abs(u)))
    wlog = -softplus - 0.5
    ew = jnp.exp(wlog)
    za = a0_ref[...] + jnp.dot(out["al"], wupa_ref[...], preferred_element_type=F32, precision=HI)
    a = _sigmoid(za)
    kkr = out["k"] * kk_ref[...]
    nr = jnp.sqrt(_segsum(kkr * kkr, nh))
    nrm = jnp.maximum(nr, KK_EPS)
    out.update(th=th, zw=zw, ew=ew, decay=jnp.exp(-ew), a=a, kkr=kkr, nr=nr, nrm=nrm, kk=kkr / nrm)
    out["kh"] = out["k"] * (1.0 + (a - 1.0) * ka_ref[...])
    return out


def _mix_specs(tt, hw, lo_blk):
    mw, lw2 = 3 * hw, 2 * LORA_PAD
    before = lambda i: jnp.maximum(i * (tt // SUBLANE) - 1, 0)
    vec = _full((1, hw))
    return [pl.BlockSpec((tt, mw), lambda i: (i, 0)), pl.BlockSpec((SUBLANE, mw), lambda i: (before(i), 0)),
            pl.BlockSpec((tt, lw2), lambda i: (i, lo_blk)), pl.BlockSpec((SUBLANE, lw2), lambda i: (before(i), lo_blk)),
            _full((1, mw)), _full((1, lw2)), vec, vec, vec, vec, _full((LORA_PAD, hw)), _full((LORA_PAD, hw))]


def _rwkv_pre(proj, mu, mu_lo, w0, a0, k_k, k_a, wup_w, wup_a, hw, nh, lo_blk):
    t = proj.shape[0]
    tt = _tile(t, RWKV_ROWS, SUBLANE)

    def body(*refs):
        r_o, w_o, kh_o, v_o, kn_o, b_o = refs[-6:]
        f = _rwkv_mix(*refs[:-6], hw, nh, pl.program_id(0) == 0)
        for ref, val in ((r_o, f["r"]), (w_o, f["decay"]), (kh_o, f["kh"]), (v_o, f["v"]), (kn_o, -f["kk"]),
                         (b_o, f["kk"] * f["a"])):
            _store_compact(ref, val, tt, q)

    q = hw // LANE
    return pl.pallas_call(
        body, name="rwkv_pre", grid=(t // tt,),
        in_specs=_mix_specs(tt, hw, lo_blk),
        out_specs=[_compact_spec(tt, q)] * 6,
        out_shape=[jax.ShapeDtypeStruct((t * q, LANE), F32)] * 6,
        compiler_params=_cparams(("parallel",)),
    )(proj, proj, proj, proj, mu, mu_lo, w0, a0, k_k, k_a, wup_w, wup_a)


def _rwkv_post_math(y, r, kh, v, g, lnx_g, lnx_b, r_k, nh, n):
    mean = _segsum(y, nh) * (1.0 / n)
    yc = y - mean
    var = _segsum(yc * yc, nh) * (1.0 / n)
    rstd = lax.rsqrt(var + GN_EPS_PER_CHANNEL * n)
    yn = yc * rstd
    s = _segsum(r * kh * r_k, nh)
    y3 = yn * lnx_g + lnx_b + s * v
    sg = _sigmoid(g)
    return yn, rstd, s, y3, sg


def _rwkv_post(y, r, kh, v, proj, lnx_g, lnx_b, r_k, hw, n, gate_blk):
    t = proj.shape[0]
    tt = _tile(t, RWKV_ROWS, SUBLANE)
    q = hw // LANE
    nh = hw // n

    def body(y_ref, r_ref, kh_ref, v_ref, g_ref, lg_ref, lb_ref, rk_ref, o_ref):
        g = g_ref[...]
        y, r, kh, v = (_load_compact(ref, tt, q) for ref in (y_ref, r_ref, kh_ref, v_ref))
        _, _, _, y3, sg = _rwkv_post_math(y, r, kh, v, g, lg_ref[...], lb_ref[...], rk_ref[...], nh, n)
        o_ref[...] = (y3 * (g * sg)).astype(o_ref.dtype)

    row = pl.BlockSpec((tt, hw), lambda i: (i, 0))
    comp = _compact_spec(tt, q)
    vec = _full((1, hw))
    return pl.pallas_call(
        body, name="rwkv_post", grid=(t // tt,),
        in_specs=[comp, comp, comp, comp, pl.BlockSpec((tt, hw), lambda i: (i, gate_blk)), vec, vec, vec],
        out_specs=row,
        out_shape=jax.ShapeDtypeStruct((t, hw), MXU_DTYPE),
        compiler_params=_cparams(("parallel",)),
    )(y, r, kh, v, proj, lnx_g, lnx_b, r_k)


def _rwkv_post_bwd(dmix, y, r, kh, v, proj, lnx_g, lnx_b, r_k, hw, n, gate_blk, dproj):
    t = proj.shape[0]
    tt = _tile(t, RWKV_ROWS, SUBLANE)
    q = hw // LANE
    nh = hw // n

    def body(dm_ref, y_ref, r_ref, kh_ref, v_ref, g_ref, lg_ref, lb_ref, rk_ref, dproj_ref,
             dg_o, dy_o, dr_o, dkh_o, dv_o, dlg_o, dlb_o, drk_o):
        i = pl.program_id(0)
        g, rk, lg = g_ref[...], rk_ref[...], lg_ref[...]
        y, r, kh, v = (_load_compact(ref, tt, q) for ref in (y_ref, r_ref, kh_ref, v_ref))
        yn, rstd, s, y3, sg = _rwkv_post_math(y, r, kh, v, g, lg, lb_ref[...], rk, nh, n)
        dyr = dm_ref[...]
        dy3 = dyr * (g * sg)
        dg_o[...] = (dyr * y3 * (sg * (1.0 + g * (1.0 - sg)))).astype(dg_o.dtype)
        ds = _segsum(dy3 * v, nh)
        _store_compact(dv_o, dy3 * s, tt, q)
        _store_compact(dr_o, ds * kh * rk, tt, q)
        _store_compact(dkh_o, ds * r * rk, tt, q)
        dyn = dy3 * lg
        m1 = _segsum(dyn, nh) * (1.0 / n)
        m2 = _segsum(dyn * yn, nh) * (1.0 / n)
        _store_compact(dy_o, rstd * (dyn - m1 - yn * m2), tt, q)

        @pl.when(i == 0)
        def _():
            dlg_o[...] = jnp.zeros_like(dlg_o)
            dlb_o[...] = jnp.zeros_like(dlb_o)
            drk_o[...] = jnp.zeros_like(drk_o)

        dlg_o[...] += jnp.sum(dy3 * yn, axis=0, keepdims=True)
        dlb_o[...] += jnp.sum(dy3, axis=0, keepdims=True)
        drk_o[...] += jnp.sum(ds * r * kh, axis=0, keepdims=True)

    row = pl.BlockSpec((tt, hw), lambda i: (i, 0))
    comp = _compact_spec(tt, q)
    vec = _full((1, hw))
    rowf = jax.ShapeDtypeStruct((t * q, LANE), F32)
    vecf = jax.ShapeDtypeStruct((1, hw), F32)
    return pl.pallas_call(
        body, name="rwkv_post_bwd", grid=(t // tt,),
        in_specs=[row, comp, comp, comp, comp, pl.BlockSpec((tt, hw), lambda i: (i, gate_blk)), vec, vec, vec,
                  ANY_SPEC],
        out_specs=[pl.BlockSpec((tt, hw), lambda i: (i, gate_blk)), comp, comp, comp, comp, vec, vec, vec],
        out_shape=[jax.ShapeDtypeStruct(dproj.shape, dproj.dtype), rowf, rowf, rowf, rowf, vecf, vecf, vecf],
        input_output_aliases={9: 0},
        compiler_params=_cparams(("arbitrary",)),
    )(dmix, y, r, kh, v, proj, lnx_g, lnx_b, r_k, dproj)


def _rwkv_pre_bwd(proj, mu, mu_lo, w0, a0, k_k, k_a, wup_w, wup_a, hw, nh, lo_blk,
                  dr_rec, dw_rec, dkh_rec, dv_rec, dkn_rec, db_rec, dr_bon, dkh_bon, dv_bon):
    t = proj.shape[0]
    mw, lw2 = 3 * hw, 2 * LORA_PAD
    tt = _tile(t, RWKV_ROWS, SUBLANE)
    q = hw // LANE
    n_in = 12

    def body(*refs):
        mix_refs = refs[:n_in]
        drr, dwr, dkhr, dvr, dknr, db, drb, dkhb, dvb = (
            _load_compact(ref, tt, q) for ref in refs[n_in:n_in + 9])
        dxs_o, dxl_o, dmu_o, dmul_o, dw0_o, da0_o, dkk_o, dka_o, dwupw_o, dwupa_o = refs[n_in + 9:]
        kk_ref, ka_ref, wupw_ref, wupa_ref = mix_refs[8:12]
        i = pl.program_id(0)
        f = _rwkv_mix(*mix_refs, hw, nh, i == 0)
        k, a, kk, nrm = f["k"], f["a"], f["kk"], f["nrm"]
        k_a, k_k = ka_ref[...], kk_ref[...]
        dr = drr + drb
        dkh = dkhr + dkhb
        dv = dvr + dvb
        da = db * kk + dkh * k * k_a
        dkk = db * a - dknr
        dk = dkh * (1.0 + (a - 1.0) * k_a)
        dka = jnp.sum(dkh * k * (a - 1.0), axis=0, keepdims=True)
        proj_kk = _segsum(dkk * kk, nh)
        dkkr = jnp.where(f["nr"] > KK_EPS, (dkk - kk * proj_kk) / nrm, dkk * (1.0 / KK_EPS))
        dk = dk + dkkr * k_k
        dkk_w = jnp.sum(dkkr * k, axis=0, keepdims=True)
        dza = da * a * (1.0 - a)
        dzw = dwr * f["decay"] * (-f["ew"]) * _sigmoid(-f["zw"])
        nt_dims = (((1,), (1,)), ((), ()))
        tn_dims = (((0,), (0,)), ((), ()))
        dal = lax.dot_general(dza, wupa_ref[...], nt_dims, preferred_element_type=F32, precision=HI)
        dth = lax.dot_general(dzw, wupw_ref[...], nt_dims, preferred_element_type=F32, precision=HI)
        dwl = dth * (1.0 - f["th"] * f["th"])
        dxs_o[:, 0:hw] = dr
        dxs_o[:, hw:2 * hw] = dk
        dxs_o[:, 2 * hw:3 * hw] = dv
        dxl_o[:, 0:LORA_PAD] = dwl
        dxl_o[:, LORA_PAD:lw2] = dal

        @pl.when(i == 0)
        def _():
            for ref in (dmu_o, dmul_o, dw0_o, da0_o, dkk_o, dka_o, dwupw_o, dwupa_o):
                ref[...] = jnp.zeros_like(ref)

        def colsum(v):
            return jnp.sum(v, axis=0, keepdims=True)

        dmu_o[:, 0:hw] += colsum(dr * f["r_d"])
        dmu_o[:, hw:2 * hw] += colsum(dk * f["k_d"])
        dmu_o[:, 2 * hw:3 * hw] += colsum(dv * f["v_d"])
        dmul_o[:, 0:LORA_PAD] += colsum(dwl * f["wl_d"])
        dmul_o[:, LORA_PAD:lw2] += colsum(dal * f["al_d"])
        dw0_o[...] += colsum(dzw)
        da0_o[...] += colsum(dza)
        dkk_o[...] += dkk_w
        dka_o[...] += dka
        dwupw_o[...] += lax.dot_general(f["th"], dzw, tn_dims, preferred_element_type=F32, precision=HI)
        dwupa_o[...] += lax.dot_general(f["al"], dza, tn_dims, preferred_element_type=F32, precision=HI)

    vec = _full((1, hw))
    vecf = jax.ShapeDtypeStruct((1, hw), F32)
    return pl.pallas_call(
        body, name="rwkv_pre_bwd", grid=(t // tt,),
        in_specs=_mix_specs(tt, hw, lo_blk) + [_compact_spec(tt, q)] * 9,
        out_specs=[pl.BlockSpec((tt, mw), lambda i: (i, 0)), pl.BlockSpec((tt, lw2), lambda i: (i, 0)),
                   _full((1, mw)), _full((1, lw2)), vec, vec, vec, vec,
                   _full((LORA_PAD, hw)), _full((LORA_PAD, hw))],
        out_shape=[jax.ShapeDtypeStruct((t, mw), F32), jax.ShapeDtypeStruct((t, lw2), F32),
                   jax.ShapeDtypeStruct((1, mw), F32), jax.ShapeDtypeStruct((1, lw2), F32),
                   vecf, vecf, vecf, vecf,
                   jax.ShapeDtypeStruct((LORA_PAD, hw), F32), jax.ShapeDtypeStruct((LORA_PAD, hw), F32)],
        compiler_params=_cparams(("arbitrary",)),
    )(proj, proj, proj, proj, mu, mu_lo, w0, a0, k_k, k_a, wup_w, wup_a,
      dr_rec, dw_rec, dkh_rec, dv_rec, dkn_rec, db_rec, dr_bon, dkh_bon, dv_bon)


def _shift_bwd(dxs, mu, name, dproj, col_blk):
    t, sw = dxs.shape
    tt = _tile(t, 256, SUBLANE)
    nblk = t // SUBLANE

    def body(d_ref, nxt_ref, mu_ref, dproj_ref, o_ref):
        last = pl.program_id(0) == pl.num_programs(0) - 1
        cur = d_ref[...]
        first_next = jnp.where(last, 0.0, nxt_ref[0:1, :])
        nxt = pltpu.roll(cur, tt - 1, 0)
        rows = lax.broadcasted_iota(jnp.int32, cur.shape, 0)
        nxt = jnp.where(rows == tt - 1, first_next, nxt)
        m = mu_ref[...]
        o_ref[...] = (cur * (1.0 - m) + nxt * m).astype(o_ref.dtype)

    return pl.pallas_call(
        body, name=name, grid=(t // tt,),
        in_specs=[pl.BlockSpec((tt, sw), lambda i: (i, 0)),
                  pl.BlockSpec((SUBLANE, sw), lambda i: (jnp.minimum((i + 1) * (tt // SUBLANE), nblk - 1), 0)),
                  _full((1, sw)), ANY_SPEC],
        out_specs=pl.BlockSpec((tt, sw), lambda i: (i, col_blk)),
        out_shape=jax.ShapeDtypeStruct(dproj.shape, dproj.dtype),
        input_output_aliases={3: 0},
        compiler_params=_cparams(("parallel",)),
    )(dxs, dxs, mu, dproj)


def _tree_sum(parts):
    while len(parts) > 1:
        parts = [parts[p] + parts[p + 1] for p in range(0, len(parts) - 1, 2)] + ([parts[-1]] if len(parts) % 2 else [])
    return parts[0]


def _tile_rows(src_ref, dst_ref, tc, nh):
    def convert(ts, carry):
        _tile_step(src_ref, dst_ref, ts, nh)
        return carry

    lax.fori_loop(0, tc, convert, 0)


def _tile_step(src_ref, dst_ref, ts, nh):
    for grp, m in enumerate(_tiled(src_ref[ts], nh)):
        dst_ref[ts, grp] = m


def _tiled(v, nh):
    rep = LANE // nh
    lane_group = lax.broadcasted_iota(jnp.int32, v.shape, 1) // nh
    rolled = [v] + [pltpu.roll(v, k * nh, 1) for k in range(1, rep)]
    out = []
    for grp in range(rep):
        m = rolled[(0 - grp) % rep]
        for g in range(1, rep):
            m = jnp.where(lane_group == g, rolled[(g - grp) % rep], m)
        out.append(m)
    return out


def _wkv_fwd(w_c, b_c, k_c, r_c, kn_c, v_c, nh):
    t, q, _ = v_c.shape
    rep = LANE // nh
    n = q * rep
    tc = _tile(t, WKV_CHUNK, 1)
    nc = t // tc
    nacc = 4
    nv = 5

    def body(*refs):
        cur, v_ref, nxt = refs[:nv], refs[nv], refs[nv + 1:2 * nv + 1]
        y_ref, sa_ref, ck_ref, s_ref = refs[2 * nv + 1:2 * nv + 5]
        tiles_even, tiles_odd = refs[2 * nv + 5:3 * nv + 5], refs[3 * nv + 5:]
        c = pl.program_id(0)

        @pl.when(c == 0)
        def _():
            s_ref[...] = jnp.zeros_like(s_ref)
            for src, dst in zip(cur, tiles_even):
                _tile_rows(src, dst, tc, nh)

        ck_ref[0] = s_ref[...]

        def row(ref, ts, j):
            return ref[ts, j % rep, pl.ds(j // rep, 1), :]

        def run(mine, ahead_tiles):
            wt, bt, kt, rt, knt = mine

            def step(ts, carry):
                for src, dst in zip(nxt, ahead_tiles):
                    _tile_step(src, dst, ts, nh)
                vt = v_ref[ts]
                acc = [None] * nacc
                for j in range(n):
                    term = s_ref[j] * row(knt, ts, j)
                    acc[j % nacc] = term if acc[j % nacc] is None else acc[j % nacc] + term
                sa = _tree_sum(acc)
                sa_ref[ts] = sa
                acc = [None] * nacc
                for j in range(n):
                    sj = s_ref[j] * row(wt, ts, j) + sa * row(bt, ts, j) + vt * row(kt, ts, j)
                    s_ref[j] = sj
                    term = sj * row(rt, ts, j)
                    acc[j % nacc] = term if acc[j % nacc] is None else acc[j % nacc] + term
                y_ref[ts] = _tree_sum(acc)
                return carry

            lax.fori_loop(0, tc, step, 0)

        @pl.when(c % 2 == 0)
        def _():
            run(tiles_even, tiles_odd)

        @pl.when(c % 2 == 1)
        def _():
            run(tiles_odd, tiles_even)

    comp = pl.BlockSpec((tc, q, LANE), lambda c: (c, 0, 0))
    ahead = pl.BlockSpec((tc, q, LANE), lambda c: (jnp.minimum(c + 1, nc - 1), 0, 0))
    return pl.pallas_call(
        body, name="wkv_fwd", grid=(nc,),
        in_specs=[comp] * (nv + 1) + [ahead] * nv,
        out_specs=[comp, comp, pl.BlockSpec((1, n, q, LANE), lambda c: (c, 0, 0, 0))],
        out_shape=[jax.ShapeDtypeStruct((t, q, LANE), F32), jax.ShapeDtypeStruct((t, q, LANE), F32),
                   jax.ShapeDtypeStruct((nc, n, q, LANE), F32)],
        scratch_shapes=[pltpu.VMEM((n, q, LANE), F32)] + [pltpu.VMEM((tc, rep, q, LANE), F32)] * (2 * nv),
        compiler_params=_cparams(("arbitrary",)),
    )(w_c, b_c, k_c, r_c, kn_c, v_c, w_c, b_c, k_c, r_c, kn_c)


def _wkv_bwd(r_c, w_c, b_c, k_c, kn_c, dy_c, sa_c, v_c, ck_i, nh, exchange=()):
    t, q, _ = dy_c.shape
    rep = LANE // nh
    n = q * rep
    tc = _tile(t, WKV_CHUNK, 1)
    nc = t // tc
    nacc = 2
    ne = len(exchange)
    nv = 8
    n_in = 2 * nv + 1 + ne

    def body(*refs):
        cur, ck_ref, nxt = refs[:nv], refs[nv], refs[nv + 1:2 * nv + 1]
        rc_ref, wc_ref, bc_ref, kc_ref, knc_ref, dyc_ref = cur[:6]
        parts = refs[2 * nv + 1:n_in]
        dv_o, dr_o, dw_o, db_o, dk_o, dkn_o = refs[n_in:n_in + 6]
        recvs = refs[n_in + 6:n_in + 6 + ne]
        hist, g_ref, gp_ref, dsat_ref = refs[n_in + 6 + ne:n_in + 10 + ne]
        tiles_even = refs[n_in + 10 + ne:n_in + 10 + ne + nv]
        tiles_odd = refs[n_in + 10 + ne + nv:n_in + 10 + ne + 2 * nv]
        c = pl.program_id(0)
        if ne:
            copies = _exchange_copies(parts, recvs, *refs[n_in + 10 + ne + 2 * nv:])

            @pl.when(c == 0)
            def _():
                for cp in copies:
                    cp.start()

        @pl.when(c == 0)
        def _():
            g_ref[...] = jnp.zeros_like(g_ref)
            gp_ref[...] = jnp.zeros_like(gp_ref)
            for src, dst in zip(cur, tiles_even):
                _tile_rows(src, dst, tc, nh)

        def row(ref, ts, idx):
            return ref[ts, idx % rep, pl.ds(idx // rep, 1), :]

        hist[0] = ck_ref[0]

        def run(mine, ahead_tiles):
            rt, wt, bt, kt, knt, dyt, sat, vt = mine

            def fstep(ts, carry):
                wv, bv, kv = wc_ref[ts], bc_ref[ts], kc_ref[ts]
                for i in range(n):
                    hist[ts + 1, i] = hist[ts, i] * wv + row(sat, ts, i) * bv + row(vt, ts, i) * kv
                return carry

            lax.fori_loop(0, tc, fstep, 0)

            def bstep(s, carry):
                ts = tc - 1 - s
                for src, dst in zip(nxt, ahead_tiles):
                    _tile_step(src, dst, ts, nh)
                dy = dyc_ref[ts]
                acc_sa, acc_v = [None] * nacc, [None] * nacc
                for j in range(n):
                    gj = g_ref[j] + dy * row(rt, ts, j)
                    g_ref[j] = gj
                    t1 = gj * row(bt, ts, j)
                    t2 = gj * row(kt, ts, j)
                    a = j % nacc
                    acc_sa[a] = t1 if acc_sa[a] is None else acc_sa[a] + t1
                    acc_v[a] = t2 if acc_v[a] is None else acc_v[a] + t2
                dsa = _tree_sum(acc_sa)
                dv_o[ts] = _tree_sum(acc_v)
                for j in range(n):
                    g_ref[j] = g_ref[j] * row(wt, ts, j) + dsa * row(knt, ts, j)
                for grp, m in enumerate(_tiled(dsa, nh)):
                    dsat_ref[grp] = m
                rv, wv, knv = rc_ref[ts], wc_ref[ts], knc_ref[ts]
                names = ("dr", "dw", "db", "dk", "dkn")
                accs = {nm: [None] * nacc for nm in names}
                for i in range(n):
                    dsai = dsat_ref[i % rep, pl.ds(i // rep, 1), :]
                    dyi = row(dyt, ts, i)
                    s_prev = hist[ts, i]
                    gi = gp_ref[i] + dyi * rv
                    terms = {"dr": hist[ts + 1, i] * dyi, "dw": gi * s_prev, "db": gi * row(sat, ts, i),
                             "dk": gi * row(vt, ts, i), "dkn": dsai * s_prev}
                    a = i % nacc
                    for nm in names:
                        accs[nm][a] = terms[nm] if accs[nm][a] is None else accs[nm][a] + terms[nm]
                    gp_ref[i] = gi * wv + dsai * knv
                dr_o[ts] = _tree_sum(accs["dr"])
                dw_o[ts] = _tree_sum(accs["dw"])
                db_o[ts] = _tree_sum(accs["db"])
                dk_o[ts] = _tree_sum(accs["dk"])
                dkn_o[ts] = _tree_sum(accs["dkn"])
                return carry

            lax.fori_loop(0, tc, bstep, 0)

        @pl.when(c % 2 == 0)
        def _():
            run(tiles_even, tiles_odd)

        @pl.when(c % 2 == 1)
        def _():
            run(tiles_odd, tiles_even)

        if ne:
            @pl.when(c == nc - 1)
            def _():
                for cp in copies:
                    cp.wait()

    comp = pl.BlockSpec((tc, q, LANE), lambda c: (nc - 1 - c, 0, 0))
    ahead = pl.BlockSpec((tc, q, LANE), lambda c: (jnp.maximum(nc - 2 - c, 0), 0, 0))
    outc = jax.ShapeDtypeStruct((t, q, LANE), F32)
    vectors = (r_c, w_c, b_c, k_c, kn_c, dy_c, sa_c, v_c)
    return pl.pallas_call(
        body, name="wkv_bwd", grid=(nc,),
        in_specs=[comp] * nv + [pl.BlockSpec((1, n, q, LANE), lambda c: (nc - 1 - c, 0, 0, 0))] + [ahead] * nv
        + [HBM_SPEC] * ne,
        out_specs=[comp] * 6 + [HBM_SPEC] * ne,
        out_shape=[outc] * 6 + _exchange_shapes(exchange),
        scratch_shapes=[pltpu.VMEM((tc + 1, n, q, LANE), F32), pltpu.VMEM((n, q, LANE), F32),
                        pltpu.VMEM((n, q, LANE), F32), pltpu.VMEM((rep, q, LANE), F32)]
        + [pltpu.VMEM((tc, rep, q, LANE), F32)] * (2 * nv) + _exchange_sems(ne),
        compiler_params=_cparams(("arbitrary",)),
    )(*vectors, ck_i, *vectors, *exchange)


def _shift_copies(ext_ref, sh_ref):
    rows = ext_ref.shape[0] - SUBLANE
    for p in range(1, SUBLANE):
        sh_ref[p - 1, 0:rows, :] = ext_ref[p:p + rows, :]


def _window(ext_ref, sh_ref, start, size):
    p = start % SUBLANE
    if p == 0:
        return ext_ref[start:start + size, :]
    assert start - p + size <= ext_ref.shape[0] - SUBLANE
    return sh_ref[p - 1, start - p:start - p + size, :]


def _conv_stage(gv_ref, gg_ref, gvh_ref, ggh_ref, cw_ref, cb_ref, lg_ref, lb_ref, ext_ref, sh_ref, first, tt, taps):
    u = gv_ref[...] * _sigmoid(gg_ref[...])
    uh = jnp.where(first, 0.0, gvh_ref[...] * _sigmoid(ggh_ref[...]))
    ext_ref[0:CONV_HALO, :] = uh
    ext_ref[CONV_HALO:CONV_HALO + tt, :] = u
    _shift_copies(ext_ref, sh_ref)
    off = CONV_HALO - (taps - 1)
    c = cb_ref[...] + _window(ext_ref, sh_ref, off, tt) * cw_ref[0:1, :]
    for j in range(1, taps):
        c = c + _window(ext_ref, sh_ref, off + j, tt) * cw_ref[j:j + 1, :]
    mean = jnp.mean(c, axis=-1, keepdims=True)
    cc = c - mean
    rstd = lax.rsqrt(jnp.mean(cc * cc, axis=-1, keepdims=True) + LN_EPS)
    chat = cc * rstd
    cn = chat * lg_ref[...] + lb_ref[...]
    return chat, rstd, cn


def _conv_specs(t, tt, cw, taps, gv_blk, gg_blk):
    hb = tt // CONV_HALO
    return [pl.BlockSpec((tt, cw), lambda i: (i, gv_blk)), pl.BlockSpec((tt, cw), lambda i: (i, gg_blk)),
            pl.BlockSpec((CONV_HALO, cw), lambda i: (jnp.maximum(i * hb - 1, 0), gv_blk)),
            pl.BlockSpec((CONV_HALO, cw), lambda i: (jnp.maximum(i * hb - 1, 0), gg_blk)),
            _full((taps, cw)), _full((1, cw)), _full((1, cw)), _full((1, cw))]


def _conv_fwd(proj, conv_w, conv_b, cln_g, cln_b, cw, gv_blk, gg_blk):
    t = proj.shape[0]
    taps = conv_w.shape[0]
    tt = _tile(t, 128, CONV_HALO)

    def body(gv_ref, gg_ref, gvh_ref, ggh_ref, cw_ref, cb_ref, lg_ref, lb_ref, o_ref, ext_ref, sh_ref):
        _, _, cn = _conv_stage(gv_ref, gg_ref, gvh_ref, ggh_ref, cw_ref, cb_ref, lg_ref, lb_ref, ext_ref, sh_ref,
                               pl.program_id(0) == 0, tt, taps)
        o_ref[...] = (cn * _sigmoid(cn)).astype(o_ref.dtype)

    return pl.pallas_call(
        body, name="conv_fwd", grid=(t // tt,),
        in_specs=_conv_specs(t, tt, cw, taps, gv_blk, gg_blk),
        out_specs=pl.BlockSpec((tt, cw), lambda i: (i, 0)),
        out_shape=jax.ShapeDtypeStruct((t, cw), MXU_DTYPE),
        scratch_shapes=[pltpu.VMEM((CONV_HALO + tt, cw), F32), pltpu.VMEM((SUBLANE - 1, CONV_HALO + tt, cw), F32)],
        compiler_params=_cparams(("parallel",)),
    )(proj, proj, proj, proj, conv_w, conv_b, cln_g, cln_b)


def _conv_gate(c2, proj, b_pw2, cw, gc_blk):
    t = c2.shape[0]
    tt = _tile(t, 256, SUBLANE)

    def body(c_ref, g_ref, b_ref, o_ref):
        g = g_ref[...]
        o_ref[...] = ((c_ref[...] + b_ref[...]) * (g * _sigmoid(g))).astype(o_ref.dtype)

    return pl.pallas_call(
        body, name="conv_gate", grid=(t // tt,),
        in_specs=[pl.BlockSpec((tt, cw), lambda i: (i, 0)), pl.BlockSpec((tt, cw), lambda i: (i, gc_blk)),
                  _full((1, cw))],
        out_specs=pl.BlockSpec((tt, cw), lambda i: (i, 0)),
        out_shape=jax.ShapeDtypeStruct((t, cw), MXU_DTYPE),
        compiler_params=_cparams(("parallel",)),
    )(c2, proj, b_pw2)


def _conv_gate_bwd(dmix, c2, proj, b_pw2, cw, dm_blk, gc_blk):
    t = c2.shape[0]
    tt = _tile(t, 256, SUBLANE)

    def body(dm_ref, c_ref, g_ref, b_ref, dc2_o, dg_o, db_o):
        g = g_ref[...]
        sg = _sigmoid(g)
        dyc = dm_ref[...]
        dc2 = dyc * (g * sg)
        dc2_o[...] = dc2.astype(dc2_o.dtype)
        dg_o[...] = (dyc * (c_ref[...] + b_ref[...]) * (sg * (1.0 + g * (1.0 - sg)))).astype(dg_o.dtype)

        @pl.when(pl.program_id(0) == 0)
        def _():
            db_o[...] = jnp.zeros_like(db_o)

        db_o[...] += jnp.sum(dc2, axis=0, keepdims=True)

    row = pl.BlockSpec((tt, cw), lambda i: (i, 0))
    return pl.pallas_call(
        body, name="conv_gate_bwd", grid=(t // tt,),
        in_specs=[pl.BlockSpec((tt, cw), lambda i: (i, dm_blk)), row,
                  pl.BlockSpec((tt, cw), lambda i: (i, gc_blk)), _full((1, cw))],
        out_specs=[row, pl.BlockSpec((tt, cw), lambda i: (i, gc_blk)), _full((1, cw))],
        out_shape=[jax.ShapeDtypeStruct((t, cw), MXU_DTYPE), jax.ShapeDtypeStruct((t, proj.shape[1]), MXU_DTYPE),
                   jax.ShapeDtypeStruct((1, cw), F32)],
        compiler_params=_cparams(("arbitrary",)),
    )(dmix, c2, proj, b_pw2)


def _conv_bwd_norm(proj, dcs, conv_w, conv_b, cln_g, cln_b, cw, gv_blk, gg_blk):
    t = proj.shape[0]
    taps = conv_w.shape[0]
    tt = _tile(t, 128, CONV_HALO)

    def body(gv_ref, gg_ref, gvh_ref, ggh_ref, cw_ref, cb_ref, lg_ref, lb_ref, dcs_ref,
             dc_o, dcw_o, dcb_o, dlg_o, dlb_o, ext_ref, sh_ref):
        chat, rstd, cn = _conv_stage(gv_ref, gg_ref, gvh_ref, ggh_ref, cw_ref, cb_ref, lg_ref, lb_ref, ext_ref,
                                     sh_ref, pl.program_id(0) == 0, tt, taps)
        s = _sigmoid(cn)
        dcn = dcs_ref[...] * (s * (1.0 + cn * (1.0 - s)))
        dchat = dcn * lg_ref[...]
        dc = rstd * (dchat - jnp.mean(dchat, axis=-1, keepdims=True)
                     - chat * jnp.mean(dchat * chat, axis=-1, keepdims=True))
        dc_o[...] = dc

        @pl.when(pl.program_id(0) == 0)
        def _():
            for ref in (dcw_o, dcb_o, dlg_o, dlb_o):
                ref[...] = jnp.zeros_like(ref)

        dlg_o[...] += jnp.sum(dcn * chat, axis=0, keepdims=True)
        dlb_o[...] += jnp.sum(dcn, axis=0, keepdims=True)
        dcb_o[...] += jnp.sum(dc, axis=0, keepdims=True)
        off = CONV_HALO - (taps - 1)
        for j in range(taps):
            dcw_o[j:j + 1, :] += jnp.sum(_window(ext_ref, sh_ref, off + j, tt) * dc, axis=0, keepdims=True)

    vec = _full((1, cw))
    vecf = jax.ShapeDtypeStruct((1, cw), F32)
    return pl.pallas_call(
        body, name="conv_bwd_norm", grid=(t // tt,),
        in_specs=_conv_specs(t, tt, cw, taps, gv_blk, gg_blk) + [pl.BlockSpec((tt, cw), lambda i: (i, 0))],
        out_specs=[pl.BlockSpec((tt, cw), lambda i: (i, 0)), _full((taps, cw)), vec, vec, vec],
        out_shape=[jax.ShapeDtypeStruct((t, cw), F32), jax.ShapeDtypeStruct((taps, cw), F32), vecf, vecf, vecf],
        scratch_shapes=[pltpu.VMEM((CONV_HALO + tt, cw), F32), pltpu.VMEM((SUBLANE - 1, CONV_HALO + tt, cw), F32)],
        compiler_params=_cparams(("arbitrary",)),
    )(proj, proj, proj, proj, conv_w, conv_b, cln_g, cln_b, dcs)


def _conv_bwd_glu(dc, proj, conv_w, cw, gv_blk, gg_blk, dproj):
    t = dc.shape[0]
    taps = conv_w.shape[0]
    tt = _tile(t, 128, CONV_HALO)
    hb = tt // CONV_HALO
    nhalo = t // CONV_HALO
    assert gg_blk == gv_blk + 1 and gv_blk % 2 == 0

    def body(dc_ref, dch_ref, gv_ref, gg_ref, cw_ref, dproj_ref, dg_o, ext_ref, sh_ref):
        last = pl.program_id(0) == pl.num_programs(0) - 1
        ext_ref[0:tt, :] = dc_ref[...]
        ext_ref[tt:tt + CONV_HALO, :] = jnp.where(last, 0.0, dch_ref[...])
        _shift_copies(ext_ref, sh_ref)
        du = _window(ext_ref, sh_ref, taps - 1, tt) * cw_ref[0:1, :]
        for j in range(1, taps):
            du = du + _window(ext_ref, sh_ref, taps - 1 - j, tt) * cw_ref[j:j + 1, :]
        sg = _sigmoid(gg_ref[...])
        dg_o[:, 0:cw] = (du * sg).astype(dg_o.dtype)
        dg_o[:, cw:2 * cw] = (du * gv_ref[...] * sg * (1.0 - sg)).astype(dg_o.dtype)

    row = pl.BlockSpec((tt, cw), lambda i: (i, 0))
    return pl.pallas_call(
        body, name="conv_bwd_glu", grid=(t // tt,),
        in_specs=[row, pl.BlockSpec((CONV_HALO, cw), lambda i: (jnp.minimum((i + 1) * hb, nhalo - 1), 0)),
                  pl.BlockSpec((tt, cw), lambda i: (i, gv_blk)), pl.BlockSpec((tt, cw), lambda i: (i, gg_blk)),
                  _full((taps, cw)), ANY_SPEC],
        out_specs=pl.BlockSpec((tt, 2 * cw), lambda i: (i, gv_blk // 2)),
        out_shape=jax.ShapeDtypeStruct(dproj.shape, dproj.dtype),
        input_output_aliases={5: 0},
        scratch_shapes=[pltpu.VMEM((tt + CONV_HALO, cw), F32), pltpu.VMEM((SUBLANE - 1, tt + CONV_HALO, cw), F32)],
        compiler_params=_cparams(("parallel",)),
    )(dc, dc, proj, proj, conv_w, dproj)


HBM_SPEC = pl.BlockSpec(memory_space=pltpu.HBM)
ANY_SPEC = pl.BlockSpec(memory_space=pl.ANY)


def _gather_plan(ins, outs, send_sems, recv_sems, local_sems):
    na = len(ins)
    x, y, c = lax.axis_index("x"), lax.axis_index("y"), lax.axis_index("c")
    me, sibling = (x, y, c), (x, y, 1 - c)
    chips = [(1 - x, y), (x, 1 - y), (1 - x, 1 - y)]

    def slot(px, py, pc):
        return 4 * px + 2 * py + pc

    def copy(a, k, block, to, src=None):
        dst = outs[a].at[slot(*block)]
        return pltpu.make_async_remote_copy(
            src_ref=dst if src is None else src, dst_ref=dst,
            send_sem=send_sems.at[a, k], recv_sem=recv_sems.at[a, k],
            device_id=to, device_id_type=MESH)

    mine = [pltpu.make_async_copy(ins[a], outs[a].at[slot(*me)], local_sems.at[a]) for a in range(na)]
    first = []
    for a in range(na):
        first.append(copy(a, 0, me, sibling, src=ins[a]))
        first += [copy(a, 1 + j, me, (*chip, c), src=ins[a]) for j, chip in enumerate(chips)]

    def finish():
        passed = []
        for j, chip in enumerate(chips):
            for a in range(na):
                copy(a, 1 + j, (*chip, c), me).wait_recv()
                fwd = copy(a, 4 + j, (*chip, c), sibling)
                fwd.start()
                passed.append(fwd)
        for a in range(na):
            copy(a, 0, sibling, me).wait_recv()
            for j, chip in enumerate(chips):
                copy(a, 4 + j, (*chip, 1 - c), me).wait_recv()
        for cp in first + passed:
            cp.wait_send()
        for cp in mine:
            cp.wait()

    return mine + first, finish


def _gather_shapes(shards):
    return [jax.ShapeDtypeStruct((N_DEV,) + s.shape, s.dtype) for s in shards]


def _gather_sems(na):
    if not na:
        return []
    return [pltpu.SemaphoreType.DMA((na, N_DEV - 1)), pltpu.SemaphoreType.DMA((na, N_DEV - 1)),
            pltpu.SemaphoreType.DMA((na,))]


def _all_gather(shards, name):
    na = len(shards)

    def body(*refs):
        starts, finish = _gather_plan(refs[:na], refs[na:2 * na], *refs[2 * na:])
        for cp in starts:
            cp.start()
        finish()

    return pl.pallas_call(
        body, name=name,
        in_specs=[HBM_SPEC] * na, out_specs=[HBM_SPEC] * na,
        out_shape=_gather_shapes(shards), scratch_shapes=_gather_sems(na),
        compiler_params=pltpu.CompilerParams(has_side_effects=True),
    )(*shards)


N_CHIPS = N_DEV // 2
EXCHANGE_SLOTS = {"direct": N_DEV - 1, "sibling": N_CHIPS, "chips": N_CHIPS - 1}


def _exchange_copies(ins, outs, send_sems, recv_sems, pattern="direct"):
    x, y, c = lax.axis_index("x"), lax.axis_index("y"), lax.axis_index("c")
    copies = []

    def add(a, src_slot, dst_slot, sem, peer):
        copies.append(pltpu.make_async_remote_copy(
            src_ref=ins[a].at[src_slot], dst_ref=outs[a].at[dst_slot],
            send_sem=send_sems.at[a, sem], recv_sem=recv_sems.at[a, sem],
            device_id=peer, device_id_type=MESH))

    for a in range(len(ins)):
        if pattern == "direct":
            for k in range(1, N_DEV):
                px = 1 - x if k & 4 else x
                py = 1 - y if k & 2 else y
                pc = 1 - c if k & 1 else c
                add(a, 4 * px + 2 * py + pc, k - 1, k - 1, (px, py, pc))
        elif pattern == "sibling":
            for j in range(N_CHIPS):
                add(a, j, j, j, (x, y, 1 - c))
        else:
            for k in range(1, N_CHIPS):
                px = 1 - x if k & 2 else x
                py = 1 - y if k & 1 else y
                add(a, 2 * px + py, k - 1, k - 1, (px, py, c))
    return copies


def _exchange_shapes(parts, pattern="direct"):
    return [jax.ShapeDtypeStruct((EXCHANGE_SLOTS[pattern],) + p.shape[1:], p.dtype) for p in parts]


def _exchange_sems(na, pattern="direct"):
    if not na:
        return []
    return [pltpu.SemaphoreType.DMA((na, EXCHANGE_SLOTS[pattern]))] * 2


def _exchange(parts, name, pattern):
    na = len(parts)

    def body(*refs):
        copies = _exchange_copies(refs[:na], refs[na:2 * na], *refs[2 * na:], pattern=pattern)
        for cp in copies:
            cp.start()
        for cp in copies:
            cp.wait()

    return pl.pallas_call(
        body, name=name,
        in_specs=[HBM_SPEC] * na, out_specs=[HBM_SPEC] * na,
        out_shape=_exchange_shapes(parts, pattern),
        scratch_shapes=_exchange_sems(na, pattern),
        compiler_params=pltpu.CompilerParams(has_side_effects=True),
    )(*parts)


def _chip_sum(mine, theirs):
    nslot, r, c = mine.shape
    rows = nslot * r
    tr = _tile(rows, 256, SUBLANE)

    def body(a_ref, b_ref, f_ref, w_ref):
        s = a_ref[...] + b_ref[...].astype(F32)
        f_ref[...] = s
        w_ref[...] = s.astype(w_ref.dtype)

    blk = pl.BlockSpec((tr, c), lambda i: (i, 0))
    f, w = pl.pallas_call(
        body, name="chip_sum", grid=(rows // tr,),
        in_specs=[blk, blk], out_specs=[blk, blk],
        out_shape=[jax.ShapeDtypeStruct((rows, c), F32), jax.ShapeDtypeStruct((rows, c), theirs.dtype)],
        compiler_params=_cparams(("parallel",)),
    )(mine.reshape(rows, c), theirs.reshape(rows, c))
    return f.reshape(mine.shape), w.reshape(mine.shape)


def _cast(v, dtype, name):
    r, c = v.shape
    tr = _tile(r, 256, SUBLANE)

    def body(i_ref, o_ref):
        o_ref[...] = i_ref[...].astype(o_ref.dtype)

    return pl.pallas_call(
        body, name=name, grid=(r // tr,),
        in_specs=[pl.BlockSpec((tr, c), lambda i: (i, 0))],
        out_specs=pl.BlockSpec((tr, c), lambda i: (i, 0)),
        out_shape=jax.ShapeDtypeStruct((r, c), dtype),
        compiler_params=_cparams(("parallel",)),
    )(v)


def _adamw(w, m, v, recv, own, name):
    r, c = w.shape
    ns = recv.shape[0]
    tr = _tile(r, 128, SUBLANE)
    c1 = 1.0 - ADAM_B1 ** ADAM_STEP
    c2 = 1.0 - ADAM_B2 ** ADAM_STEP

    def body(*refs):
        if own is None:
            w_ref, m_ref, v_ref, rc_ref = refs[:4]
            g = rc_ref[0].astype(F32)
            start = 1
        else:
            w_ref, m_ref, v_ref, rc_ref, own_ref = refs[:5]
            g = own_ref[...]
            start = 0
        g_o, d_o, m_o, v_o = refs[-4:]
        for s in range(start, ns):
            g = g + rc_ref[s].astype(F32)
        mn = ADAM_B1 * m_ref[...] + (1.0 - ADAM_B1) * g
        vn = ADAM_B2 * v_ref[...] + (1.0 - ADAM_B2) * (g * g)
        m_hat = mn / c1
        v_hat = vn / c2
        g_o[...] = g
        d_o[...] = -ADAM_LR * (m_hat / (jnp.sqrt(v_hat) + ADAM_EPS) + ADAM_WD * w_ref[...])
        m_o[...] = mn
        v_o[...] = vn

    row = pl.BlockSpec((tr, c), lambda i: (i, 0))
    ins = [w, m, v, recv] + ([] if own is None else [own])
    in_specs = [row, row, row, pl.BlockSpec((ns, tr, c), lambda i: (0, i, 0))] + ([] if own is None else [row])
    return pl.pallas_call(
        body, name=name, grid=(r // tr,),
        in_specs=in_specs, out_specs=[row] * 4,
        out_shape=[jax.ShapeDtypeStruct((r, c), F32)] * 4,
        compiler_params=_cparams(("parallel",)),
    )(*ins)


def _to_t(v, nh, n, axis=-1):
    v = jnp.moveaxis(v, axis, -1)
    v = v.reshape(v.shape[:-1] + (nh, n)).swapaxes(-1, -2).reshape(v.shape)
    return jnp.moveaxis(v, -1, axis)


def _from_t(v, nh, n, axis=-1):
    v = jnp.moveaxis(v, axis, -1)
    v = v.reshape(v.shape[:-1] + (n, nh)).swapaxes(-1, -2).reshape(v.shape)
    return jnp.moveaxis(v, -1, axis)


def _pad_to(v, size, axis):
    pad = [(0, 0)] * v.ndim
    pad[axis] = (0, size - v.shape[axis])
    return jnp.pad(v, pad)


def kernel(x, norm_pre_g, w_in, mu_shift, w0, w_lora_up, a0, a_lora_up, k_k, k_a, r_k, lnx_g, lnx_b, conv_w, conv_b, cln_g, cln_b, w_pw2, b_pw2, w_out, norm_post_g, loss_target, m_norm_pre_g, m_w_in, m_mu_shift, m_w0, m_w_lora_up, m_a0, m_a_lora_up, m_k_k, m_k_a, m_r_k, m_lnx_g, m_lnx_b, m_conv_w, m_conv_b, m_cln_g, m_cln_b, m_w_pw2, m_b_pw2, m_w_out, m_norm_post_g, v_norm_pre_g, v_w_in, v_mu_shift, v_w0, v_w_lora_up, v_a0, v_a_lora_up, v_k_k, v_k_a, v_r_k, v_lnx_g, v_lnx_b, v_conv_w, v_conv_b, v_cln_g, v_cln_b, v_w_pw2, v_b_pw2, v_w_out, v_norm_post_g):
    args = dict(locals())
    t, d = x.shape[1], x.shape[2]
    hw = w0.shape[0]
    cw = conv_b.shape[0]
    nh, n = r_k.shape
    lw, la = w_lora_up.shape[0], a_lora_up.shape[0]
    taps = conv_w.shape[0]
    in_cols = w_in.shape[1] * N_DEV
    shift_cols = 3 * hw + lw + la
    assert in_cols == shift_cols + hw + 3 * cw and hw == cw and hw % LANE == 0 and LANE % nh == 0
    assert lw <= LORA_PAD and la <= LORA_PAD and taps - 1 <= CONV_HALO and hw % (2 * LORA_PAD) == 0
    q = hw // LANE
    gate_blk, gv_blk, gg_blk, gc_blk = 3, 4, 5, 6
    lo0 = 7 * hw
    lo_blk = lo0 // (2 * LORA_PAD)
    x2, tgt2 = x[0], loss_target[0]
    row = lambda v: v.reshape(1, -1)

    (w_in_g,) = _all_gather([_cast(w_in, MXU_DTYPE, "cast_w_in")], "gather_w_in")
    w_full = w_in_g.transpose(1, 0, 2).reshape(d, in_cols)
    c0 = shift_cols
    wp = jnp.concatenate([
        _to_t(w_full[:, 0:hw], nh, n), _to_t(w_full[:, hw:2 * hw], nh, n), _to_t(w_full[:, 2 * hw:3 * hw], nh, n),
        _to_t(w_full[:, c0:c0 + hw], nh, n), w_full[:, c0 + hw:],
        _pad_to(w_full[:, 3 * hw:3 * hw + lw], LORA_PAD, 1), _pad_to(w_full[:, 3 * hw + lw:c0], LORA_PAD, 1)], axis=1)
    mu_p = row(jnp.concatenate([
        _to_t(mu_shift[0:hw], nh, n), _to_t(mu_shift[hw:2 * hw], nh, n), _to_t(mu_shift[2 * hw:3 * hw], nh, n)]))
    mu_lo = row(jnp.concatenate([
        _pad_to(mu_shift[3 * hw:3 * hw + lw], LORA_PAD, 0), _pad_to(mu_shift[3 * hw + lw:], LORA_PAD, 0)]))
    tvec = lambda v: row(_to_t(v, nh, n))
    w0_t, a0_t, kk_t, ka_t, lg_t, lb_t = tvec(w0), tvec(a0), tvec(k_k), tvec(k_a), tvec(lnx_g), tvec(lnx_b)
    rk_t = row(r_k.T)

    c3 = lambda v: v.reshape(t, q, LANE)
    c2d = lambda v: v.reshape(t * q, LANE)
    me = 4 * lax.axis_index("x") + 2 * lax.axis_index("y") + lax.axis_index("c")
    own = lambda parts: lax.dynamic_index_in_dim(parts, me, 0, keepdims=False)
    wire = lambda parts, nm: _cast(parts.reshape(-1, parts.shape[-1]), WIRE_DTYPE, nm).reshape(parts.shape)

    h = _prenorm(x2, row(norm_pre_g))
    proj, w_out_g, w_pw2_g, wup_w_g, wup_a_g, conv_w_g = _matmul(
        h, wp, "nn", F32, "mm_proj",
        gather=[_cast(w_out, MXU_DTYPE, "cast_w_out"), _cast(w_pw2, MXU_DTYPE, "cast_w_pw2"),
                w_lora_up, a_lora_up, conv_w])
    w_out_f = w_out_g.reshape(N_DEV * w_out.shape[0], d)
    w_out_p = jnp.concatenate([_to_t(w_out_f[:hw], nh, n, axis=0), w_out_f[hw:]], axis=0)
    w_pw2_f = w_pw2_g.reshape(N_DEV * w_pw2.shape[0], cw)
    wup_w = _pad_to(_to_t(wup_w_g.transpose(1, 0, 2).reshape(lw, hw), nh, n), LORA_PAD, 0)
    wup_a = _pad_to(_to_t(wup_a_g.transpose(1, 0, 2).reshape(la, hw), nh, n), LORA_PAD, 0)
    conv_w_f = conv_w_g.transpose(1, 0, 2).reshape(taps, cw)
    r_a, w_a, kh_a, v_a, kn_a, b_a = _rwkv_pre(
        proj, mu_p, mu_lo, w0_t, a0_t, kk_t, ka_t, wup_w, wup_a, hw, nh, lo_blk)
    y_c, sa_c, ck = _wkv_fwd(c3(w_a), c3(b_a), c3(kh_a), c3(r_a), c3(kn_a), c3(v_a), nh)
    y_a = c2d(y_c)
    y_rwkv = _rwkv_post(y_a, r_a, kh_a, v_a, proj, lg_t, lb_t, rk_t, hw, n, gate_blk)
    cs = _conv_fwd(proj, conv_w_f, row(conv_b), row(cln_g), row(cln_b), cw, gv_blk, gg_blk)
    c2 = _matmul(cs, w_pw2_f, "nn", F32, "mm_pw2")
    y_conv = _conv_gate(c2, proj, row(b_pw2), cw, gc_blk)
    mix = jnp.concatenate([y_rwkv, y_conv], axis=1)
    out = _matmul(mix, w_out_p, "nn", F32, "mm_out")
    dout, dy, loss_part, d_post_g = _post_loss(out, x2, tgt2, row(norm_post_g))

    dmix = _matmul(dout, w_out_p, "nt", F32, "mm_dmix")
    d_w_out_p = _matmul(mix, dout, "tn", F32, "mm_dw_out")
    dc2, dproj, d_b_pw2 = _conv_gate_bwd(dmix, c2, proj, row(b_pw2), cw, 1, gc_blk)
    dcs = _matmul(dc2, w_pw2_f, "nt", F32, "mm_dcs")
    d_w_pw2 = _matmul(cs, dc2, "tn", F32, "mm_dw_pw2")
    dc, d_conv_w, d_conv_b, d_cln_g, d_cln_b = _conv_bwd_norm(
        proj, dcs, conv_w_f, row(conv_b), row(cln_g), row(cln_b), cw, gv_blk, gg_blk)
    dproj = _conv_bwd_glu(dc, proj, conv_w_f, cw, gv_blk, gg_blk, dproj)
    dproj, dy_rec, dr_bon, dkh_bon, dv_bon, d_lg_t, d_lb_t, d_rk_t = _rwkv_post_bwd(
        dmix, y_a, r_a, kh_a, v_a, proj, lg_t, lb_t, rk_t, hw, n, gate_blk, dproj)
    ck_i = ck.reshape(ck.shape[0], n, n, nh).transpose(0, 2, 1, 3).reshape(ck.shape)
    d_w_out_f = jnp.concatenate([_from_t(d_w_out_p[:hw], nh, n, axis=0), d_w_out_p[hw:]], axis=0)
    d_w_out_parts = d_w_out_f.reshape((N_DEV,) + w_out.shape)
    d_w_pw2_parts = d_w_pw2.reshape((N_DEV,) + w_pw2.shape)
    dv_c, dr_c, dw_c, db_c, dk_c, dkn_c, recv_w_out, recv_w_pw2 = _wkv_bwd(
        c3(r_a), c3(w_a), c3(b_a), c3(kh_a), c3(kn_a), c3(dy_rec), sa_c, c3(v_a), ck_i, nh,
        exchange=[wire(d_w_out_parts, "wire_w_out"), wire(d_w_pw2_parts, "wire_w_pw2")])
    dxs, dxs_lo, d_mu_p, d_mu_lo, d_w0_t, d_a0_t, d_kk_t, d_ka_t, d_wup_w, d_wup_a = _rwkv_pre_bwd(
        proj, mu_p, mu_lo, w0_t, a0_t, kk_t, ka_t, wup_w, wup_a, hw, nh, lo_blk,
        c2d(dr_c), c2d(dw_c), c2d(dk_c), c2d(dv_c), c2d(dkn_c), c2d(db_c), dr_bon, dkh_bon, dv_bon)
    dproj = _shift_bwd(dxs, mu_p, "shift_bwd", dproj, 0)
    dproj = _shift_bwd(dxs_lo, mu_lo, "shift_bwd_lora", dproj, lo_blk)
    colparts = lambda v: v.reshape(v.shape[0], N_DEV, v.shape[1] // N_DEV).transpose(1, 0, 2)
    d_wup_w_parts = colparts(_from_t(d_wup_w[:lw], nh, n))
    d_wup_a_parts = colparts(_from_t(d_wup_a[:la], nh, n))
    d_conv_w_parts = colparts(d_conv_w)
    d_wp, recv_wup_w, recv_wup_a, recv_conv_w = _matmul(
        h, dproj, "tn", F32, "mm_dw_in", exchange=[d_wup_w_parts, d_wup_a_parts, d_conv_w_parts])

    d_w_full = jnp.concatenate([
        _from_t(d_wp[:, 0:hw], nh, n), _from_t(d_wp[:, hw:2 * hw], nh, n), _from_t(d_wp[:, 2 * hw:3 * hw], nh, n),
        d_wp[:, lo0:lo0 + lw], d_wp[:, lo0 + LORA_PAD:lo0 + LORA_PAD + la],
        _from_t(d_wp[:, 3 * hw:4 * hw], nh, n), d_wp[:, 4 * hw:lo0]], axis=1)
    sc = w_in.shape[1]
    by_core = d_w_full.reshape(d, N_CHIPS, 2, sc).transpose(2, 1, 0, 3)
    core = lax.axis_index("c")
    for_mine = lax.dynamic_index_in_dim(by_core, core, 0, keepdims=False)
    for_sibling = lax.dynamic_index_in_dim(by_core, 1 - core, 0, keepdims=False)
    (from_sibling,) = _exchange([wire(for_sibling, "wire_w_in")], "exchange_sibling", "sibling")
    chip_f32, chip_wire = _chip_sum(for_mine, from_sibling)
    dh, recv_w_in = _matmul(dproj, wp, "nt", F32, "mm_dh", exchange=[chip_wire], pattern="chips")
    own_w_in = lax.dynamic_index_in_dim(chip_f32, 2 * lax.axis_index("x") + lax.axis_index("y"), 0, keepdims=False)
    grad_x, d_pre_g = _prenorm_bwd(dh, x2, dy, row(norm_pre_g))
    d_mu = jnp.concatenate([
        _from_t(d_mu_p[0, 0:hw], nh, n), _from_t(d_mu_p[0, hw:2 * hw], nh, n), _from_t(d_mu_p[0, 2 * hw:3 * hw], nh, n),
        d_mu_lo[0, 0:lw], d_mu_lo[0, LORA_PAD:LORA_PAD + la]])
    ft = lambda v: _from_t(v[0], nh, n)
    small = {
        "norm_pre_g": d_pre_g[0], "mu_shift": d_mu, "w0": ft(d_w0_t), "a0": ft(d_a0_t), "k_k": ft(d_kk_t),
        "k_a": ft(d_ka_t), "r_k": d_rk_t[0].reshape(n, nh).T.reshape(-1), "lnx_g": ft(d_lg_t), "lnx_b": ft(d_lb_t),
        "conv_b": d_conv_b[0], "cln_g": d_cln_g[0], "cln_b": d_cln_b[0], "b_pw2": d_b_pw2[0],
        "norm_post_g": d_post_g[0]}
    small_names = list(small)
    packed = jnp.concatenate([small[k] for k in small_names] + [loss_part[0, 0:1]])
    plen = packed.shape[0]
    ppad = -(-plen // LANE) * LANE
    packed = _pad_to(packed, ppad, 0).reshape(1, ppad)

    (packed_all,) = _all_gather([packed], "gather_small")

    res = {}
    sharded = [("w_in", own_w_in, recv_w_in), ("w_out", own(d_w_out_parts), recv_w_out),
               ("w_pw2", own(d_w_pw2_parts), recv_w_pw2), ("w_lora_up", own(d_wup_w_parts), recv_wup_w),
               ("a_lora_up", own(d_wup_a_parts), recv_wup_a), ("conv_w", own(d_conv_w_parts), recv_conv_w)]
    for nm, mine, rc in sharded:
        res[nm] = _adamw(args[nm], args["m_" + nm], args["v_" + nm], rc, mine, "adamw_" + nm)
    w_small = _pad_to(jnp.concatenate([args[k].reshape(-1) for k in small_names]), ppad, 0).reshape(1, ppad)
    m_small = _pad_to(jnp.concatenate([args["m_" + k].reshape(-1) for k in small_names]), ppad, 0).reshape(1, ppad)
    v_small = _pad_to(jnp.concatenate([args["v_" + k].reshape(-1) for k in small_names]), ppad, 0).reshape(1, ppad)
    g_s, d_s, m_s, v_s = _adamw(w_small, m_small, v_small, packed_all, None, "adamw_small")
    off = 0
    for k in small_names:
        size = args[k].size
        res[k] = tuple(o[0, off:off + size].reshape(args[k].shape) for o in (g_s, d_s, m_s, v_s))
        off += size
    loss = g_s[0, plen - 1]

    order = ["norm_pre_g", "w_in", "mu_shift", "w0", "w_lora_up", "a0", "a_lora_up", "k_k", "k_a", "r_k",
             "lnx_g", "lnx_b", "conv_w", "conv_b", "cln_g", "cln_b", "w_pw2", "b_pw2", "w_out", "norm_post_g"]
    outs = [loss, grad_x[None]]
    for slot in range(4):
        outs += [res[k][slot] for k in order]
    return tuple(outs)
```

```python
import functools

import jax
import jax.numpy as jnp
from jax import lax
from jax.experimental import pallas as pl
from jax.experimental.pallas import tpu as pltpu

F32 = jnp.float32
MXU_DTYPE = jnp.bfloat16
WIRE_DTYPE = jnp.bfloat16
HI = lax.Precision.HIGHEST

NORM_EPS = 1e-6
LN_EPS = 1e-5
GN_EPS_PER_CHANNEL = 1e-5
KK_EPS = 1e-12
ADAM_LR = 0.001
ADAM_B1 = 0.9
ADAM_B2 = 0.999
ADAM_EPS = 1e-08
ADAM_WD = 0.01
ADAM_STEP = 10

LANE = 128
SUBLANE = 8
LORA_PAD = 128
CONV_HALO = 32
N_DEV = 8
VMEM_LIMIT = 56 * 1024 * 1024
WKV_CHUNK = 16
RWKV_ROWS = 64
MESH = pl.DeviceIdType.MESH


def _tile(n, target, mult):
    if n <= target:
        return n
    best = None
    for d in range(mult, target + 1, mult):
        if n % d == 0:
            best = d
    assert best is not None, (n, target, mult)
    return best


def _cparams(sem=None):
    return pltpu.CompilerParams(dimension_semantics=sem, vmem_limit_bytes=VMEM_LIMIT)


def _sigmoid(x):
    return 1.0 / (1.0 + jnp.exp(-x))


def _full(shape):
    nd = len(shape)
    return pl.BlockSpec(shape, lambda *_: (0,) * nd)


def _compact_spec(tt, q):
    return pl.BlockSpec((tt * q, LANE), lambda i: (i, 0))


def _load_compact(ref, tt, q):
    return jnp.concatenate([ref[pl.ds(p, tt, stride=q), :] for p in range(q)], axis=1)


def _store_compact(ref, val, tt, q):
    for p in range(q):
        ref[pl.ds(p, tt, stride=q), :] = val[:, p * LANE:(p + 1) * LANE]


def _matmul(a, b, mode, out_dtype, name, exchange=(), pattern="direct", gather=()):
    if mode == "nn":
        (m, k), (k2, n) = a.shape, b.shape
    elif mode == "nt":
        (m, k), (n, k2) = a.shape, b.shape
    else:
        (k, m), (k2, n) = a.shape, b.shape
    assert k == k2, (a.shape, b.shape, mode)
    tm, tn, tk = _tile(m, 1024, LANE), _tile(n, 768, LANE), _tile(k, 2048, LANE)
    nk = k // tk
    ne = len(exchange)
    grid = (m // tm, n // tn, nk)
    if mode == "nn":
        a_spec = pl.BlockSpec((tm, tk), lambda i, j, kk: (i, kk))
        b_spec = pl.BlockSpec((tk, tn), lambda i, j, kk: (kk, j))
        dims = (((1,), (0,)), ((), ()))
    elif mode == "nt":
        a_spec = pl.BlockSpec((tm, tk), lambda i, j, kk: (i, kk))
        b_spec = pl.BlockSpec((tn, tk), lambda i, j, kk: (j, kk))
        dims = (((1,), (1,)), ((), ()))
    else:
        a_spec = pl.BlockSpec((tk, tm), lambda i, j, kk: (kk, i))
        b_spec = pl.BlockSpec((tk, tn), lambda i, j, kk: (kk, j))
        dims = (((0,), (0,)), ((), ()))

    ng = len(gather)
    nx = ne + ng

    def body(*refs):
        a_ref, b_ref = refs[:2]
        parts, shards = refs[2:2 + ne], refs[2 + ne:2 + nx]
        o_ref = refs[2 + nx]
        recvs, gathered = refs[3 + nx:3 + nx + ne], refs[3 + nx + ne:3 + 2 * nx]
        acc_ref = refs[3 + 2 * nx]
        sems = refs[4 + 2 * nx:]
        i, j, kk = pl.program_id(0), pl.program_id(1), pl.program_id(2)
        first_step = (i == 0) & (j == 0) & (kk == 0)
        last_step = (i == grid[0] - 1) & (j == grid[1] - 1) & (kk == nk - 1)
        if ne:
            copies = _exchange_copies(parts, recvs, *sems[:2], pattern=pattern)
        if ng:
            gather_starts, gather_finish = _gather_plan(shards, gathered, *sems[2 if ne else 0:])

        @pl.when(first_step)
        def _():
            for cp in (copies if ne else []) + (gather_starts if ng else []):
                cp.start()

        @pl.when(kk == 0)
        def _():
            acc_ref[...] = jnp.zeros_like(acc_ref)

        acc_ref[...] += lax.dot_general(a_ref[...], b_ref[...], dims, preferred_element_type=F32)

        @pl.when(kk == nk - 1)
        def _():
            o_ref[...] = acc_ref[...].astype(o_ref.dtype)

        @pl.when(last_step)
        def _():
            for cp in (copies if ne else []):
                cp.wait()
            if ng:
                gather_finish()

    res = pl.pallas_call(
        body, name=name,
        grid=grid,
        in_specs=[a_spec, b_spec] + [HBM_SPEC] * nx,
        out_specs=[pl.BlockSpec((tm, tn), lambda i, j, kk: (i, j))] + [HBM_SPEC] * nx,
        out_shape=[jax.ShapeDtypeStruct((m, n), out_dtype)] + _exchange_shapes(exchange, pattern)
        + _gather_shapes(gather),
        scratch_shapes=[pltpu.VMEM((tm, tn), F32)] + _exchange_sems(ne, pattern) + _gather_sems(ng),
        compiler_params=_cparams(("arbitrary",) * 3 if nx else ("parallel", "parallel", "arbitrary")),
    )(a, b, *exchange, *gather)
    return res if nx else res[0]


def _prenorm(x, g):
    t, d = x.shape
    tt = _tile(t, 256, SUBLANE)

    def body(x_ref, g_ref, h_ref):
        xv = x_ref[...]
        rinv = lax.rsqrt(jnp.mean(xv * xv, axis=-1, keepdims=True) + NORM_EPS)
        h_ref[...] = (xv * rinv * g_ref[...]).astype(h_ref.dtype)

    return pl.pallas_call(
        body, name="prenorm", grid=(t // tt,),
        in_specs=[pl.BlockSpec((tt, d), lambda i: (i, 0)), _full((1, d))],
        out_specs=pl.BlockSpec((tt, d), lambda i: (i, 0)),
        out_shape=jax.ShapeDtypeStruct((t, d), MXU_DTYPE),
        compiler_params=_cparams(("parallel",)),
    )(x, g)


def _post_loss(out, x, target, g):
    t, d = out.shape
    tt = _tile(t, 128, SUBLANE)

    def body(o_ref, x_ref, t_ref, g_ref, dout_ref, dy_ref, loss_ref, dg_ref):
        i = pl.program_id(0)
        ov = o_ref[...]
        rinv = lax.rsqrt(jnp.mean(ov * ov, axis=-1, keepdims=True) + NORM_EPS)
        nv = ov * rinv
        gv = g_ref[...]
        err = x_ref[...] + nv * gv - t_ref[...]
        part = 0.5 * jnp.sum(jnp.mean(err * err, axis=-1, keepdims=True), axis=0, keepdims=True)
        dy = err * (1.0 / d)
        dy_ref[...] = dy
        dn = dy * gv
        dout = rinv * (dn - nv * jnp.mean(dn * nv, axis=-1, keepdims=True))
        dout_ref[...] = dout.astype(dout_ref.dtype)
        dg = jnp.sum(dy * nv, axis=0, keepdims=True)

        @pl.when(i == 0)
        def _():
            loss_ref[...] = jnp.zeros_like(loss_ref)
            dg_ref[...] = jnp.zeros_like(dg_ref)

        loss_ref[...] += jnp.broadcast_to(part, loss_ref.shape)
        dg_ref[...] += dg

    row = pl.BlockSpec((tt, d), lambda i: (i, 0))
    return pl.pallas_call(
        body, name="post_loss", grid=(t // tt,),
        in_specs=[row, row, row, _full((1, d))],
        out_specs=[row, row, _full((1, LANE)), _full((1, d))],
        out_shape=[jax.ShapeDtypeStruct((t, d), MXU_DTYPE), jax.ShapeDtypeStruct((t, d), F32),
                   jax.ShapeDtypeStruct((1, LANE), F32), jax.ShapeDtypeStruct((1, d), F32)],
        compiler_params=_cparams(("arbitrary",)),
    )(out, x, target, g)


def _prenorm_bwd(dh, x, dy, g):
    t, d = x.shape
    tt = _tile(t, 128, SUBLANE)

    def body(dh_ref, x_ref, dy_ref, g_ref, gx_ref, dg_ref):
        i = pl.program_id(0)
        xv = x_ref[...]
        rinv = lax.rsqrt(jnp.mean(xv * xv, axis=-1, keepdims=True) + NORM_EPS)
        nx = xv * rinv
        dhv = dh_ref[...]
        dnx = dhv * g_ref[...]
        dx = rinv * (dnx - nx * jnp.mean(dnx * nx, axis=-1, keepdims=True))
        gx_ref[...] = dy_ref[...] + dx

        @pl.when(i == 0)
        def _():
            dg_ref[...] = jnp.zeros_like(dg_ref)

        dg_ref[...] += jnp.sum(dhv * nx, axis=0, keepdims=True)

    row = pl.BlockSpec((tt, d), lambda i: (i, 0))
    return pl.pallas_call(
        body, name="prenorm_bwd", grid=(t // tt,),
        in_specs=[row, row, row, _full((1, d))],
        out_specs=[row, _full((1, d))],
        out_shape=[jax.ShapeDtypeStruct((t, d), F32), jax.ShapeDtypeStruct((1, d), F32)],
        compiler_params=_cparams(("arbitrary",)),
    )(dh, x, dy, g)


def _segsum(v, nh):
    q = v.shape[1] // LANE
    s = v[:, 0:LANE]
    for p in range(1, q):
        s = s + v[:, p * LANE:(p + 1) * LANE]
    shift = nh
    while shift < LANE:
        s = s + pltpu.roll(s, shift, 1)
        shift *= 2
    return jnp.concatenate([s] * q, axis=1)


def _shifted(cur_ref, prev_ref, lo, hi, first):
    cur = cur_ref[:, lo:hi]
    last = jnp.where(first, 0.0, prev_ref[SUBLANE - 1:SUBLANE, lo:hi])
    prev = pltpu.roll(cur, 1, 0)
    rows = lax.broadcasted_iota(jnp.int32, cur.shape, 0)
    return cur, jnp.where(rows == 0, last, prev)


def _rwkv_mix(main_ref, mainp_ref, lo_ref, lop_ref, mu_ref, mulo_ref, w0_ref, a0_ref, kk_ref, ka_ref,
              wupw_ref, wupa_ref, hw, nh, first):
    def xs(cur_ref, prev_ref, m_ref, lo, hi):
        cur, prev = _shifted(cur_ref, prev_ref, lo, hi, first)
        return cur + (prev - cur) * m_ref[:, lo:hi], prev - cur

    out = {}
    out["r"], out["r_d"] = xs(main_ref, mainp_ref, mu_ref, 0, hw)
    out["k"], out["k_d"] = xs(main_ref, mainp_ref, mu_ref, hw, 2 * hw)
    out["v"], out["v_d"] = xs(main_ref, mainp_ref, mu_ref, 2 * hw, 3 * hw)
    out["wl"], out["wl_d"] = xs(lo_ref, lop_ref, mulo_ref, 0, LORA_PAD)
    out["al"], out["al_d"] = xs(lo_ref, lop_ref, mulo_ref, LORA_PAD, 2 * LORA_PAD)
    th = jnp.tanh(out["wl"])
    zw = w0_ref[...] + jnp.dot(th, wupw_ref[...], preferred_element_type=F32, precision=HI)
    u = -zw
    softplus = jnp.maximum(u, 0.0) + jnp.log(1.0 + jnp.exp(-jnp.abs(u)))
    wlog = -softplus - 0.5
    ew = jnp.exp(wlog)
    za = a0_ref[...] + jnp.dot(out["al"], wupa_ref[...], preferred_element_type=F32, precision=HI)
    a = _sigmoid(za)
    kkr = out["k"] * kk_ref[...]
    nr = jnp.sqrt(_segsum(kkr * kkr, nh))
    nrm = jnp.maximum(nr, KK_EPS)
    out.update(th=th, zw=zw, ew=ew, decay=jnp.exp(-ew), a=a, kkr=kkr, nr=nr, nrm=nrm, kk=kkr / nrm)
    out["kh"] = out["k"] * (1.0 + (a - 1.0) * ka_ref[...])
    return out


def _mix_specs(tt, hw, lo_blk):
    mw, lw2 = 3 * hw, 2 * LORA_PAD
    before = lambda i: jnp.maximum(i * (tt // SUBLANE) - 1, 0)
    vec = _full((1, hw))
    return [pl.BlockSpec((tt, mw), lambda i: (i, 0)), pl.BlockSpec((SUBLANE, mw), lambda i: (before(i), 0)),
            pl.BlockSpec((tt, lw2), lambda i: (i, lo_blk)), pl.BlockSpec((SUBLANE, lw2), lambda i: (before(i), lo_blk)),
            _full((1, mw)), _full((1, lw2)), vec, vec, vec, vec, _full((LORA_PAD, hw)), _full((LORA_PAD, hw))]


def _rwkv_pre(proj, mu, mu_lo, w0, a0, k_k, k_a, wup_w, wup_a, hw, nh, lo_blk):
    t = proj.shape[0]
    tt = _tile(t, RWKV_ROWS, SUBLANE)

    def body(*refs):
        r_o, w_o, kh_o, v_o, kn_o, b_o = refs[-6:]
        f = _rwkv_mix(*refs[:-6], hw, nh, pl.program_id(0) == 0)
        for ref, val in ((r_o, f["r"]), (w_o, f["decay"]), (kh_o, f["kh"]), (v_o, f["v"]), (kn_o, -f["kk"]),
                         (b_o, f["kk"] * f["a"])):
            _store_compact(ref, val, tt, q)

    q = hw // LANE
    return pl.pallas_call(
        body, name="rwkv_pre", grid=(t // tt,),
        in_specs=_mix_specs(tt, hw, lo_blk),
        out_specs=[_compact_spec(tt, q)] * 6,
        out_shape=[jax.ShapeDtypeStruct((t * q, LANE), F32)] * 6,
        compiler_params=_cparams(("parallel",)),
    )(proj, proj, proj, proj, mu, mu_lo, w0, a0, k_k, k_a, wup_w, wup_a)


def _rwkv_post_math(y, r, kh, v, g, lnx_g, lnx_b, r_k, nh, n):
    mean = _segsum(y, nh) * (1.0 / n)
    yc = y - mean
    var = _segsum(yc * yc, nh) * (1.0 / n)
    rstd = lax.rsqrt(var + GN_EPS_PER_CHANNEL * n)
    yn = yc * rstd
    s = _segsum(r * kh * r_k, nh)
    y3 = yn * lnx_g + lnx_b + s * v
    sg = _sigmoid(g)
    return yn, rstd, s, y3, sg


def _rwkv_post(y, r, kh, v, proj, lnx_g, lnx_b, r_k, hw, n, gate_blk):
    t = proj.shape[0]
    tt = _tile(t, RWKV_ROWS, SUBLANE)
    q = hw // LANE
    nh = hw // n

    def body(y_ref, r_ref, kh_ref, v_ref, g_ref, lg_ref, lb_ref, rk_ref, o_ref):
        g = g_ref[...]
        y, r, kh, v = (_load_compact(ref, tt, q) for ref in (y_ref, r_ref, kh_ref, v_ref))
        _, _, _, y3, sg = _rwkv_post_math(y, r, kh, v, g, lg_ref[...], lb_ref[...], rk_ref[...], nh, n)
        o_ref[...] = (y3 * (g * sg)).astype(o_ref.dtype)

    row = pl.BlockSpec((tt, hw), lambda i: (i, 0))
    comp = _compact_spec(tt, q)
    vec = _full((1, hw))
    return pl.pallas_call(
        body, name="rwkv_post", grid=(t // tt,),
        in_specs=[comp, comp, comp, comp, pl.BlockSpec((tt, hw), lambda i: (i, gate_blk)), vec, vec, vec],
        out_specs=row,
        out_shape=jax.ShapeDtypeStruct((t, hw), MXU_DTYPE),
        compiler_params=_cparams(("parallel",)),
    )(y, r, kh, v, proj, lnx_g, lnx_b, r_k)


def _rwkv_post_bwd(dmix, y, r, kh, v, proj, lnx_g, lnx_b, r_k, hw, n, gate_blk, dproj):
    t = proj.shape[0]
    tt = _tile(t, RWKV_ROWS, SUBLANE)
    q = hw // LANE
    nh = hw // n

    def body(dm_ref, y_ref, r_ref, kh_ref, v_ref, g_ref, lg_ref, lb_ref, rk_ref, dproj_ref,
             dg_o, dy_o, dr_o, dkh_o, dv_o, dlg_o, dlb_o, drk_o):
        i = pl.program_id(0)
        g, rk, lg = g_ref[...], rk_ref[...], lg_ref[...]
        y, r, kh, v = (_load_compact(ref, tt, q) for ref in (y_ref, r_ref, kh_ref, v_ref))
        yn, rstd, s, y3, sg = _rwkv_post_math(y, r, kh, v, g, lg, lb_ref[...], rk, nh, n)
        dyr = dm_ref[...]
        dy3 = dyr * (g * sg)
        dg_o[...] = (dyr * y3 * (sg * (1.0 + g * (1.0 - sg)))).astype(dg_o.dtype)
        ds = _segsum(dy3 * v, nh)
        _store_compact(dv_o, dy3 * s, tt, q)
        _store_compact(dr_o, ds * kh * rk, tt, q)
        _store_compact(dkh_o, ds * r * rk, tt, q)
        dyn = dy3 * lg
        m1 = _segsum(dyn, nh) * (1.0 / n)
        m2 = _segsum(dyn * yn, nh) * (1.0 / n)
        _store_compact(dy_o, rstd * (dyn - m1 - yn * m2), tt, q)

        @pl.when(i == 0)
        def _():
            dlg_o[...] = jnp.zeros_like(dlg_o)
            dlb_o[...] = jnp.zeros_like(dlb_o)
            drk_o[...] = jnp.zeros_like(drk_o)

        dlg_o[...] += jnp.sum(dy3 * yn, axis=0, keepdims=True)
        dlb_o[...] += jnp.sum(dy3, axis=0, keepdims=True)
        drk_o[...] += jnp.sum(ds * r * kh, axis=0, keepdims=True)

    row = pl.BlockSpec((tt, hw), lambda i: (i, 0))
    comp = _compact_spec(tt, q)
    vec = _full((1, hw))
    rowf = jax.ShapeDtypeStruct((t * q, LANE), F32)
    vecf = jax.ShapeDtypeStruct((1, hw), F32)
    return pl.pallas_call(
        body, name="rwkv_post_bwd", grid=(t // tt,),
        in_specs=[row, comp, comp, comp, comp, pl.BlockSpec((tt, hw), lambda i: (i, gate_blk)), vec, vec, vec,
                  ANY_SPEC],
        out_specs=[pl.BlockSpec((tt, hw), lambda i: (i, gate_blk)), comp, comp, comp, comp, vec, vec, vec],
        out_shape=[jax.ShapeDtypeStruct(dproj.shape, dproj.dtype), rowf, rowf, rowf, rowf, vecf, vecf, vecf],
        input_output_aliases={9: 0},
        compiler_params=_cparams(("arbitrary",)),
    )(dmix, y, r, kh, v, proj, lnx_g, lnx_b, r_k, dproj)


def _rwkv_pre_bwd(proj, mu, mu_lo, w0, a0, k_k, k_a, wup_w, wup_a, hw, nh, lo_blk,
                  dr_rec, dw_rec, dkh_rec, dv_rec, dkn_rec, db_rec, dr_bon, dkh_bon, dv_bon):
    t = proj.shape[0]
    mw, lw2 = 3 * hw, 2 * LORA_PAD
    tt = _tile(t, RWKV_ROWS, SUBLANE)
    q = hw // LANE
    n_in = 12

    def body(*refs):
        mix_refs = refs[:n_in]
        drr, dwr, dkhr, dvr, dknr, db, drb, dkhb, dvb = (
            _load_compact(ref, tt, q) for ref in refs[n_in:n_in + 9])
        dxs_o, dxl_o, dmu_o, dmul_o, dw0_o, da0_o, dkk_o, dka_o, dwupw_o, dwupa_o = refs[n_in + 9:]
        kk_ref, ka_ref, wupw_ref, wupa_ref = mix_refs[8:12]
        i = pl.program_id(0)
        f = _rwkv_mix(*mix_refs, hw, nh, i == 0)
        k, a, kk, nrm = f["k"], f["a"], f["kk"], f["nrm"]
        k_a, k_k = ka_ref[...], kk_ref[...]
        dr = drr + drb
        dkh = dkhr + dkhb
        dv = dvr + dvb
        da = db * kk + dkh * k * k_a
        dkk = db * a - dknr
        dk = dkh * (1.0 + (a - 1.0) * k_a)
        dka = jnp.sum(dkh * k * (a - 1.0), axis=0, keepdims=True)
        proj_kk = _segsum(dkk * kk, nh)
        dkkr = jnp.where(f["nr"] > KK_EPS, (dkk - kk * proj_kk) / nrm, dkk * (1.0 / KK_EPS))
        dk = dk + dkkr * k_k
        dkk_w = jnp.sum(dkkr * k, axis=0, keepdims=True)
        dza = da * a * (1.0 - a)
        dzw = dwr * f["decay"] * (-f["ew"]) * _sigmoid(-f["zw"])
        nt_dims = (((1,), (1,)), ((), ()))
        tn_dims = (((0,), (0,)), ((), ()))
        dal = lax.dot_general(dza, wupa_ref[...], nt_dims, preferred_element_type=F32, precision=HI)
        dth = lax.dot_general(dzw, wupw_ref[...], nt_dims, preferred_element_type=F32, precision=HI)
        dwl = dth * (1.0 - f["th"] * f["th"])
        dxs_o[:, 0:hw] = dr
        dxs_o[:, hw:2 * hw] = dk
        dxs_o[:, 2 * hw:3 * hw] = dv
        dxl_o[:, 0:LORA_PAD] = dwl
        dxl_o[:, LORA_PAD:lw2] = dal

        @pl.when(i == 0)
        def _():
            for ref in (dmu_o, dmul_o, dw0_o, da0_o, dkk_o, dka_o, dwupw_o, dwupa_o):
                ref[...] = jnp.zeros_like(ref)

        def colsum(v):
            return jnp.sum(v, axis=0, keepdims=True)

        dmu_o[:, 0:hw] += colsum(dr * f["r_d"])
        dmu_o[:, hw:2 * hw] += colsum(dk * f["k_d"])
        dmu_o[:, 2 * hw:3 * hw] += colsum(dv * f["v_d"])
        dmul_o[:, 0:LORA_PAD] += colsum(dwl * f["wl_d"])
        dmul_o[:, LORA_PAD:lw2] += colsum(dal * f["al_d"])
        dw0_o[...] += colsum(dzw)
        da0_o[...] += colsum(dza)
        dkk_o[...] += dkk_w
        dka_o[...] += dka
        dwupw_o[...] += lax.dot_general(f["th"], dzw, tn_dims, preferred_element_type=F32, precision=HI)
        dwupa_o[...] += lax.dot_general(f["al"], dza, tn_dims, preferred_element_type=F32, precision=HI)

    vec = _full((1, hw))
    vecf = jax.ShapeDtypeStruct((1, hw), F32)
    return pl.pallas_call(
        body, name="rwkv_pre_bwd", grid=(t // tt,),
        in_specs=_mix_specs(tt, hw, lo_blk) + [_compact_spec(tt, q)] * 9,
        out_specs=[pl.BlockSpec((tt, mw), lambda i: (i, 0)), pl.BlockSpec((tt, lw2), lambda i: (i, 0)),
                   _full((1, mw)), _full((1, lw2)), vec, vec, vec, vec,
                   _full((LORA_PAD, hw)), _full((LORA_PAD, hw))],
        out_shape=[jax.ShapeDtypeStruct((t, mw), F32), jax.ShapeDtypeStruct((t, lw2), F32),
                   jax.ShapeDtypeStruct((1, mw), F32), jax.ShapeDtypeStruct((1, lw2), F32),
                   vecf, vecf, vecf, vecf,
                   jax.ShapeDtypeStruct((LORA_PAD, hw), F32), jax.ShapeDtypeStruct((LORA_PAD, hw), F32)],
        compiler_params=_cparams(("arbitrary",)),
    )(proj, proj, proj, proj, mu, mu_lo, w0, a0, k_k, k_a, wup_w, wup_a,
      dr_rec, dw_rec, dkh_rec, dv_rec, dkn_rec, db_rec, dr_bon, dkh_bon, dv_bon)


def _shift_bwd(dxs, mu, name, dproj, col_blk):
    t, sw = dxs.shape
    tt = _tile(t, 256, SUBLANE)
    nblk = t // SUBLANE

    def body(d_ref, nxt_ref, mu_ref, dproj_ref, o_ref):
        last = pl.program_id(0) == pl.num_programs(0) - 1
        cur = d_ref[...]
        first_next = jnp.where(last, 0.0, nxt_ref[0:1, :])
        nxt = pltpu.roll(cur, tt - 1, 0)
        rows = lax.broadcasted_iota(jnp.int32, cur.shape, 0)
        nxt = jnp.where(rows == tt - 1, first_next, nxt)
        m = mu_ref[...]
        o_ref[...] = (cur * (1.0 - m) + nxt * m).astype(o_ref.dtype)

    return pl.pallas_call(
        body, name=name, grid=(t // tt,),
        in_specs=[pl.BlockSpec((tt, sw), lambda i: (i, 0)),
                  pl.BlockSpec((SUBLANE, sw), lambda i: (jnp.minimum((i + 1) * (tt // SUBLANE), nblk - 1), 0)),
                  _full((1, sw)), ANY_SPEC],
        out_specs=pl.BlockSpec((tt, sw), lambda i: (i, col_blk)),
        out_shape=jax.ShapeDtypeStruct(dproj.shape, dproj.dtype),
        input_output_aliases={3: 0},
        compiler_params=_cparams(("parallel",)),
    )(dxs, dxs, mu, dproj)


def _tree_sum(parts):
    while len(parts) > 1:
        parts = [parts[p] + parts[p + 1] for p in range(0, len(parts) - 1, 2)] + ([parts[-1]] if len(parts) % 2 else [])
    return parts[0]


def _tile_rows(src_ref, dst_ref, tc, nh):
    def convert(ts, carry):
        _tile_step(src_ref, dst_ref, ts, nh)
        return carry

    lax.fori_loop(0, tc, convert, 0)


def _tile_step(src_ref, dst_ref, ts, nh):
    for grp, m in enumerate(_tiled(src_ref[ts], nh)):
        dst_ref[ts, grp] = m


def _tiled(v, nh):
    rep = LANE // nh
    lane_group = lax.broadcasted_iota(jnp.int32, v.shape, 1) // nh
    rolled = [v] + [pltpu.roll(v, k * nh, 1) for k in range(1, rep)]
    out = []
    for grp in range(rep):
        m = rolled[(0 - grp) % rep]
        for g in range(1, rep):
            m = jnp.where(lane_group == g, rolled[(g - grp) % rep], m)
        out.append(m)
    return out


def _wkv_fwd(w_c, b_c, k_c, r_c, kn_c, v_c, nh):
    t, q, _ = v_c.shape
    rep = LANE // nh
    n = q * rep
    tc = _tile(t, WKV_CHUNK, 1)
    nc = t // tc
    nacc = 4
    nv = 5

    def body(*refs):
        cur, v_ref, nxt = refs[:nv], refs[nv], refs[nv + 1:2 * nv + 1]
        y_ref, sa_ref, ck_ref, s_ref = refs[2 * nv + 1:2 * nv + 5]
        tiles_even, tiles_odd = refs[2 * nv + 5:3 * nv + 5], refs[3 * nv + 5:]
        c = pl.program_id(0)

        @pl.when(c == 0)
        def _():
            s_ref[...] = jnp.zeros_like(s_ref)
            for src, dst in zip(cur, tiles_even):
                _tile_rows(src, dst, tc, nh)

        ck_ref[0] = s_ref[...]

        def row(ref, ts, j):
            return ref[ts, j % rep, pl.ds(j // rep, 1), :]

        def run(mine, ahead_tiles):
            wt, bt, kt, rt, knt = mine

            def step(ts, carry):
                for src, dst in zip(nxt, ahead_tiles):
                    _tile_step(src, dst, ts, nh)
                vt = v_ref[ts]
                acc = [None] * nacc
                for j in range(n):
                    term = s_ref[j] * row(knt, ts, j)
                    acc[j % nacc] = term if acc[j % nacc] is None else acc[j % nacc] + term
                sa = _tree_sum(acc)
                sa_ref[ts] = sa
                acc = [None] * nacc
                for j in range(n):
                    sj = s_ref[j] * row(wt, ts, j) + sa * row(bt, ts, j) + vt * row(kt, ts, j)
                    s_ref[j] = sj
                    term = sj * row(rt, ts, j)
                    acc[j % nacc] = term if acc[j % nacc] is None else acc[j % nacc] + term
                y_ref[ts] = _tree_sum(acc)
                return carry

            lax.fori_loop(0, tc, step, 0)

        @pl.when(c % 2 == 0)
        def _():
            run(tiles_even, tiles_odd)

        @pl.when(c % 2 == 1)
        def _():
            run(tiles_odd, tiles_even)

    comp = pl.BlockSpec((tc, q, LANE), lambda c: (c, 0, 0))
    ahead = pl.BlockSpec((tc, q, LANE), lambda c: (jnp.minimum(c + 1, nc - 1), 0, 0))
    return pl.pallas_call(
        body, name="wkv_fwd", grid=(nc,),
        in_specs=[comp] * (nv + 1) + [ahead] * nv,
        out_specs=[comp, comp, pl.BlockSpec((1, n, q, LANE), lambda c: (c, 0, 0, 0))],
        out_shape=[jax.ShapeDtypeStruct((t, q, LANE), F32), jax.ShapeDtypeStruct((t, q, LANE), F32),
                   jax.ShapeDtypeStruct((nc, n, q, LANE), F32)],
        scratch_shapes=[pltpu.VMEM((n, q, LANE), F32)] + [pltpu.VMEM((tc, rep, q, LANE), F32)] * (2 * nv),
        compiler_params=_cparams(("arbitrary",)),
    )(w_c, b_c, k_c, r_c, kn_c, v_c, w_c, b_c, k_c, r_c, kn_c)


def _wkv_bwd(r_c, w_c, b_c, k_c, kn_c, dy_c, sa_c, v_c, ck_i, nh, exchange=()):
    t, q, _ = dy_c.shape
    rep = LANE // nh
    n = q * rep
    tc = _tile(t, WKV_CHUNK, 1)
    nc = t // tc
    nacc = 2
    ne = len(exchange)
    nv = 8
    n_in = 2 * nv + 1 + ne

    def body(*refs):
        cur, ck_ref, nxt = refs[:nv], refs[nv], refs[nv + 1:2 * nv + 1]
        rc_ref, wc_ref, bc_ref, kc_ref, knc_ref, dyc_ref = cur[:6]
        parts = refs[2 * nv + 1:n_in]
        dv_o, dr_o, dw_o, db_o, dk_o, dkn_o = refs[n_in:n_in + 6]
        recvs = refs[n_in + 6:n_in + 6 + ne]
        hist, g_ref, gp_ref, dsat_ref = refs[n_in + 6 + ne:n_in + 10 + ne]
        tiles_even = refs[n_in + 10 + ne:n_in + 10 + ne + nv]
        tiles_odd = refs[n_in + 10 + ne + nv:n_in + 10 + ne + 2 * nv]
        c = pl.program_id(0)
        if ne:
            copies = _exchange_copies(parts, recvs, *refs[n_in + 10 + ne + 2 * nv:])

            @pl.when(c == 0)
            def _():
                for cp in copies:
                    cp.start()

        @pl.when(c == 0)
        def _():
            g_ref[...] = jnp.zeros_like(g_ref)
            gp_ref[...] = jnp.zeros_like(gp_ref)
            for src, dst in zip(cur, tiles_even):
                _tile_rows(src, dst, tc, nh)

        def row(ref, ts, idx):
            return ref[ts, idx % rep, pl.ds(idx // rep, 1), :]

        hist[0] = ck_ref[0]

        def run(mine, ahead_tiles):
            rt, wt, bt, kt, knt, dyt, sat, vt = mine

            def fstep(ts, carry):
                wv, bv, kv = wc_ref[ts], bc_ref[ts], kc_ref[ts]
                for i in range(n):
                    hist[ts + 1, i] = hist[ts, i] * wv + row(sat, ts, i) * bv + row(vt, ts, i) * kv
                return carry

            lax.fori_loop(0, tc, fstep, 0)

            def bstep(s, carry):
                ts = tc - 1 - s
                for src, dst in zip(nxt, ahead_tiles):
                    _tile_step(src, dst, ts, nh)
                dy = dyc_ref[ts]
                acc_sa, acc_v = [None] * nacc, [None] * nacc
                for j in range(n):
                    gj = g_ref[j] + dy * row(rt, ts, j)
                    g_ref[j] = gj
                    t1 = gj * row(bt, ts, j)
                    t2 = gj * row(kt, ts, j)
                    a = j % nacc
                    acc_sa[a] = t1 if acc_sa[a] is None else acc_sa[a] + t1
                    acc_v[a] = t2 if acc_v[a] is None else acc_v[a] + t2
                dsa = _tree_sum(acc_sa)
                dv_o[ts] = _tree_sum(acc_v)
                for j in range(n):
                    g_ref[j] = g_ref[j] * row(wt, ts, j) + dsa * row(knt, ts, j)
                for grp, m in enumerate(_tiled(dsa, nh)):
                    dsat_ref[grp] = m
                rv, wv, knv = rc_ref[ts], wc_ref[ts], knc_ref[ts]
                names = ("dr", "dw", "db", "dk", "dkn")
                accs = {nm: [None] * nacc for nm in names}
                for i in range(n):
                    dsai = dsat_ref[i % rep, pl.ds(i // rep, 1), :]
                    dyi = row(dyt, ts, i)
                    s_prev = hist[ts, i]
                    gi = gp_ref[i] + dyi * rv
                    terms = {"dr": hist[ts + 1, i] * dyi, "dw": gi * s_prev, "db": gi * row(sat, ts, i),
                             "dk": gi * row(vt, ts, i), "dkn": dsai * s_prev}
                    a = i % nacc
                    for nm in names:
                        accs[nm][a] = terms[nm] if accs[nm][a] is None else accs[nm][a] + terms[nm]
                    gp_ref[i] = gi * wv + dsai * knv
                dr_o[ts] = _tree_sum(accs["dr"])
                dw_o[ts] = _tree_sum(accs["dw"])
                db_o[ts] = _tree_sum(accs["db"])
                dk_o[ts] = _tree_sum(accs["dk"])
                dkn_o[ts] = _tree_sum(accs["dkn"])
                return carry

            lax.fori_loop(0, tc, bstep, 0)

        @pl.when(c % 2 == 0)
        def _():
            run(tiles_even, tiles_odd)

        @pl.when(c % 2 == 1)
        def _():
            run(tiles_odd, tiles_even)

        if ne:
            @pl.when(c == nc - 1)
            def _():
                for cp in copies:
                    cp.wait()

    comp = pl.BlockSpec((tc, q, LANE), lambda c: (nc - 1 - c, 0, 0))
    ahead = pl.BlockSpec((tc, q, LANE), lambda c: (jnp.maximum(nc - 2 - c, 0), 0, 0))
    outc = jax.ShapeDtypeStruct((t, q, LANE), F32)
    vectors = (r_c, w_c, b_c, k_c, kn_c, dy_c, sa_c, v_c)
    return pl.pallas_call(
        body, name="wkv_bwd", grid=(nc,),
        in_specs=[comp] * nv + [pl.BlockSpec((1, n, q, LANE), lambda c: (nc - 1 - c, 0, 0, 0))] + [ahead] * nv
        + [HBM_SPEC] * ne,
        out_specs=[comp] * 6 + [HBM_SPEC] * ne,
        out_shape=[outc] * 6 + _exchange_shapes(exchange),
        scratch_shapes=[pltpu.VMEM((tc + 1, n, q, LANE), F32), pltpu.VMEM((n, q, LANE), F32),
                        pltpu.VMEM((n, q, LANE), F32), pltpu.VMEM((rep, q, LANE), F32)]
        + [pltpu.VMEM((tc, rep, q, LANE), F32)] * (2 * nv) + _exchange_sems(ne),
        compiler_params=_cparams(("arbitrary",)),
    )(*vectors, ck_i, *vectors, *exchange)


def _shift_copies(ext_ref, sh_ref):
    rows = ext_ref.shape[0] - SUBLANE
    for p in range(1, SUBLANE):
        sh_ref[p - 1, 0:rows, :] = ext_ref[p:p + rows, :]


def _window(ext_ref, sh_ref, start, size):
    p = start % SUBLANE
    if p == 0:
        return ext_ref[start:start + size, :]
    assert start - p + size <= ext_ref.shape[0] - SUBLANE
    return sh_ref[p - 1, start - p:start - p + size, :]


def _conv_stage(gv_ref, gg_ref, gvh_ref, ggh_ref, cw_ref, cb_ref, lg_ref, lb_ref, ext_ref, sh_ref, first, tt, taps):
    u = gv_ref[...] * _sigmoid(gg_ref[...])
    uh = jnp.where(first, 0.0, gvh_ref[...] * _sigmoid(ggh_ref[...]))
    ext_ref[0:CONV_HALO, :] = uh
    ext_ref[CONV_HALO:CONV_HALO + tt, :] = u
    _shift_copies(ext_ref, sh_ref)
    off = CONV_HALO - (taps - 1)
    c = cb_ref[...] + _window(ext_ref, sh_ref, off, tt) * cw_ref[0:1, :]
    for j in range(1, taps):
        c = c + _window(ext_ref, sh_ref, off + j, tt) * cw_ref[j:j + 1, :]
    mean = jnp.mean(c, axis=-1, keepdims=True)
    cc = c - mean
    rstd = lax.rsqrt(jnp.mean(cc * cc, axis=-1, keepdims=True) + LN_EPS)
    chat = cc * rstd
    cn = chat * lg_ref[...] + lb_ref[...]
    return chat, rstd, cn


def _conv_specs(t, tt, cw, taps, gv_blk, gg_blk):
    hb = tt // CONV_HALO
    return [pl.BlockSpec((tt, cw), lambda i: (i, gv_blk)), pl.BlockSpec((tt, cw), lambda i: (i, gg_blk)),
            pl.BlockSpec((CONV_HALO, cw), lambda i: (jnp.maximum(i * hb - 1, 0), gv_blk)),
            pl.BlockSpec((CONV_HALO, cw), lambda i: (jnp.maximum(i * hb - 1, 0), gg_blk)),
            _full((taps, cw)), _full((1, cw)), _full((1, cw)), _full((1, cw))]


def _conv_fwd(proj, conv_w, conv_b, cln_g, cln_b, cw, gv_blk, gg_blk):
    t = proj.shape[0]
    taps = conv_w.shape[0]
    tt = _tile(t, 128, CONV_HALO)

    def body(gv_ref, gg_ref, gvh_ref, ggh_ref, cw_ref, cb_ref, lg_ref, lb_ref, o_ref, ext_ref, sh_ref):
        _, _, cn = _conv_stage(gv_ref, gg_ref, gvh_ref, ggh_ref, cw_ref, cb_ref, lg_ref, lb_ref, ext_ref, sh_ref,
                               pl.program_id(0) == 0, tt, taps)
        o_ref[...] = (cn * _sigmoid(cn)).astype(o_ref.dtype)

    return pl.pallas_call(
        body, name="conv_fwd", grid=(t // tt,),
        in_specs=_conv_specs(t, tt, cw, taps, gv_blk, gg_blk),
        out_specs=pl.BlockSpec((tt, cw), lambda i: (i, 0)),
        out_shape=jax.ShapeDtypeStruct((t, cw), MXU_DTYPE),
        scratch_shapes=[pltpu.VMEM((CONV_HALO + tt, cw), F32), pltpu.VMEM((SUBLANE - 1, CONV_HALO + tt, cw), F32)],
        compiler_params=_cparams(("parallel",)),
    )(proj, proj, proj, proj, conv_w, conv_b, cln_g, cln_b)


def _conv_gate(c2, proj, b_pw2, cw, gc_blk):
    t = c2.shape[0]
    tt = _tile(t, 256, SUBLANE)

    def body(c_ref, g_ref, b_ref, o_ref):
        g = g_ref[...]
        o_ref[...] = ((c_ref[...] + b_ref[...]) * (g * _sigmoid(g))).astype(o_ref.dtype)

    return pl.pallas_call(
        body, name="conv_gate", grid=(t // tt,),
        in_specs=[pl.BlockSpec((tt, cw), lambda i: (i, 0)), pl.BlockSpec((tt, cw), lambda i: (i, gc_blk)),
                  _full((1, cw))],
        out_specs=pl.BlockSpec((tt, cw), lambda i: (i, 0)),
        out_shape=jax.ShapeDtypeStruct((t, cw), MXU_DTYPE),
        compiler_params=_cparams(("parallel",)),
    )(c2, proj, b_pw2)


def _conv_gate_bwd(dmix, c2, proj, b_pw2, cw, dm_blk, gc_blk):
    t = c2.shape[0]
    tt = _tile(t, 256, SUBLANE)

    def body(dm_ref, c_ref, g_ref, b_ref, dc2_o, dg_o, db_o):
        g = g_ref[...]
        sg = _sigmoid(g)
        dyc = dm_ref[...]
        dc2 = dyc * (g * sg)
        dc2_o[...] = dc2.astype(dc2_o.dtype)
        dg_o[...] = (dyc * (c_ref[...] + b_ref[...]) * (sg * (1.0 + g * (1.0 - sg)))).astype(dg_o.dtype)

        @pl.when(pl.program_id(0) == 0)
        def _():
            db_o[...] = jnp.zeros_like(db_o)

        db_o[...] += jnp.sum(dc2, axis=0, keepdims=True)

    row = pl.BlockSpec((tt, cw), lambda i: (i, 0))
    return pl.pallas_call(
        body, name="conv_gate_bwd", grid=(t // tt,),
        in_specs=[pl.BlockSpec((tt, cw), lambda i: (i, dm_blk)), row,
                  pl.BlockSpec((tt, cw), lambda i: (i, gc_blk)), _full((1, cw))],
        out_specs=[row, pl.BlockSpec((tt, cw), lambda i: (i, gc_blk)), _full((1, cw))],
        out_shape=[jax.ShapeDtypeStruct((t, cw), MXU_DTYPE), jax.ShapeDtypeStruct((t, proj.shape[1]), MXU_DTYPE),
                   jax.ShapeDtypeStruct((1, cw), F32)],
        compiler_params=_cparams(("arbitrary",)),
    )(dmix, c2, proj, b_pw2)


def _conv_bwd_norm(proj, dcs, conv_w, conv_b, cln_g, cln_b, cw, gv_blk, gg_blk):
    t = proj.shape[0]
    taps = conv_w.shape[0]
    tt = _tile(t, 128, CONV_HALO)

    def body(gv_ref, gg_ref, gvh_ref, ggh_ref, cw_ref, cb_ref, lg_ref, lb_ref, dcs_ref,
             dc_o, dcw_o, dcb_o, dlg_o, dlb_o, ext_ref, sh_ref):
        chat, rstd, cn = _conv_stage(gv_ref, gg_ref, gvh_ref, ggh_ref, cw_ref, cb_ref, lg_ref, lb_ref, ext_ref,
                                     sh_ref, pl.program_id(0) == 0, tt, taps)
        s = _sigmoid(cn)
        dcn = dcs_ref[...] * (s * (1.0 + cn * (1.0 - s)))
        dchat = dcn * lg_ref[...]
        dc = rstd * (dchat - jnp.mean(dchat, axis=-1, keepdims=True)
                     - chat * jnp.mean(dchat * chat, axis=-1, keepdims=True))
        dc_o[...] = dc

        @pl.when(pl.program_id(0) == 0)
        def _():
            for ref in (dcw_o, dcb_o, dlg_o, dlb_o):
                ref[...] = jnp.zeros_like(ref)

        dlg_o[...] += jnp.sum(dcn * chat, axis=0, keepdims=True)
        dlb_o[...] += jnp.sum(dcn, axis=0, keepdims=True)
        dcb_o[...] += jnp.sum(dc, axis=0, keepdims=True)
        off = CONV_HALO - (taps - 1)
        for j in range(taps):
            dcw_o[j:j + 1, :] += jnp.sum(_window(ext_ref, sh_ref, off + j, tt) * dc, axis=0, keepdims=True)

    vec = _full((1, cw))
    vecf = jax.ShapeDtypeStruct((1, cw), F32)
    return pl.pallas_call(
        body, name="conv_bwd_norm", grid=(t // tt,),
        in_specs=_conv_specs(t, tt, cw, taps, gv_blk, gg_blk) + [pl.BlockSpec((tt, cw), lambda i: (i, 0))],
        out_specs=[pl.BlockSpec((tt, cw), lambda i: (i, 0)), _full((taps, cw)), vec, vec, vec],
        out_shape=[jax.ShapeDtypeStruct((t, cw), F32), jax.ShapeDtypeStruct((taps, cw), F32), vecf, vecf, vecf],
        scratch_shapes=[pltpu.VMEM((CONV_HALO + tt, cw), F32), pltpu.VMEM((SUBLANE - 1, CONV_HALO + tt, cw), F32)],
        compiler_params=_cparams(("arbitrary",)),
    )(proj, proj, proj, proj, conv_w, conv_b, cln_g, cln_b, dcs)


def _conv_bwd_glu(dc, proj, conv_w, cw, gv_blk, gg_blk, dproj):
    t = dc.shape[0]
    taps = conv_w.shape[0]
    tt = _tile(t, 128, CONV_HALO)
    hb = tt // CONV_HALO
    nhalo = t // CONV_HALO
    assert gg_blk == gv_blk + 1 and gv_blk % 2 == 0

    def body(dc_ref, dch_ref, gv_ref, gg_ref, cw_ref, dproj_ref, dg_o, ext_ref, sh_ref):
        last = pl.program_id(0) == pl.num_programs(0) - 1
        ext_ref[0:tt, :] = dc_ref[...]
        ext_ref[tt:tt + CONV_HALO, :] = jnp.where(last, 0.0, dch_ref[...])
        _shift_copies(ext_ref, sh_ref)
        du = _window(ext_ref, sh_ref, taps - 1, tt) * cw_ref[0:1, :]
        for j in range(1, taps):
            du = du + _window(ext_ref, sh_ref, taps - 1 - j, tt) * cw_ref[j:j + 1, :]
        sg = _sigmoid(gg_ref[...])
        dg_o[:, 0:cw] = (du * sg).astype(dg_o.dtype)
        dg_o[:, cw:2 * cw] = (du * gv_ref[...] * sg * (1.0 - sg)).astype(dg_o.dtype)

    row = pl.BlockSpec((tt, cw), lambda i: (i, 0))
    return pl.pallas_call(
        body, name="conv_bwd_glu", grid=(t // tt,),
        in_specs=[row, pl.BlockSpec((CONV_HALO, cw), lambda i: (jnp.minimum((i + 1) * hb, nhalo - 1), 0)),
                  pl.BlockSpec((tt, cw), lambda i: (i, gv_blk)), pl.BlockSpec((tt, cw), lambda i: (i, gg_blk)),
                  _full((taps, cw)), ANY_SPEC],
        out_specs=pl.BlockSpec((tt, 2 * cw), lambda i: (i, gv_blk // 2)),
        out_shape=jax.ShapeDtypeStruct(dproj.shape, dproj.dtype),
        input_output_aliases={5: 0},
        scratch_shapes=[pltpu.VMEM((tt + CONV_HALO, cw), F32), pltpu.VMEM((SUBLANE - 1, tt + CONV_HALO, cw), F32)],
        compiler_params=_cparams(("parallel",)),
    )(dc, dc, proj, proj, conv_w, dproj)


HBM_SPEC = pl.BlockSpec(memory_space=pltpu.HBM)
ANY_SPEC = pl.BlockSpec(memory_space=pl.ANY)


def _gather_plan(ins, outs, send_sems, recv_sems, local_sems):
    na = len(ins)
    x, y, c = lax.axis_index("x"), lax.axis_index("y"), lax.axis_index("c")
    me, sibling = (x, y, c), (x, y, 1 - c)
    chips = [(1 - x, y), (x, 1 - y), (1 - x, 1 - y)]

    def slot(px, py, pc):
        return 4 * px + 2 * py + pc

    def copy(a, k, block, to, src=None):
        dst = outs[a].at[slot(*block)]
        return pltpu.make_async_remote_copy(
            src_ref=dst if src is None else src, dst_ref=dst,
            send_sem=send_sems.at[a, k], recv_sem=recv_sems.at[a, k],
            device_id=to, device_id_type=MESH)

    mine = [pltpu.make_async_copy(ins[a], outs[a].at[slot(*me)], local_sems.at[a]) for a in range(na)]
    first = []
    for a in range(na):
        first.append(copy(a, 0, me, sibling, src=ins[a]))
        first += [copy(a, 1 + j, me, (*chip, c), src=ins[a]) for j, chip in enumerate(chips)]

    def finish():
        passed = []
        for j, chip in enumerate(chips):
            for a in range(na):
                copy(a, 1 + j, (*chip, c), me).wait_recv()
                fwd = copy(a, 4 + j, (*chip, c), sibling)
                fwd.start()
                passed.append(fwd)
        for a in range(na):
            copy(a, 0, sibling, me).wait_recv()
            for j, chip in enumerate(chips):
                copy(a, 4 + j, (*chip, 1 - c), me).wait_recv()
        for cp in first + passed:
            cp.wait_send()
        for cp in mine:
            cp.wait()

    return mine + first, finish


def _gather_shapes(shards):
    return [jax.ShapeDtypeStruct((N_DEV,) + s.shape, s.dtype) for s in shards]


def _gather_sems(na):
    if not na:
        return []
    return [pltpu.SemaphoreType.DMA((na, N_DEV - 1)), pltpu.SemaphoreType.DMA((na, N_DEV - 1)),
            pltpu.SemaphoreType.DMA((na,))]


def _all_gather(shards, name):
    na = len(shards)

    def body(*refs):
        starts, finish = _gather_plan(refs[:na], refs[na:2 * na], *refs[2 * na:])
        for cp in starts:
            cp.start()
        finish()

    return pl.pallas_call(
        body, name=name,
        in_specs=[HBM_SPEC] * na, out_specs=[HBM_SPEC] * na,
        out_shape=_gather_shapes(shards), scratch_shapes=_gather_sems(na),
        compiler_params=pltpu.CompilerParams(has_side_effects=True),
    )(*shards)


N_CHIPS = N_DEV // 2
EXCHANGE_SLOTS = {"direct": N_DEV - 1, "sibling": N_CHIPS, "chips": N_CHIPS - 1}


def _exchange_copies(ins, outs, send_sems, recv_sems, pattern="direct"):
    x, y, c = lax.axis_index("x"), lax.axis_index("y"), lax.axis_index("c")
    copies = []

    def add(a, src_slot, dst_slot, sem, peer):
        copies.append(pltpu.make_async_remote_copy(
            src_ref=ins[a].at[src_slot], dst_ref=outs[a].at[dst_slot],
            send_sem=send_sems.at[a, sem], recv_sem=recv_sems.at[a, sem],
            device_id=peer, device_id_type=MESH))

    for a in range(len(ins)):
        if pattern == "direct":
            for k in range(1, N_DEV):
                px = 1 - x if k & 4 else x
                py = 1 - y if k & 2 else y
                pc = 1 - c if k & 1 else c
                add(a, 4 * px + 2 * py + pc, k - 1, k - 1, (px, py, pc))
        elif pattern == "sibling":
            for j in range(N_CHIPS):
                add(a, j, j, j, (x, y, 1 - c))
        else:
            for k in range(1, N_CHIPS):
                px = 1 - x if k & 2 else x
                py = 1 - y if k & 1 else y
                add(a, 2 * px + py, k - 1, k - 1, (px, py, c))
    return copies


def _exchange_shapes(parts, pattern="direct"):
    return [jax.ShapeDtypeStruct((EXCHANGE_SLOTS[pattern],) + p.shape[1:], p.dtype) for p in parts]


def _exchange_sems(na, pattern="direct"):
    if not na:
        return []
    return [pltpu.SemaphoreType.DMA((na, EXCHANGE_SLOTS[pattern]))] * 2


def _exchange(parts, name, pattern):
    na = len(parts)

    def body(*refs):
        copies = _exchange_copies(refs[:na], refs[na:2 * na], *refs[2 * na:], pattern=pattern)
        for cp in copies:
            cp.start()
        for cp in copies:
            cp.wait()

    return pl.pallas_call(
        body, name=name,
        in_specs=[HBM_SPEC] * na, out_specs=[HBM_SPEC] * na,
        out_shape=_exchange_shapes(parts, pattern),
        scratch_shapes=_exchange_sems(na, pattern),
        compiler_params=pltpu.CompilerParams(has_side_effects=True),
    )(*parts)


def _chip_sum(mine, theirs):
    nslot, r, c = mine.shape
    rows = nslot * r
    tr = _tile(rows, 256, SUBLANE)

    def body(a_ref, b_ref, f_ref, w_ref):
        s = a_ref[...].astype(F32) + b_ref[...].astype(F32)
        f_ref[...] = s
        w_ref[...] = s.astype(w_ref.dtype)

    blk = pl.BlockSpec((tr, c), lambda i: (i, 0))
    f, w = pl.pallas_call(
        body, name="chip_sum", grid=(rows // tr,),
        in_specs=[blk, blk], out_specs=[blk, blk],
        out_shape=[jax.ShapeDtypeStruct((rows, c), F32), jax.ShapeDtypeStruct((rows, c), theirs.dtype)],
        compiler_params=_cparams(("parallel",)),
    )(mine.reshape(rows, c), theirs.reshape(rows, c))
    return f.reshape(mine.shape), w.reshape(mine.shape)


def _cast(v, dtype, name):
    r, c = v.shape
    tr = _tile(r, 256, SUBLANE)

    def body(i_ref, o_ref):
        o_ref[...] = i_ref[...].astype(o_ref.dtype)

    return pl.pallas_call(
        body, name=name, grid=(r // tr,),
        in_specs=[pl.BlockSpec((tr, c), lambda i: (i, 0))],
        out_specs=pl.BlockSpec((tr, c), lambda i: (i, 0)),
        out_shape=jax.ShapeDtypeStruct((r, c), dtype),
        compiler_params=_cparams(("parallel",)),
    )(v)


def _adamw(w, m, v, recv, own, name):
    r, c = w.shape
    ns = recv.shape[0]
    tr = _tile(r, 128, SUBLANE)
    c1 = 1.0 - ADAM_B1 ** ADAM_STEP
    c2 = 1.0 - ADAM_B2 ** ADAM_STEP

    def body(*refs):
        if own is None:
            w_ref, m_ref, v_ref, rc_ref = refs[:4]
            g = rc_ref[0].astype(F32)
            start = 1
        else:
            w_ref, m_ref, v_ref, rc_ref, own_ref = refs[:5]
            g = own_ref[...]
            start = 0
        g_o, d_o, m_o, v_o = refs[-4:]
        for s in range(start, ns):
            g = g + rc_ref[s].astype(F32)
        mn = ADAM_B1 * m_ref[...] + (1.0 - ADAM_B1) * g
        vn = ADAM_B2 * v_ref[...] + (1.0 - ADAM_B2) * (g * g)
        m_hat = mn / c1
        v_hat = vn / c2
        g_o[...] = g
        d_o[...] = -ADAM_LR * (m_hat / (jnp.sqrt(v_hat) + ADAM_EPS) + ADAM_WD * w_ref[...])
        m_o[...] = mn
        v_o[...] = vn

    row = pl.BlockSpec((tr, c), lambda i: (i, 0))
    ins = [w, m, v, recv] + ([] if own is None else [own])
    in_specs = [row, row, row, pl.BlockSpec((ns, tr, c), lambda i: (0, i, 0))] + ([] if own is None else [row])
    return pl.pallas_call(
        body, name=name, grid=(r // tr,),
        in_specs=in_specs, out_specs=[row] * 4,
        out_shape=[jax.ShapeDtypeStruct((r, c), F32)] * 4,
        compiler_params=_cparams(("parallel",)),
    )(*ins)


def _to_t(v, nh, n, axis=-1):
    v = jnp.moveaxis(v, axis, -1)
    v = v.reshape(v.shape[:-1] + (nh, n)).swapaxes(-1, -2).reshape(v.shape)
    return jnp.moveaxis(v, -1, axis)


def _from_t(v, nh, n, axis=-1):
    v = jnp.moveaxis(v, axis, -1)
    v = v.reshape(v.shape[:-1] + (n, nh)).swapaxes(-1, -2).reshape(v.shape)
    return jnp.moveaxis(v, -1, axis)


def _pad_to(v, size, axis):
    pad = [(0, 0)] * v.ndim
    pad[axis] = (0, size - v.shape[axis])
    return jnp.pad(v, pad)


def kernel(x, norm_pre_g, w_in, mu_shift, w0, w_lora_up, a0, a_lora_up, k_k, k_a, r_k, lnx_g, lnx_b, conv_w, conv_b, cln_g, cln_b, w_pw2, b_pw2, w_out, norm_post_g, loss_target, m_norm_pre_g, m_w_in, m_mu_shift, m_w0, m_w_lora_up, m_a0, m_a_lora_up, m_k_k, m_k_a, m_r_k, m_lnx_g, m_lnx_b, m_conv_w, m_conv_b, m_cln_g, m_cln_b, m_w_pw2, m_b_pw2, m_w_out, m_norm_post_g, v_norm_pre_g, v_w_in, v_mu_shift, v_w0, v_w_lora_up, v_a0, v_a_lora_up, v_k_k, v_k_a, v_r_k, v_lnx_g, v_lnx_b, v_conv_w, v_conv_b, v_cln_g, v_cln_b, v_w_pw2, v_b_pw2, v_w_out, v_norm_post_g):
    args = dict(locals())
    t, d = x.shape[1], x.shape[2]
    hw = w0.shape[0]
    cw = conv_b.shape[0]
    nh, n = r_k.shape
    lw, la = w_lora_up.shape[0], a_lora_up.shape[0]
    taps = conv_w.shape[0]
    in_cols = w_in.shape[1] * N_DEV
    shift_cols = 3 * hw + lw + la
    assert in_cols == shift_cols + hw + 3 * cw and hw == cw and hw % LANE == 0 and LANE % nh == 0
    assert lw <= LORA_PAD and la <= LORA_PAD and taps - 1 <= CONV_HALO and hw % (2 * LORA_PAD) == 0
    q = hw // LANE
    gate_blk, gv_blk, gg_blk, gc_blk = 3, 4, 5, 6
    lo0 = 7 * hw
    lo_blk = lo0 // (2 * LORA_PAD)
    x2, tgt2 = x[0], loss_target[0]
    row = lambda v: v.reshape(1, -1)

    (w_in_g,) = _all_gather([_cast(w_in, MXU_DTYPE, "cast_w_in")], "gather_w_in")
    w_full = w_in_g.transpose(1, 0, 2).reshape(d, in_cols)
    c0 = shift_cols
    wp = jnp.concatenate([
        _to_t(w_full[:, 0:hw], nh, n), _to_t(w_full[:, hw:2 * hw], nh, n), _to_t(w_full[:, 2 * hw:3 * hw], nh, n),
        _to_t(w_full[:, c0:c0 + hw], nh, n), w_full[:, c0 + hw:],
        _pad_to(w_full[:, 3 * hw:3 * hw + lw], LORA_PAD, 1), _pad_to(w_full[:, 3 * hw + lw:c0], LORA_PAD, 1)], axis=1)
    mu_p = row(jnp.concatenate([
        _to_t(mu_shift[0:hw], nh, n), _to_t(mu_shift[hw:2 * hw], nh, n), _to_t(mu_shift[2 * hw:3 * hw], nh, n)]))
    mu_lo = row(jnp.concatenate([
        _pad_to(mu_shift[3 * hw:3 * hw + lw], LORA_PAD, 0), _pad_to(mu_shift[3 * hw + lw:], LORA_PAD, 0)]))
    tvec = lambda v: row(_to_t(v, nh, n))
    w0_t, a0_t, kk_t, ka_t, lg_t, lb_t = tvec(w0), tvec(a0), tvec(k_k), tvec(k_a), tvec(lnx_g), tvec(lnx_b)
    rk_t = row(r_k.T)

    c3 = lambda v: v.reshape(t, q, LANE)
    c2d = lambda v: v.reshape(t * q, LANE)
    me = 4 * lax.axis_index("x") + 2 * lax.axis_index("y") + lax.axis_index("c")
    own = lambda parts: lax.dynamic_index_in_dim(parts, me, 0, keepdims=False)
    wire = lambda parts, nm: _cast(parts.reshape(-1, parts.shape[-1]), WIRE_DTYPE, nm).reshape(parts.shape)

    h = _prenorm(x2, row(norm_pre_g))
    proj, w_out_g, w_pw2_g, wup_w_g, wup_a_g, conv_w_g = _matmul(
        h, wp, "nn", F32, "mm_proj",
        gather=[_cast(w_out, MXU_DTYPE, "cast_w_out"), _cast(w_pw2, MXU_DTYPE, "cast_w_pw2"),
                w_lora_up, a_lora_up, conv_w])
    w_out_f = w_out_g.reshape(N_DEV * w_out.shape[0], d)
    w_out_p = jnp.concatenate([_to_t(w_out_f[:hw], nh, n, axis=0), w_out_f[hw:]], axis=0)
    w_pw2_f = w_pw2_g.reshape(N_DEV * w_pw2.shape[0], cw)
    wup_w = _pad_to(_to_t(wup_w_g.transpose(1, 0, 2).reshape(lw, hw), nh, n), LORA_PAD, 0)
    wup_a = _pad_to(_to_t(wup_a_g.transpose(1, 0, 2).reshape(la, hw), nh, n), LORA_PAD, 0)
    conv_w_f = conv_w_g.transpose(1, 0, 2).reshape(taps, cw)
    r_a, w_a, kh_a, v_a, kn_a, b_a = _rwkv_pre(
        proj, mu_p, mu_lo, w0_t, a0_t, kk_t, ka_t, wup_w, wup_a, hw, nh, lo_blk)
    y_c, sa_c, ck = _wkv_fwd(c3(w_a), c3(b_a), c3(kh_a), c3(r_a), c3(kn_a), c3(v_a), nh)
    y_a = c2d(y_c)
    y_rwkv = _rwkv_post(y_a, r_a, kh_a, v_a, proj, lg_t, lb_t, rk_t, hw, n, gate_blk)
    cs = _conv_fwd(proj, conv_w_f, row(conv_b), row(cln_g), row(cln_b), cw, gv_blk, gg_blk)
    c2 = _matmul(cs, w_pw2_f, "nn", F32, "mm_pw2")
    y_conv = _conv_gate(c2, proj, row(b_pw2), cw, gc_blk)
    mix = jnp.concatenate([y_rwkv, y_conv], axis=1)
    out = _matmul(mix, w_out_p, "nn", F32, "mm_out")
    dout, dy, loss_part, d_post_g = _post_loss(out, x2, tgt2, row(norm_post_g))

    dmix = _matmul(dout, w_out_p, "nt", F32, "mm_dmix")
    d_w_out_p = _matmul(mix, dout, "tn", F32, "mm_dw_out")
    dc2, dproj, d_b_pw2 = _conv_gate_bwd(dmix, c2, proj, row(b_pw2), cw, 1, gc_blk)
    dcs = _matmul(dc2, w_pw2_f, "nt", F32, "mm_dcs")
    d_w_pw2 = _matmul(cs, dc2, "tn", F32, "mm_dw_pw2")
    dc, d_conv_w, d_conv_b, d_cln_g, d_cln_b = _conv_bwd_norm(
        proj, dcs, conv_w_f, row(conv_b), row(cln_g), row(cln_b), cw, gv_blk, gg_blk)
    dproj = _conv_bwd_glu(dc, proj, conv_w_f, cw, gv_blk, gg_blk, dproj)
    dproj, dy_rec, dr_bon, dkh_bon, dv_bon, d_lg_t, d_lb_t, d_rk_t = _rwkv_post_bwd(
        dmix, y_a, r_a, kh_a, v_a, proj, lg_t, lb_t, rk_t, hw, n, gate_blk, dproj)
    ck_i = ck.reshape(ck.shape[0], n, n, nh).transpose(0, 2, 1, 3).reshape(ck.shape)
    d_w_out_f = jnp.concatenate([_from_t(d_w_out_p[:hw], nh, n, axis=0), d_w_out_p[hw:]], axis=0)
    d_w_out_parts = d_w_out_f.reshape((N_DEV,) + w_out.shape)
    d_w_pw2_parts = d_w_pw2.reshape((N_DEV,) + w_pw2.shape)
    dv_c, dr_c, dw_c, db_c, dk_c, dkn_c, recv_w_out, recv_w_pw2 = _wkv_bwd(
        c3(r_a), c3(w_a), c3(b_a), c3(kh_a), c3(kn_a), c3(dy_rec), sa_c, c3(v_a), ck_i, nh,
        exchange=[wire(d_w_out_parts, "wire_w_out"), wire(d_w_pw2_parts, "wire_w_pw2")])
    dxs, dxs_lo, d_mu_p, d_mu_lo, d_w0_t, d_a0_t, d_kk_t, d_ka_t, d_wup_w, d_wup_a = _rwkv_pre_bwd(
        proj, mu_p, mu_lo, w0_t, a0_t, kk_t, ka_t, wup_w, wup_a, hw, nh, lo_blk,
        c2d(dr_c), c2d(dw_c), c2d(dk_c), c2d(dv_c), c2d(dkn_c), c2d(db_c), dr_bon, dkh_bon, dv_bon)
    dproj = _shift_bwd(dxs, mu_p, "shift_bwd", dproj, 0)
    dproj = _shift_bwd(dxs_lo, mu_lo, "shift_bwd_lora", dproj, lo_blk)
    colparts = lambda v: v.reshape(v.shape[0], N_DEV, v.shape[1] // N_DEV).transpose(1, 0, 2)
    d_wup_w_parts = colparts(_from_t(d_wup_w[:lw], nh, n))
    d_wup_a_parts = colparts(_from_t(d_wup_a[:la], nh, n))
    d_conv_w_parts = colparts(d_conv_w)
    d_wp, recv_wup_w, recv_wup_a, recv_conv_w = _matmul(
        h, dproj, "tn", WIRE_DTYPE, "mm_dw_in", exchange=[d_wup_w_parts, d_wup_a_parts, d_conv_w_parts])

    d_w_full = jnp.concatenate([
        _from_t(d_wp[:, 0:hw], nh, n), _from_t(d_wp[:, hw:2 * hw], nh, n), _from_t(d_wp[:, 2 * hw:3 * hw], nh, n),
        d_wp[:, lo0:lo0 + lw], d_wp[:, lo0 + LORA_PAD:lo0 + LORA_PAD + la],
        _from_t(d_wp[:, 3 * hw:4 * hw], nh, n), d_wp[:, 4 * hw:lo0]], axis=1)
    sc = w_in.shape[1]
    by_core = d_w_full.reshape(d, N_CHIPS, 2, sc).transpose(2, 1, 0, 3)
    core = lax.axis_index("c")
    for_mine = lax.dynamic_index_in_dim(by_core, core, 0, keepdims=False)
    for_sibling = lax.dynamic_index_in_dim(by_core, 1 - core, 0, keepdims=False)
    (from_sibling,) = _exchange([for_sibling], "exchange_sibling", "sibling")
    chip_f32, chip_wire = _chip_sum(for_mine, from_sibling)
    dh, recv_w_in = _matmul(dproj, wp, "nt", F32, "mm_dh", exchange=[chip_wire], pattern="chips")
    own_w_in = lax.dynamic_index_in_dim(chip_f32, 2 * lax.axis_index("x") + lax.axis_index("y"), 0, keepdims=False)
    grad_x, d_pre_g = _prenorm_bwd(dh, x2, dy, row(norm_pre_g))
    d_mu = jnp.concatenate([
        _from_t(d_mu_p[0, 0:hw], nh, n), _from_t(d_mu_p[0, hw:2 * hw], nh, n), _from_t(d_mu_p[0, 2 * hw:3 * hw], nh, n),
        d_mu_lo[0, 0:lw], d_mu_lo[0, LORA_PAD:LORA_PAD + la]])
    ft = lambda v: _from_t(v[0], nh, n)
    small = {
        "norm_pre_g": d_pre_g[0], "mu_shift": d_mu, "w0": ft(d_w0_t), "a0": ft(d_a0_t), "k_k": ft(d_kk_t),
        "k_a": ft(d_ka_t), "r_k": d_rk_t[0].reshape(n, nh).T.reshape(-1), "lnx_g": ft(d_lg_t), "lnx_b": ft(d_lb_t),
        "conv_b": d_conv_b[0], "cln_g": d_cln_g[0], "cln_b": d_cln_b[0], "b_pw2": d_b_pw2[0],
        "norm_post_g": d_post_g[0]}
    small_names = list(small)
    packed = jnp.concatenate([small[k] for k in small_names] + [loss_part[0, 0:1]])
    plen = packed.shape[0]
    ppad = -(-plen // LANE) * LANE
    packed = _pad_to(packed, ppad, 0).reshape(1, ppad)

    (packed_all,) = _all_gather([packed], "gather_small")

    res = {}
    sharded = [("w_in", own_w_in, recv_w_in), ("w_out", own(d_w_out_parts), recv_w_out),
               ("w_pw2", own(d_w_pw2_parts), recv_w_pw2), ("w_lora_up", own(d_wup_w_parts), recv_wup_w),
               ("a_lora_up", own(d_wup_a_parts), recv_wup_a), ("conv_w", own(d_conv_w_parts), recv_conv_w)]
    for nm, mine, rc in sharded:
        res[nm] = _adamw(args[nm], args["m_" + nm], args["v_" + nm], rc, mine, "adamw_" + nm)
    w_small = _pad_to(jnp.concatenate([args[k].reshape(-1) for k in small_names]), ppad, 0).reshape(1, ppad)
    m_small = _pad_to(jnp.concatenate([args["m_" + k].reshape(-1) for k in small_names]), ppad, 0).reshape(1, ppad)
    v_small = _pad_to(jnp.concatenate([args["v_" + k].reshape(-1) for k in small_names]), ppad, 0).reshape(1, ppad)
    g_s, d_s, m_s, v_s = _adamw(w_small, m_small, v_small, packed_all, None, "adamw_small")
    off = 0
    for k in small_names:
        size = args[k].size
        res[k] = tuple(o[0, off:off + size].reshape(args[k].shape) for o in (g_s, d_s, m_s, v_s))
        off += size
    loss = g_s[0, plen - 1]

    order = ["norm_pre_g", "w_in", "mu_shift", "w0", "w_lora_up", "a0", "a_lora_up", "k_k", "k_a", "r_k",
             "lnx_g", "lnx_b", "conv_w", "conv_b", "cln_g", "cln_b", "w_pw2", "b_pw2", "w_out", "norm_post_g"]
    outs = [loss, grad_x[None]]
    for slot in range(4):
        outs += [res[k][slot] for k in order]
    return tuple(outs)
```
